```python
import math
import jax, jax.numpy as jnp
from jax import lax
import numpy as np

D_MODEL = 1024
BATCH = 8
SEQ = 2048
DEPTH = 4

GRID_W = 64
CTX_LEN = 256
N_MIXERS = 3
N_NA = len(range(0, DEPTH, N_MIXERS))
N_LRU = len(range(1, DEPTH, N_MIXERS))
N_SC = len(range(2, DEPTH, N_MIXERS))
NA_HEADS = 16
NA_HEAD_DIM = D_MODEL // NA_HEADS
WIN_ROWS = 8
WIN_COLS = 16
D_RNN = 128 * ((4 * D_MODEL // 3 + 64) // 128)
LRU_BLOCKS = 16
LRU_BLOCK_W = D_RNN // LRU_BLOCKS
LRU_CONV_W = 4
LRU_C = 8.0
SC_CONV_W = 3
D_FF = 256 * ((8 * D_MODEL // 3 + 128) // 256)
N_MOD = 9
ALPHA = (2 * DEPTH) ** 0.25
BETA = (8 * DEPTH) ** -0.25
LN_EPS = 1e-5
NEG_INF = -1e30

kernel_name = "hybrid_na_rglru_shortconv_prefix_dit"


def layer_norm(x, g, b):
    xf = x.astype(jnp.float32)
    mu = xf.mean(-1, keepdims=True)
    var = jnp.square(xf - mu).mean(-1, keepdims=True)
    return ((xf - mu) * lax.rsqrt(var + LN_EPS)).astype(x.dtype) * g + b


def modulation(cond, w, b):
    m = jax.nn.silu(cond) @ w + b
    return m.reshape(cond.shape[0], N_MOD, 1, D_MODEL)


def modulate(t, m, j):
    return t * (1 + m[:, 3 * j + 1]) + m[:, 3 * j]


def post_norm(t, m, j, y, g, b):
    return layer_norm(ALPHA * t + m[:, 3 * j + 2] * y, g, b)


def swiglu(h, w_in, w_out):
    g, u = jnp.split(h @ w_in, 2, axis=-1)
    return (jax.nn.silu(g) * u) @ w_out


def ffn_half(t, m, j, w_in, w_out, g, b):
    return post_norm(t, m, j, 0.5 * swiglu(modulate(t, m, j), w_in, w_out), g, b)


def dwconv_centred(u, w):
    k = w.shape[0]
    left = k // 2
    right = k - 1 - left
    return lax.conv_general_dilated(
        u, w[:, None, :].astype(u.dtype), window_strides=(1,), padding=[(left, right)],
        dimension_numbers=('NWC', 'WIO', 'NWC'), feature_group_count=u.shape[-1])


def na_mixer(h, hc, w_qkv, w_o, rpb, need_ctx):
    B, S, _ = h.shape
    L = hc.shape[1]
    rows = S // GRID_W
    kr = min(WIN_ROWS, rows)
    scale = NA_HEAD_DIM ** -0.5
    grid = (B, rows, GRID_W, NA_HEADS, NA_HEAD_DIM)
    q, k, v = [t.reshape(grid) for t in jnp.split(h @ w_qkv, 3, axis=-1)]
    kc, vc = [t.reshape(B, L, NA_HEADS, NA_HEAD_DIM) for t in jnp.split(hc @ w_qkv[:, D_MODEL:], 2, axis=-1)]

    col = jnp.arange(GRID_W)
    col_start = jnp.clip(col - WIN_COLS // 2, 0, GRID_W - WIN_COLS)
    kcol = col[None, :]
    col_in = (kcol >= col_start[:, None]) & (kcol < col_start[:, None] + WIN_COLS)
    dcol_idx = jnp.clip(kcol - col[:, None], 1 - WIN_COLS, WIN_COLS - 1) + WIN_COLS - 1

    def row_block(r):
        r0 = jnp.clip(r - kr // 2, 0, rows - kr)
        kb = lax.dynamic_slice_in_dim(k, r0, kr, axis=1)
        vb = lax.dynamic_slice_in_dim(v, r0, kr, axis=1)
        qr = lax.dynamic_index_in_dim(q, r, axis=1, keepdims=False)
        drow_idx = r0 + jnp.arange(kr) - r + WIN_ROWS - 1
        bias = rpb[:, drow_idx[None, :, None], dcol_idx[:, None, :]].astype(jnp.float32)
        s_loc = jnp.einsum('bqhd,bjkhd->bhqjk', qr, kb).astype(jnp.float32) * scale + bias
        s_loc = jnp.where(col_in[:, None, :], s_loc, NEG_INF)
        s_ctx = jnp.einsum('bqhd,blhd->bhql', qr, kc).astype(jnp.float32) * scale
        s = jnp.concatenate([s_loc.reshape(B, NA_HEADS, GRID_W, kr * GRID_W), s_ctx], axis=-1)
        p = jax.nn.softmax(s, axis=-1).astype(vb.dtype)
        p_loc = p[..., :kr * GRID_W].reshape(B, NA_HEADS, GRID_W, kr, GRID_W)
        p_ctx = p[..., kr * GRID_W:]
        return (jnp.einsum('bhqjk,bjkhd->bqhd', p_loc, vb)
                + jnp.einsum('bhql,blhd->bqhd', p_ctx, vc))

    o = lax.map(row_block, jnp.arange(rows))
    y = jnp.moveaxis(o, 0, 1).reshape(B, S, D_MODEL) @ w_o
    yc = None
    if need_ctx:
        qc = (hc @ w_qkv[:, :D_MODEL]).reshape(B, L, NA_HEADS, NA_HEAD_DIM)
        sc = jnp.einsum('bqhd,bkhd->bhqk', qc, kc).astype(jnp.float32) * scale
        pc = jax.nn.softmax(sc, axis=-1).astype(vc.dtype)
        yc = jnp.einsum('bhqk,bkhd->bqhd', pc, vc).reshape(B, L, D_MODEL) @ w_o
    return y, yc


def rglru_coeffs(u, w_g, b_g, lam):
    B, T, _ = u.shape
    ub = u.reshape(B, T, LRU_BLOCKS, LRU_BLOCK_W)
    gates = jnp.einsum('btnk,gnkj->gbtnj', ub, w_g).reshape(2, B, T, D_RNN) + b_g[:, None, None]
    gates = gates.astype(jnp.float32)
    r = jax.nn.sigmoid(gates[0])
    i = jax.nn.sigmoid(gates[1])
    log_a = -LRU_C * r * jax.nn.softplus(-lam.astype(jnp.float32))
    a = jnp.exp(log_a)
    b = jnp.sqrt(-jnp.expm1(2 * log_a)) * (i * u.astype(jnp.float32))
    return a, b


def linear_scan(a, b, reverse):
    def combine(e1, e2):
        a1, b1 = e1
        a2, b2 = e2
        return a1 * a2, a2 * b1 + b2
    return lax.associative_scan(combine, (a, b), axis=1, reverse=reverse)


def lru_mixer(h, hc, w_in, conv_w, conv_b, w_g, b_g, lam, w_out, need_ctx):
    gate_l, u_l = jnp.split(h @ w_in, 2, axis=-1)
    u_l = dwconv_centred(u_l, conv_w) + conv_b
    if need_ctx:
        gate_c, u_c = jnp.split(hc @ w_in, 2, axis=-1)
    else:
        u_c = hc @ w_in[:, D_RNN:]
    u_c = dwconv_centred(u_c, conv_w) + conv_b
    ys_l = []
    ys_c = []
    for d, rev in enumerate((False, True)):
        a_c, b_c = rglru_coeffs(u_c, w_g[d], b_g[d], lam[d])
        _, h_c = linear_scan(a_c, b_c, rev)
        h0 = h_c[:, 0] if rev else h_c[:, -1]
        a_l, b_l = rglru_coeffs(u_l, w_g[d], b_g[d], lam[d])
        a_cum, h_l = linear_scan(a_l, b_l, rev)
        ys_l.append(h_l + a_cum * h0[:, None])
        if need_ctx:
            ys_c.append(h_c)
    y = (jax.nn.gelu(gate_l) * (ys_l[0] + ys_l[1]).astype(h.dtype)) @ w_out
    yc = None
    if need_ctx:
        yc = (jax.nn.gelu(gate_c) * (ys_c[0] + ys_c[1]).astype(hc.dtype)) @ w_out
    return y, yc


def sc_mixer(h, hc, w_in, conv_w, w_out, need_ctx):
    def one(t):
        bg, cg, u = jnp.split(t @ w_in, 3, axis=-1)
        return (bg * dwconv_centred(cg * u, conv_w)) @ w_out
    return one(h), (one(hc) if need_ctx else None)


def setup_inputs(seed: int = 0) -> dict:
    key = jax.random.key(seed)
    ks = jax.random.split(key, 24)
    f32 = jnp.float32

    def nrm(k, shape, scale):
        return jax.random.normal(k, shape, f32) * scale

    a_pow = jax.random.uniform(ks[18], (N_LRU, 2, D_RNN), f32, 0.9, 0.999)
    a_base = a_pow ** (1.0 / LRU_C)
    return {
        "x": nrm(ks[0], (BATCH, SEQ, D_MODEL), 1.0),
        "c": nrm(ks[1], (BATCH, D_MODEL), 1.0),
        "ctx": nrm(ks[2], (BATCH, CTX_LEN, D_MODEL), 1.0),
        "c_ctx": nrm(ks[3], (D_MODEL,), 1.0),
        "mod_w": nrm(ks[4], (DEPTH, D_MODEL, N_MOD * D_MODEL), 0.5 * D_MODEL ** -0.5),
        "mod_b": nrm(ks[5], (DEPTH, N_MOD * D_MODEL), 0.02),
        "ln_g": 1.0 + nrm(ks[6], (DEPTH, 3, D_MODEL), 0.02),
        "ln_b": nrm(ks[7], (DEPTH, 3, D_MODEL), 0.02),
        "ffn_w_in": nrm(ks[8], (DEPTH, 2, D_MODEL, 2 * D_FF), D_MODEL ** -0.5),
        "ffn_w_out": nrm(ks[9], (DEPTH, 2, D_FF, D_MODEL), BETA * D_FF ** -0.5),
        "na_w_qkv": nrm(ks[10], (N_NA, D_MODEL, 3 * D_MODEL), D_MODEL ** -0.5),
        "na_w_o": nrm(ks[11], (N_NA, D_MODEL, D_MODEL), BETA * D_MODEL ** -0.5),
        "na_rpb": nrm(ks[12], (N_NA, NA_HEADS, 2 * WIN_ROWS - 1, 2 * WIN_COLS - 1), 0.1),
        "lru_w_in": nrm(ks[13], (N_LRU, D_MODEL, 2 * D_RNN), D_MODEL ** -0.5),
        "lru_conv_w": nrm(ks[14], (N_LRU, LRU_CONV_W, D_RNN), LRU_CONV_W ** -0.5),
        "lru_conv_b": nrm(ks[15], (N_LRU, D_RNN), 0.02),
        "lru_w_gates": nrm(ks[16], (N_LRU, 2, 2, LRU_BLOCKS, LRU_BLOCK_W, LRU_BLOCK_W), LRU_BLOCK_W ** -0.5),
        "lru_b_gates": nrm(ks[17], (N_LRU, 2, 2, D_RNN), 0.02),
        "lru_lambda": jnp.log(a_base) - jnp.log1p(-a_base),
        "lru_w_out": nrm(ks[19], (N_LRU, D_RNN, D_MODEL), BETA * D_RNN ** -0.5),
        "sc_w_in": nrm(ks[20], (N_SC, D_MODEL, 3 * D_MODEL), D_MODEL ** -0.5),
        "sc_conv_w": nrm(ks[21], (N_SC, SC_CONV_W, D_MODEL), SC_CONV_W ** -0.5),
        "sc_w_out": nrm(ks[22], (N_SC, D_MODEL, D_MODEL), BETA * D_MODEL ** -0.5),
    }


def reference(x, c, ctx, c_ctx, mod_w, mod_b, ln_g, ln_b, ffn_w_in, ffn_w_out, na_w_qkv, na_w_o, na_rpb,
              lru_w_in, lru_conv_w, lru_conv_b, lru_w_gates, lru_b_gates, lru_lambda, lru_w_out,
              sc_w_in, sc_conv_w, sc_w_out):
    xc = ctx
    for l in range(DEPTH):
        kind = l % N_MIXERS
        idx = l // N_MIXERS
        ctx_out = l < DEPTH - 1
        ctx_in = ctx_out or kind != 2
        m = modulation(c, mod_w[l], mod_b[l])
        mc = modulation(c_ctx[None], mod_w[l], mod_b[l]) if ctx_in else None

        x = ffn_half(x, m, 0, ffn_w_in[l, 0], ffn_w_out[l, 0], ln_g[l, 0], ln_b[l, 0])
        if ctx_in:
            xc = ffn_half(xc, mc, 0, ffn_w_in[l, 0], ffn_w_out[l, 0], ln_g[l, 0], ln_b[l, 0])

        h = modulate(x, m, 1)
        hc = modulate(xc, mc, 1) if ctx_in else None
        if kind == 0:
            y, yc = na_mixer(h, hc, na_w_qkv[idx], na_w_o[idx], na_rpb[idx], ctx_out)
        elif kind == 1:
            y, yc = lru_mixer(h, hc, lru_w_in[idx], lru_conv_w[idx], lru_conv_b[idx], lru_w_gates[idx],
                              lru_b_gates[idx], lru_lambda[idx], lru_w_out[idx], ctx_out)
        else:
            y, yc = sc_mixer(h, hc, sc_w_in[idx], sc_conv_w[idx], sc_w_out[idx], ctx_out)
        x = post_norm(x, m, 1, y, ln_g[l, 1], ln_b[l, 1])

        x = ffn_half(x, m, 2, ffn_w_in[l, 1], ffn_w_out[l, 1], ln_g[l, 2], ln_b[l, 2])
        if ctx_out:
            xc = post_norm(xc, mc, 1, yc, ln_g[l, 1], ln_b[l, 1])
            xc = ffn_half(xc, mc, 2, ffn_w_in[l, 1], ffn_w_out[l, 1], ln_g[l, 2], ln_b[l, 2])
    return x
```

```python
import functools

import jax
import jax.numpy as jnp
from jax import lax
from jax.experimental import pallas as pl
from jax.experimental.pallas import tpu as pltpu

F32 = jnp.float32
BF16 = jnp.bfloat16

D_MODEL = 1024
DEPTH = 4
GRID_W = 64
N_MIXERS = 3
NA_HEADS = 16
NA_HEAD_DIM = D_MODEL // NA_HEADS
WIN_ROWS = 8
WIN_COLS = 16
D_RNN = 1408
LRU_BLOCKS = 16
LRU_BLOCK_W = D_RNN // LRU_BLOCKS
LRU_CONV_W = 4
LRU_C = 8.0
SC_CONV_W = 3
D_FF = 2816
N_MOD = 9
ALPHA = (2 * DEPTH) ** 0.25
LN_EPS = 1e-5
NEG_INF = -1e30

LANES = 128
SUBLANES = 8
VMEM_LIMIT_BYTES = 56 * 1024 * 1024

ROW_TILE = 512
FF_CHUNK = 256
SC_CHUNK = 256
MOD_COLS = 1024
MOD_ROWS = 16
LRU_TILE = LANES
LRU_KW = 3 * LANES
HALO = SUBLANES
PAIR = 2 * NA_HEAD_DIM


def _params(*sem):
    return pltpu.CompilerParams(dimension_semantics=sem, vmem_limit_bytes=VMEM_LIMIT_BYTES)


def _resident(shape):
    zeros = (0,) * len(shape)
    return pl.BlockSpec(shape, lambda *_: zeros, pipeline_mode=pl.Buffered(1))


def _layer_norm(t, g, b):
    mu = jnp.mean(t, axis=-1, keepdims=True)
    d = t - mu
    var = jnp.mean(d * d, axis=-1, keepdims=True)
    return d * lax.rsqrt(var + LN_EPS) * g + b


def _mod_rows(m_ref, j):
    return m_ref[3 * j:3 * j + 1, :], m_ref[3 * j + 1:3 * j + 2, :], m_ref[3 * j + 2:3 * j + 3, :]


def _mod_kernel(c_ref, w_ref, b_ref, o_ref):
    c = c_ref[...]
    s = (c * jax.nn.sigmoid(c)).astype(BF16)
    o_ref[...] = jnp.dot(s, w_ref[...].astype(BF16), preferred_element_type=F32) + b_ref[...]


def _modulation(cond, mod_w, mod_b):
    depth, d, n = mod_w.shape
    return pl.pallas_call(
        _mod_kernel,
        grid=(depth, n // MOD_COLS),
        in_specs=[
            pl.BlockSpec((MOD_ROWS, d), lambda l, j: (0, 0)),
            pl.BlockSpec((None, d, MOD_COLS), lambda l, j: (l, 0, j)),
            pl.BlockSpec((None, 1, MOD_COLS), lambda l, j: (l, 0, j)),
        ],
        out_specs=pl.BlockSpec((None, MOD_ROWS, MOD_COLS), lambda l, j: (l, 0, j)),
        out_shape=jax.ShapeDtypeStruct((depth, MOD_ROWS, n), F32),
        compiler_params=_params("arbitrary", "arbitrary"),
        name="modulation",
    )(cond, mod_w, mod_b.reshape(depth, 1, n))


def _ffn_kernel(x_ref, m_ref, w1_ref, w2_ref, g_ref, b_ref, o_ref, h_ref, acc_ref, *, j):
    shift, scale, gate = _mod_rows(m_ref, j)
    h_ref[...] = (x_ref[...] * (1 + scale) + shift).astype(BF16)
    acc_ref[...] = jnp.zeros_like(acc_ref)

    def chunk(c, carry):
        gu = jnp.dot(h_ref[...], w1_ref[c], preferred_element_type=F32)
        g = gu[:, :FF_CHUNK]
        u = gu[:, FF_CHUNK:]
        a = (g * jax.nn.sigmoid(g) * u).astype(BF16)
        acc_ref[...] += jnp.dot(a, w2_ref[c], preferred_element_type=F32)
        return carry

    lax.fori_loop(0, w1_ref.shape[0], chunk, 0)
    t = ALPHA * x_ref[...] + gate * (0.5 * acc_ref[...])
    o_ref[...] = _layer_norm(t, g_ref[...], b_ref[...])


def _ffn_half(x, m, j, w1, w2, ln_g, ln_b, n_groups):
    _, rows, d = x.shape
    return pl.pallas_call(
        functools.partial(_ffn_kernel, j=j),
        grid=(n_groups, rows // ROW_TILE),
        in_specs=[
            pl.BlockSpec((None, ROW_TILE, d), lambda g, i: (g, i, 0)),
            pl.BlockSpec((None, N_MOD, d), lambda g, i: (g, 0, 0)),
            _resident(w1.shape),
            _resident(w2.shape),
            _resident((1, d)),
            _resident((1, d)),
        ],
        out_specs=pl.BlockSpec((None, ROW_TILE, d), lambda g, i: (g, i, 0)),
        out_shape=jax.ShapeDtypeStruct((n_groups, rows, d), F32),
        scratch_shapes=[pltpu.VMEM((ROW_TILE, d), BF16), pltpu.VMEM((ROW_TILE, d), F32)],
        compiler_params=_params("arbitrary", "arbitrary"),
        name="ffn_half",
    )(x, m, w1, w2, ln_g.reshape(1, d), ln_b.reshape(1, d))


def _proj_kernel(x_ref, m_ref, w_ref, o_ref):
    shift, scale, _ = _mod_rows(m_ref, 1)
    h = (x_ref[...] * (1 + scale) + shift).astype(BF16)
    o_ref[...] = jnp.dot(h, w_ref[...], preferred_element_type=F32).astype(o_ref.dtype)


def _mixer_proj(x, m, w, out_dtype):
    n_groups, rows, d = x.shape
    n = w.shape[1]
    return pl.pallas_call(
        _proj_kernel,
        grid=(n_groups, rows // ROW_TILE),
        in_specs=[
            pl.BlockSpec((None, ROW_TILE, d), lambda g, i: (g, i, 0)),
            pl.BlockSpec((None, N_MOD, d), lambda g, i: (g, 0, 0)),
            _resident(w.shape),
        ],
        out_specs=pl.BlockSpec((None, ROW_TILE, n), lambda g, i: (g, i, 0)),
        out_shape=jax.ShapeDtypeStruct((n_groups, rows, n), out_dtype),
        compiler_params=_params("arbitrary", "arbitrary"),
        name="mixer_proj",
    )(x, m, w)


def _outproj_kernel(*refs, n_lat, has_ctx):
    if has_ctx:
        al_ref, ac_ref, x_ref, m_ref, w_ref, g_ref, b_ref, o_ref = refs
        a = jnp.where(pl.program_id(0) == n_lat, ac_ref[...], al_ref[...])
    else:
        al_ref, x_ref, m_ref, w_ref, g_ref, b_ref, o_ref = refs
        a = al_ref[...]
    _, _, gate = _mod_rows(m_ref, 1)
    y = jnp.dot(a, w_ref[...], preferred_element_type=F32)
    t = ALPHA * x_ref[...] + gate * y
    o_ref[...] = _layer_norm(t, g_ref[...], b_ref[...])


def _mixer_outproj(a_lat, a_ctx, x, m, w, ln_g, ln_b):
    n_lat, rows, k = a_lat.shape
    d = x.shape[-1]
    has_ctx = a_ctx is not None
    n_groups = n_lat + 1 if has_ctx else n_lat
    in_specs = [pl.BlockSpec((None, ROW_TILE, k), lambda g, i: (jnp.minimum(g, n_lat - 1), i, 0))]
    args = [a_lat]
    if has_ctx:
        in_specs.append(pl.BlockSpec((None, ROW_TILE, k), lambda g, i: (0, i, 0)))
        args.append(a_ctx)
    in_specs += [
        pl.BlockSpec((None, ROW_TILE, d), lambda g, i: (g, i, 0)),
        pl.BlockSpec((None, N_MOD, d), lambda g, i: (g, 0, 0)),
        _resident(w.shape),
        _resident((1, d)),
        _resident((1, d)),
    ]
    args += [x, m, w, ln_g.reshape(1, d), ln_b.reshape(1, d)]
    return pl.pallas_call(
        functools.partial(_outproj_kernel, n_lat=n_lat, has_ctx=has_ctx),
        grid=(n_groups, rows // ROW_TILE),
        in_specs=in_specs,
        out_specs=pl.BlockSpec((None, ROW_TILE, d), lambda g, i: (g, i, 0)),
        out_shape=jax.ShapeDtypeStruct((n_groups, rows, d), F32),
        compiler_params=_params("arbitrary", "arbitrary"),
        name="mixer_outproj",
    )(*args)


def _pair_queries(q2):
    lane = lax.broadcasted_iota(jnp.int32, q2.shape, 1)
    zero = jnp.zeros_like(q2)
    qs = q2 * jnp.asarray(NA_HEAD_DIM ** -0.5, q2.dtype)
    return jnp.concatenate([jnp.where(lane < NA_HEAD_DIM, qs, zero),
                            jnp.where(lane < NA_HEAD_DIM, zero, qs)], axis=0)


def _unpair(o2):
    n = o2.shape[0] // 2
    lane = lax.broadcasted_iota(jnp.int32, (n, PAIR), 1)
    return jnp.where(lane < NA_HEAD_DIM, o2[:n], o2[n:])


def _scores(q, k):
    return lax.dot_general(q, k, (((1,), (1,)), ((), ())), preferred_element_type=F32)


def _na_kernel(q_ref, k_ref, v_ref, kc_ref, vc_ref, bias_ref, o_ref, *, n_rows):
    r = pl.program_id(1)
    r0 = jnp.clip(r - WIN_ROWS // 2, 0, n_rows - WIN_ROWS)
    start = pl.multiple_of(r0 * GRID_W, GRID_W)
    n_loc = WIN_ROWS * GRID_W
    for p in range(NA_HEADS // 2):
        sl = slice(PAIR * p, PAIR * (p + 1))
        q = _pair_queries(q_ref[:, sl])
        s_loc = _scores(q, k_ref[pl.ds(start, n_loc), sl]) + bias_ref[p]
        s_ctx = _scores(q, kc_ref[:, sl])
        mx = jnp.maximum(jnp.max(s_loc, axis=-1, keepdims=True), jnp.max(s_ctx, axis=-1, keepdims=True))
        e_loc = jnp.exp(s_loc - mx)
        e_ctx = jnp.exp(s_ctx - mx)
        inv = 1.0 / (jnp.sum(e_loc, axis=-1, keepdims=True) + jnp.sum(e_ctx, axis=-1, keepdims=True))
        o2 = (jnp.dot((e_loc * inv).astype(BF16), v_ref[pl.ds(start, n_loc), sl], preferred_element_type=F32)
              + jnp.dot((e_ctx * inv).astype(BF16), vc_ref[:, sl], preferred_element_type=F32))
        o_ref[:, sl] = _unpair(o2).astype(o_ref.dtype)


def _na_attention(qkv, bias, n_lat, ctx_len):
    _, rows, _ = qkv.shape
    d = D_MODEL
    n_rows = rows // GRID_W
    half = WIN_ROWS // 2

    def variant(r):
        return r - jnp.clip(r - half, 0, n_rows - WIN_ROWS)

    return pl.pallas_call(
        functools.partial(_na_kernel, n_rows=n_rows),
        grid=(n_lat, n_rows),
        in_specs=[
            pl.BlockSpec((None, GRID_W, d), lambda b, r: (b, r, 0)),
            pl.BlockSpec((None, rows, d), lambda b, r: (b, 0, 1)),
            pl.BlockSpec((None, rows, d), lambda b, r: (b, 0, 2)),
            pl.BlockSpec((None, ctx_len, d), lambda b, r: (n_lat, b, 1)),
            pl.BlockSpec((None, ctx_len, d), lambda b, r: (n_lat, b, 2)),
            pl.BlockSpec((None,) + bias.shape[1:], lambda b, r: (variant(r), 0, 0, 0)),
        ],
        out_specs=pl.BlockSpec((None, GRID_W, d), lambda b, r: (b, r, 0)),
        out_shape=jax.ShapeDtypeStruct((n_lat, rows, d), BF16),
        compiler_params=_params("arbitrary", "arbitrary"),
        name="na_attention",
    )(qkv, qkv, qkv, qkv, qkv, bias)


def _ctx_attn_kernel(q_ref, k_ref, v_ref, o_ref):
    for p in range(NA_HEADS // 2):
        sl = slice(PAIR * p, PAIR * (p + 1))
        s = _scores(_pair_queries(q_ref[:, sl]), k_ref[:, sl])
        e = jnp.exp(s - jnp.max(s, axis=-1, keepdims=True))
        pr = (e * (1.0 / jnp.sum(e, axis=-1, keepdims=True))).astype(BF16)
        o_ref[:, sl] = _unpair(jnp.dot(pr, v_ref[:, sl], preferred_element_type=F32)).astype(o_ref.dtype)


def _ctx_attention(qkv, n_lat, ctx_len):
    _, rows, _ = qkv.shape
    d = D_MODEL
    return pl.pallas_call(
        _ctx_attn_kernel,
        grid=(n_lat,),
        in_specs=[pl.BlockSpec((None, ctx_len, d), lambda b, c=c: (n_lat, b, c)) for c in range(3)],
        out_specs=pl.BlockSpec((None, ctx_len, d), lambda b: (0, b, 0)),
        out_shape=jax.ShapeDtypeStruct((1, rows, d), BF16),
        compiler_params=_params("arbitrary"),
        name="ctx_attention",
    )(qkv, qkv, qkv)


def _na_bias_table(rpb):
    col = jnp.arange(GRID_W)
    col_start = jnp.clip(col - WIN_COLS // 2, 0, GRID_W - WIN_COLS)
    kcol = col[None, :]
    col_in = (kcol >= col_start[:, None]) & (kcol < col_start[:, None] + WIN_COLS)
    dcol = jnp.clip(kcol - col[:, None], 1 - WIN_COLS, WIN_COLS - 1) + WIN_COLS - 1
    drow = jnp.arange(WIN_ROWS)[None, :] - jnp.arange(WIN_ROWS)[:, None] + WIN_ROWS - 1
    tbl = rpb[:, drow[:, None, :, None], dcol[None, :, None, :]].astype(F32)
    tbl = jnp.where(col_in[None, None, :, None, :], tbl, NEG_INF)
    tbl = jnp.transpose(tbl, (1, 0, 2, 3, 4))
    return tbl.reshape(WIN_ROWS, NA_HEADS // 2, 2 * GRID_W, WIN_ROWS * GRID_W)


def _lru_tile_starts():
    starts = []
    for j in range(D_RNN // LRU_TILE):
        lo = (LRU_TILE * j // LRU_BLOCK_W) * LRU_BLOCK_W
        hi = ((LRU_TILE * (j + 1) - 1) // LRU_BLOCK_W + 1) * LRU_BLOCK_W
        ks = min(lo // LANES * LANES, D_RNN - LRU_KW)
        assert ks <= lo and hi <= ks + LRU_KW
        starts.append(ks)
    return starts


def _shift_rows(x, s, row):
    n = x.shape[0]
    if s > 0:
        return jnp.where(row >= s, pltpu.roll(x, s, 0), 0.0)
    return jnp.where(row < n + s, pltpu.roll(x, n + s, 0), 0.0)


def _dwconv(u, w_ref, row, left):
    acc = None
    for i in range(w_ref.shape[0]):
        s = left - i
        term = w_ref[i:i + 1, :] * (u if s == 0 else _shift_rows(u, s, row))
        acc = term if acc is None else acc + term
    return acc


def _lru_kernel(kb_ref, u0_ref, u1_ref, u2_ref, ut_ref, gate_ref, wt_ref, bg_ref, cw_ref, cb_ref, cwt_ref,
                cbt_ref, lam_ref, h0_ref, y_ref, hfin_ref, a_scr, b_scr, pa_scr, pb_scr, cin_scr, y_scr):
    del kb_ref
    t_len = ut_ref.shape[0]
    n_tiles = t_len // SUBLANES
    left = LRU_CONV_W // 2
    row = lax.broadcasted_iota(jnp.int32, (t_len, 1), 0)

    u3 = jnp.concatenate([u0_ref[...], u1_ref[...], u2_ref[...]], axis=1)
    uc = _dwconv(u3, cw_ref, row, left) + cb_ref[...]
    gates = jnp.dot(uc.astype(BF16), wt_ref[...], preferred_element_type=F32) + bg_ref[...]
    ut = _dwconv(ut_ref[...], cwt_ref, row, left) + cbt_ref[...]

    for d in range(2):
        r = jax.nn.sigmoid(gates[:, (2 * d) * LRU_TILE:(2 * d + 1) * LRU_TILE])
        i = jax.nn.sigmoid(gates[:, (2 * d + 1) * LRU_TILE:(2 * d + 2) * LRU_TILE])
        neg_lam = -lam_ref[d:d + 1, :]
        softplus = jnp.maximum(neg_lam, 0.0) + jnp.log1p(jnp.exp(-jnp.abs(neg_lam)))
        a = jnp.exp(-LRU_C * r * softplus)
        a_scr[d] = a
        b_scr[d] = jnp.sqrt(1.0 - a * a) * (i * ut)

    rowk = lax.broadcasted_iota(jnp.int32, (n_tiles, 1), 0)
    for d, rev in ((0, False), (1, True)):
        acc_a = acc_b = None
        for n in range(SUBLANES):
            i = SUBLANES - 1 - n if rev else n
            ai = a_scr[d, pl.ds(i, n_tiles, stride=SUBLANES), :]
            bi = b_scr[d, pl.ds(i, n_tiles, stride=SUBLANES), :]
            if n == 0:
                acc_a, acc_b = ai, bi
            else:
                acc_b = ai * acc_b + bi
                acc_a = ai * acc_a
            pa_scr[d, i] = acc_a
            pb_scr[d, i] = acc_b
        s = 1
        while s < n_tiles:
            if rev:
                valid = rowk < n_tiles - s
                a_sh, b_sh = pltpu.roll(acc_a, n_tiles - s, 0), pltpu.roll(acc_b, n_tiles - s, 0)
            else:
                valid = rowk >= s
                a_sh, b_sh = pltpu.roll(acc_a, s, 0), pltpu.roll(acc_b, s, 0)
            acc_b = jnp.where(valid, acc_a * b_sh + acc_b, acc_b)
            acc_a = jnp.where(valid, acc_a * a_sh, acc_a)
            s *= 2
        h0 = h0_ref[d:d + 1, :]
        state = acc_a * h0 + acc_b
        if rev:
            cin_scr[d] = jnp.where(rowk < n_tiles - 1, pltpu.roll(state, n_tiles - 1, 0), h0)
            hfin_ref[d:d + 1, :] = state[0:1, :]
        else:
            cin_scr[d] = jnp.where(rowk >= 1, pltpu.roll(state, 1, 0), h0)
            hfin_ref[d:d + 1, :] = state[n_tiles - 1:n_tiles, :]

    for i in range(SUBLANES):
        h_sum = (pa_scr[0, i] * cin_scr[0] + pb_scr[0, i]) + (pa_scr[1, i] * cin_scr[1] + pb_scr[1, i])
        gi = gate_ref[pl.ds(i, n_tiles, stride=SUBLANES), :]
        y_scr[pl.ds(i, n_tiles, stride=SUBLANES), :] = jax.nn.gelu(gi) * h_sum
    y_ref[...] = y_scr[...].astype(y_ref.dtype)


def _lru_scan(proj, tables, h0, n_seq, seq_len, group0, per_group):
    kb, wt, bg, cw, cb, cwt, cbt, lam = tables
    n_tiles = D_RNN // LRU_TILE
    col0 = D_RNN // LRU_TILE
    rows = proj.shape[1]

    def seq_block(s):
        return group0 + s // per_group, s % per_group

    def u_spec(off):
        return pl.BlockSpec((None, seq_len, LRU_TILE),
                            lambda s, j, kb_ref: (*seq_block(s), col0 + kb_ref[j] + off))

    def tile_spec(shape):
        return pl.BlockSpec((None,) + shape, lambda s, j, kb_ref: (j,) + (0,) * len(shape))

    grid_spec = pltpu.PrefetchScalarGridSpec(
        num_scalar_prefetch=1,
        grid=(n_seq, n_tiles),
        in_specs=[
            u_spec(0), u_spec(1), u_spec(2),
            pl.BlockSpec((None, seq_len, LRU_TILE), lambda s, j, kb_ref: (*seq_block(s), col0 + j)),
            pl.BlockSpec((None, seq_len, LRU_TILE), lambda s, j, kb_ref: (*seq_block(s), j)),
            tile_spec((LRU_KW, 4 * LRU_TILE)),
            tile_spec((1, 4 * LRU_TILE)),
            tile_spec((LRU_CONV_W, LRU_KW)),
            tile_spec((1, LRU_KW)),
            tile_spec((LRU_CONV_W, LRU_TILE)),
            tile_spec((1, LRU_TILE)),
            tile_spec((2, LRU_TILE)),
            pl.BlockSpec((None, 2, LRU_TILE), lambda s, j, kb_ref: (s, 0, j)),
        ],
        out_specs=[
            pl.BlockSpec((None, seq_len, LRU_TILE), lambda s, j, kb_ref: (s // per_group, s % per_group, j)),
            pl.BlockSpec((None, 2, LRU_TILE), lambda s, j, kb_ref: (s, 0, j)),
        ],
        scratch_shapes=[
            pltpu.VMEM((2, seq_len, LRU_TILE), F32),
            pltpu.VMEM((2, seq_len, LRU_TILE), F32),
            pltpu.VMEM((2, SUBLANES, seq_len // SUBLANES, LRU_TILE), F32),
            pltpu.VMEM((2, SUBLANES, seq_len // SUBLANES, LRU_TILE), F32),
            pltpu.VMEM((2, seq_len // SUBLANES, LRU_TILE), F32),
            pltpu.VMEM((seq_len, LRU_TILE), F32),
        ],
    )
    return pl.pallas_call(
        _lru_kernel,
        grid_spec=grid_spec,
        out_shape=[jax.ShapeDtypeStruct((n_seq // per_group, rows, D_RNN), BF16),
                   jax.ShapeDtypeStruct((n_seq, 2, D_RNN), F32)],
        compiler_params=_params("arbitrary", "arbitrary"),
        name="rglru_scan",
    )(kb, proj, proj, proj, proj, proj, wt, bg, cw, cb, cwt, cbt, lam, h0)


def _lru_tables(conv_w, conv_b, w_gates, b_gates, lam):
    starts = _lru_tile_starts()
    eye = jnp.eye(LRU_BLOCKS, dtype=F32)
    dense = (w_gates[:, :, :, :, None, :] * eye[None, None, :, None, :, None]).reshape(2, 2, D_RNN, D_RNN)
    wt, bg, cw, cb, cwt, cbt, lm = [], [], [], [], [], [], []
    for j, ks in enumerate(starts):
        cols = slice(LRU_TILE * j, LRU_TILE * (j + 1))
        win = slice(ks, ks + LRU_KW)
        wt.append(jnp.concatenate([dense[d, g, win, cols] for d in range(2) for g in range(2)], axis=1))
        bg.append(jnp.concatenate([b_gates[d, g, cols] for d in range(2) for g in range(2)])[None, :])
        cw.append(conv_w[:, win])
        cb.append(conv_b[None, win])
        cwt.append(conv_w[:, cols])
        cbt.append(conv_b[None, cols])
        lm.append(lam[:, cols])
    kb = jnp.asarray([ks // LANES for ks in starts], jnp.int32)
    return (kb, jnp.stack(wt).astype(BF16), jnp.stack(bg), jnp.stack(cw), jnp.stack(cb), jnp.stack(cwt),
            jnp.stack(cbt), jnp.stack(lm))


def _sc_kernel(xm_ref, xp_ref, xn_ref, m_ref, w1_ref, w2_ref, cw_ref, g_ref, b_ref, o_ref, h_ref, acc_ref,
               *, n_lat, ctx_len, rows):
    tm = xm_ref.shape[0]
    ext = tm + 2 * HALO
    shift, scale, gate = _mod_rows(m_ref, 1)

    def mod(x):
        return (x * (1 + scale) + shift).astype(BF16)

    h_ref[0:HALO, :] = mod(xp_ref[...])
    h_ref[HALO:HALO + tm, :] = mod(xm_ref[...])
    h_ref[HALO + tm:ext, :] = mod(xn_ref[...])
    acc_ref[...] = jnp.zeros_like(acc_ref)

    pos = pl.program_id(1) * tm + lax.broadcasted_iota(jnp.int32, (tm, 1), 0)
    is_ctx = pl.program_id(0) == n_lat
    seq_pos = jnp.where(is_ctx, pos & (ctx_len - 1), pos)
    seq_last = jnp.where(is_ctx, ctx_len - 1, rows - 1)
    has_prev = seq_pos != 0
    has_next = seq_pos != seq_last

    def chunk(c, carry):
        t = jnp.dot(h_ref[...], w1_ref[c], preferred_element_type=F32)
        bgate = t[HALO:HALO + tm, :SC_CHUNK]
        w = t[:, SC_CHUNK:2 * SC_CHUNK] * t[:, 2 * SC_CHUNK:]
        w_prev = pltpu.roll(w, 1, 0)[HALO:HALO + tm]
        w_next = pltpu.roll(w, ext - 1, 0)[HALO:HALO + tm]
        cw = cw_ref[c]
        conv = (cw[0:1, :] * jnp.where(has_prev, w_prev, 0.0) + cw[1:2, :] * w[HALO:HALO + tm]
                + cw[2:3, :] * jnp.where(has_next, w_next, 0.0))
        acc_ref[...] += jnp.dot((bgate * conv).astype(BF16), w2_ref[c], preferred_element_type=F32)
        return carry

    lax.fori_loop(0, w1_ref.shape[0], chunk, 0)
    t = ALPHA * xm_ref[...] + gate * acc_ref[...]
    o_ref[...] = _layer_norm(t, g_ref[...], b_ref[...])


def _sc_mixer(x, m, w1, w2, cw, ln_g, ln_b, n_groups, n_lat, ctx_len):
    _, rows, d = x.shape
    tm = ROW_TILE
    halo_per_tile = tm // HALO
    n_halo = rows // HALO
    assert ctx_len & (ctx_len - 1) == 0 and SC_CONV_W == 3
    return pl.pallas_call(
        functools.partial(_sc_kernel, n_lat=n_lat, ctx_len=ctx_len, rows=rows),
        grid=(n_groups, rows // tm),
        in_specs=[
            pl.BlockSpec((None, tm, d), lambda g, i: (g, i, 0)),
            pl.BlockSpec((None, HALO, d), lambda g, i: (g, jnp.maximum(i * halo_per_tile - 1, 0), 0)),
            pl.BlockSpec((None, HALO, d), lambda g, i: (g, jnp.minimum((i + 1) * halo_per_tile, n_halo - 1), 0)),
            pl.BlockSpec((None, N_MOD, d), lambda g, i: (g, 0, 0)),
            _resident(w1.shape),
            _resident(w2.shape),
            _resident(cw.shape),
            _resident((1, d)),
            _resident((1, d)),
        ],
        out_specs=pl.BlockSpec((None, tm, d), lambda g, i: (g, i, 0)),
        out_shape=jax.ShapeDtypeStruct((n_groups, rows, d), F32),
        scratch_shapes=[pltpu.VMEM((tm + 2 * HALO, d), BF16), pltpu.VMEM((tm, d), F32)],
        compiler_params=_params("arbitrary", "arbitrary"),
        name="sc_mixer",
    )(x, x, x, m, w1, w2, cw, ln_g.reshape(1, d), ln_b.reshape(1, d))


def _chunk_cols(w, parts, chunk):
    *lead, k, pn = w.shape
    n = pn // parts
    w = w.reshape(*lead, k, parts, n // chunk, chunk)
    nl = len(lead)
    w = jnp.transpose(w, tuple(range(nl)) + (nl + 2, nl, nl + 1, nl + 3))
    return w.reshape(*lead, n // chunk, k, parts * chunk)


def kernel(x, c, ctx, c_ctx, mod_w, mod_b, ln_g, ln_b, ffn_w_in, ffn_w_out, na_w_qkv, na_w_o, na_rpb,
           lru_w_in, lru_conv_w, lru_conv_b, lru_w_gates, lru_b_gates, lru_lambda, lru_w_out,
           sc_w_in, sc_conv_w, sc_w_out):
    n_lat, rows, d = x.shape
    ctx_len = ctx.shape[1]
    assert d == D_MODEL and n_lat * ctx_len == rows and n_lat + 1 <= MOD_ROWS
    assert rows % ROW_TILE == 0 and rows % GRID_W == 0 and rows // GRID_W >= WIN_ROWS
    n_all = n_lat + 1

    xs = jnp.concatenate([x, ctx.reshape(1, rows, d)], axis=0)
    cond = jnp.concatenate([c, c_ctx[None], jnp.zeros((MOD_ROWS - n_all, d), F32)], axis=0)
    mods = _modulation(cond, mod_w, mod_b).reshape(DEPTH, MOD_ROWS, N_MOD, d)

    ffn_w1 = _chunk_cols(ffn_w_in, 2, FF_CHUNK).astype(BF16)
    ffn_w2 = ffn_w_out.reshape(DEPTH, 2, D_FF // FF_CHUNK, FF_CHUNK, d).astype(BF16)

    for l in range(DEPTH):
        kind = l % N_MIXERS
        idx = l // N_MIXERS
        ctx_out = l < DEPTH - 1
        ctx_in = ctx_out or kind != 2
        m = mods[l]
        n_in = n_all if ctx_in else n_lat
        n_out = n_all if ctx_out else n_lat

        xs = _ffn_half(xs, m, 0, ffn_w1[l, 0], ffn_w2[l, 0], ln_g[l, 0], ln_b[l, 0], n_in)

        if kind == 0:
            qkv = _mixer_proj(xs, m, na_w_qkv[idx].astype(BF16), BF16)
            a_lat = _na_attention(qkv, _na_bias_table(na_rpb[idx]), n_lat, ctx_len)
            a_ctx = _ctx_attention(qkv, n_lat, ctx_len) if ctx_out else None
            xs = _mixer_outproj(a_lat, a_ctx, xs, m, na_w_o[idx].astype(BF16), ln_g[l, 1], ln_b[l, 1])
        elif kind == 1:
            proj = _mixer_proj(xs, m, lru_w_in[idx].astype(BF16), F32)
            tables = _lru_tables(lru_conv_w[idx], lru_conv_b[idx], lru_w_gates[idx], lru_b_gates[idx],
                                 lru_lambda[idx])
            zeros = jnp.zeros((n_lat, 2, D_RNN), F32)
            a_ctx, h_ctx = _lru_scan(proj, tables, zeros, n_lat, ctx_len, n_lat, n_lat)
            a_lat, _ = _lru_scan(proj, tables, h_ctx, n_lat, rows, 0, 1)
            xs = _mixer_outproj(a_lat, a_ctx if ctx_out else None, xs, m, lru_w_out[idx].astype(BF16),
                                ln_g[l, 1], ln_b[l, 1])
        else:
            w1 = _chunk_cols(sc_w_in[idx], 3, SC_CHUNK).astype(BF16)
            w2 = sc_w_out[idx].reshape(d // SC_CHUNK, SC_CHUNK, d).astype(BF16)
            cw = jnp.transpose(sc_conv_w[idx].reshape(SC_CONV_W, d // SC_CHUNK, SC_CHUNK), (1, 0, 2))
            xs = _sc_mixer(xs, m, w1, w2, cw, ln_g[l, 1], ln_b[l, 1], n_out, n_lat, ctx_len)

        xs = _ffn_half(xs, m, 2, ffn_w1[l, 1], ffn_w2[l, 1], ln_g[l, 2], ln_b[l, 2], n_out)
    return xs[:n_lat]
```

```python
import functools

import jax
import jax.numpy as jnp
from jax import lax
from jax.experimental import pallas as pl
from jax.experimental.pallas import tpu as pltpu

F32 = jnp.float32
BF16 = jnp.bfloat16

D_MODEL = 1024
DEPTH = 4
GRID_W = 64
N_MIXERS = 3
NA_HEADS = 16
NA_HEAD_DIM = D_MODEL // NA_HEADS
WIN_ROWS = 8
WIN_COLS = 16
D_RNN = 1408
LRU_BLOCKS = 16
LRU_BLOCK_W = D_RNN // LRU_BLOCKS
LRU_CONV_W = 4
LRU_C = 8.0
SC_CONV_W = 3
D_FF = 2816
N_MOD = 9
ALPHA = (2 * DEPTH) ** 0.25
LN_EPS = 1e-5
NEG_INF = -1e30

LANES = 128
SUBLANES = 8
VMEM_LIMIT_BYTES = 56 * 1024 * 1024

ROW_TILE = 512
FF_CHUNK = 256
SC_CHUNK = 256
MOD_COLS = 1024
MOD_ROWS = 16
LRU_TILE = LANES
LRU_KW = 3 * LANES
HALO = SUBLANES
PAIR = 2 * NA_HEAD_DIM


def _params(*sem):
    return pltpu.CompilerParams(dimension_semantics=sem, vmem_limit_bytes=VMEM_LIMIT_BYTES)


def _resident(shape):
    zeros = (0,) * len(shape)
    return pl.BlockSpec(shape, lambda *_: zeros, pipeline_mode=pl.Buffered(1))


def _layer_norm(t, g, b):
    mu = jnp.mean(t, axis=-1, keepdims=True)
    d = t - mu
    var = jnp.mean(d * d, axis=-1, keepdims=True)
    return d * lax.rsqrt(var + LN_EPS) * g + b


def _mod_rows(m_ref, j):
    return m_ref[3 * j:3 * j + 1, :], m_ref[3 * j + 1:3 * j + 2, :], m_ref[3 * j + 2:3 * j + 3, :]


def _mod_kernel(c_ref, w_ref, b_ref, o_ref):
    c = c_ref[...]
    s = (c * jax.nn.sigmoid(c)).astype(BF16)
    o_ref[...] = jnp.dot(s, w_ref[...].astype(BF16), preferred_element_type=F32) + b_ref[...]


def _modulation(cond, mod_w, mod_b):
    depth, d, n = mod_w.shape
    return pl.pallas_call(
        _mod_kernel,
        grid=(depth, n // MOD_COLS),
        in_specs=[
            pl.BlockSpec((MOD_ROWS, d), lambda l, j: (0, 0)),
            pl.BlockSpec((None, d, MOD_COLS), lambda l, j: (l, 0, j)),
            pl.BlockSpec((None, 1, MOD_COLS), lambda l, j: (l, 0, j)),
        ],
        out_specs=pl.BlockSpec((None, MOD_ROWS, MOD_COLS), lambda l, j: (l, 0, j)),
        out_shape=jax.ShapeDtypeStruct((depth, MOD_ROWS, n), F32),
        compiler_params=_params("arbitrary", "arbitrary"),
        name="modulation",
    )(cond, mod_w, mod_b.reshape(depth, 1, n))


def _ffn_kernel(x_ref, m_ref, w1_ref, w2_ref, g_ref, b_ref, o_ref, h_ref, acc_ref, *, j):
    shift, scale, gate = _mod_rows(m_ref, j)
    h_ref[...] = (x_ref[...] * (1 + scale) + shift).astype(BF16)
    acc_ref[...] = jnp.zeros_like(acc_ref)

    def chunk(c, carry):
        gu = jnp.dot(h_ref[...], w1_ref[c], preferred_element_type=F32)
        g = gu[:, :FF_CHUNK]
        u = gu[:, FF_CHUNK:]
        a = (g * jax.nn.sigmoid(g) * u).astype(BF16)
        acc_ref[...] += jnp.dot(a, w2_ref[c], preferred_element_type=F32)
        return carry

    lax.fori_loop(0, w1_ref.shape[0], chunk, 0)
    t = ALPHA * x_ref[...] + gate * (0.5 * acc_ref[...])
    o_ref[...] = _layer_norm(t, g_ref[...], b_ref[...])


def _ffn_half(x, m, j, w1, w2, ln_g, ln_b, n_groups):
    _, rows, d = x.shape
    return pl.pallas_call(
        functools.partial(_ffn_kernel, j=j),
        grid=(n_groups, rows // ROW_TILE),
        in_specs=[
            pl.BlockSpec((None, ROW_TILE, d), lambda g, i: (g, i, 0)),
            pl.BlockSpec((None, N_MOD, d), lambda g, i: (g, 0, 0)),
            _resident(w1.shape),
            _resident(w2.shape),
            _resident((1, d)),
            _resident((1, d)),
        ],
        out_specs=pl.BlockSpec((None, ROW_TILE, d), lambda g, i: (g, i, 0)),
        out_shape=jax.ShapeDtypeStruct((n_groups, rows, d), F32),
        scratch_shapes=[pltpu.VMEM((ROW_TILE, d), BF16), pltpu.VMEM((ROW_TILE, d), F32)],
        compiler_params=_params("arbitrary", "arbitrary"),
        name="ffn_half",
    )(x, m, w1, w2, ln_g.reshape(1, d), ln_b.reshape(1, d))


def _proj_kernel(x_ref, m_ref, w_ref, o_ref):
    shift, scale, _ = _mod_rows(m_ref, 1)
    h = (x_ref[...] * (1 + scale) + shift).astype(BF16)
    o_ref[...] = jnp.dot(h, w_ref[...], preferred_element_type=F32).astype(o_ref.dtype)


def _mixer_proj(x, m, w, out_dtype):
    n_groups, rows, d = x.shape
    n = w.shape[1]
    return pl.pallas_call(
        _proj_kernel,
        grid=(n_groups, rows // ROW_TILE),
        in_specs=[
            pl.BlockSpec((None, ROW_TILE, d), lambda g, i: (g, i, 0)),
            pl.BlockSpec((None, N_MOD, d), lambda g, i: (g, 0, 0)),
            _resident(w.shape),
        ],
        out_specs=pl.BlockSpec((None, ROW_TILE, n), lambda g, i: (g, i, 0)),
        out_shape=jax.ShapeDtypeStruct((n_groups, rows, n), out_dtype),
        compiler_params=_params("arbitrary", "arbitrary"),
        name="mixer_proj",
    )(x, m, w)


def _outproj_kernel(*refs, n_lat, has_ctx):
    if has_ctx:
        al_ref, ac_ref, x_ref, m_ref, w_ref, g_ref, b_ref, o_ref = refs
        a = jnp.where(pl.program_id(0) == n_lat, ac_ref[...], al_ref[...])
    else:
        al_ref, x_ref, m_ref, w_ref, g_ref, b_ref, o_ref = refs
        a = al_ref[...]
    _, _, gate = _mod_rows(m_ref, 1)
    y = jnp.dot(a, w_ref[...], preferred_element_type=F32)
    t = ALPHA * x_ref[...] + gate * y
    o_ref[...] = _layer_norm(t, g_ref[...], b_ref[...])


def _mixer_outproj(a_lat, a_ctx, x, m, w, ln_g, ln_b):
    n_lat, rows, k = a_lat.shape
    d = x.shape[-1]
    has_ctx = a_ctx is not None
    n_groups = n_lat + 1 if has_ctx else n_lat
    in_specs = [pl.BlockSpec((None, ROW_TILE, k), lambda g, i: (jnp.minimum(g, n_lat - 1), i, 0))]
    args = [a_lat]
    if has_ctx:
        in_specs.append(pl.BlockSpec((None, ROW_TILE, k), lambda g, i: (0, i, 0)))
        args.append(a_ctx)
    in_specs += [
        pl.BlockSpec((None, ROW_TILE, d), lambda g, i: (g, i, 0)),
        pl.BlockSpec((None, N_MOD, d), lambda g, i: (g, 0, 0)),
        _resident(w.shape),
        _resident((1, d)),
        _resident((1, d)),
    ]
    args += [x, m, w, ln_g.reshape(1, d), ln_b.reshape(1, d)]
    return pl.pallas_call(
        functools.partial(_outproj_kernel, n_lat=n_lat, has_ctx=has_ctx),
        grid=(n_groups, rows // ROW_TILE),
        in_specs=in_specs,
        out_specs=pl.BlockSpec((None, ROW_TILE, d), lambda g, i: (g, i, 0)),
        out_shape=jax.ShapeDtypeStruct((n_groups, rows, d), F32),
        compiler_params=_params("arbitrary", "arbitrary"),
        name="mixer_outproj",
    )(*args)


def _pair_queries(q2):
    lane = lax.broadcasted_iota(jnp.int32, q2.shape, 1)
    zero = jnp.zeros_like(q2)
    qs = q2 * jnp.asarray(NA_HEAD_DIM ** -0.5, q2.dtype)
    return jnp.concatenate([jnp.where(lane < NA_HEAD_DIM, qs, zero),
                            jnp.where(lane < NA_HEAD_DIM, zero, qs)], axis=0)


def _unpair(o2):
    n = o2.shape[0] // 2
    lane = lax.broadcasted_iota(jnp.int32, (n, PAIR), 1)
    return jnp.where(lane < NA_HEAD_DIM, o2[:n], o2[n:])


def _scores(q, k):
    return lax.dot_general(q, k, (((1,), (1,)), ((), ())), preferred_element_type=F32)


def _na_kernel(q_ref, k_ref, v_ref, kc_ref, vc_ref, bias_ref, o_ref, *, n_rows):
    r = pl.program_id(1)
    r0 = jnp.clip(r - WIN_ROWS // 2, 0, n_rows - WIN_ROWS)
    start = pl.multiple_of(r0 * GRID_W, GRID_W)
    n_loc = WIN_ROWS * GRID_W
    for p in range(NA_HEADS // 2):
        sl = slice(PAIR * p, PAIR * (p + 1))
        q = _pair_queries(q_ref[:, sl])
        s_loc = _scores(q, k_ref[pl.ds(start, n_loc), sl]) + bias_ref[p]
        s_ctx = _scores(q, kc_ref[:, sl])
        mx = jnp.maximum(jnp.max(s_loc, axis=-1, keepdims=True), jnp.max(s_ctx, axis=-1, keepdims=True))
        e_loc = jnp.exp(s_loc - mx)
        e_ctx = jnp.exp(s_ctx - mx)
        inv = 1.0 / (jnp.sum(e_loc, axis=-1, keepdims=True) + jnp.sum(e_ctx, axis=-1, keepdims=True))
        o2 = (jnp.dot((e_loc * inv).astype(BF16), v_ref[pl.ds(start, n_loc), sl], preferred_element_type=F32)
              + jnp.dot((e_ctx * inv).astype(BF16), vc_ref[:, sl], preferred_element_type=F32))
        o_ref[:, sl] = _unpair(o2).astype(o_ref.dtype)


def _na_attention(qkv, bias, n_lat, ctx_len):
    _, rows, _ = qkv.shape
    d = D_MODEL
    n_rows = rows // GRID_W
    half = WIN_ROWS // 2

    def variant(r):
        return r - jnp.clip(r - half, 0, n_rows - WIN_ROWS)

    return pl.pallas_call(
        functools.partial(_na_kernel, n_rows=n_rows),
        grid=(n_lat, n_rows),
        in_specs=[
            pl.BlockSpec((None, GRID_W, d), lambda b, r: (b, r, 0)),
            pl.BlockSpec((None, rows, d), lambda b, r: (b, 0, 1)),
            pl.BlockSpec((None, rows, d), lambda b, r: (b, 0, 2)),
            pl.BlockSpec((None, ctx_len, d), lambda b, r: (n_lat, b, 1)),
            pl.BlockSpec((None, ctx_len, d), lambda b, r: (n_lat, b, 2)),
            pl.BlockSpec((None,) + bias.shape[1:], lambda b, r: (variant(r), 0, 0, 0)),
        ],
        out_specs=pl.BlockSpec((None, GRID_W, d), lambda b, r: (b, r, 0)),
        out_shape=jax.ShapeDtypeStruct((n_lat, rows, d), BF16),
        compiler_params=_params("arbitrary", "arbitrary"),
        name="na_attention",
    )(qkv, qkv, qkv, qkv, qkv, bias)


def _ctx_attn_kernel(q_ref, k_ref, v_ref, o_ref):
    for p in range(NA_HEADS // 2):
        sl = slice(PAIR * p, PAIR * (p + 1))
        s = _scores(_pair_queries(q_ref[:, sl]), k_ref[:, sl])
        e = jnp.exp(s - jnp.max(s, axis=-1, keepdims=True))
        pr = (e * (1.0 / jnp.sum(e, axis=-1, keepdims=True))).astype(BF16)
        o_ref[:, sl] = _unpair(jnp.dot(pr, v_ref[:, sl], preferred_element_type=F32)).astype(o_ref.dtype)


def _ctx_attention(qkv, n_lat, ctx_len):
    _, rows, _ = qkv.shape
    d = D_MODEL
    return pl.pallas_call(
        _ctx_attn_kernel,
        grid=(n_lat,),
        in_specs=[pl.BlockSpec((None, ctx_len, d), lambda b, c=c: (n_lat, b, c)) for c in range(3)],
        out_specs=pl.BlockSpec((None, ctx_len, d), lambda b: (0, b, 0)),
        out_shape=jax.ShapeDtypeStruct((1, rows, d), BF16),
        compiler_params=_params("arbitrary"),
        name="ctx_attention",
    )(qkv, qkv, qkv)


def _na_bias_table(rpb):
    col = jnp.arange(GRID_W)
    col_start = jnp.clip(col - WIN_COLS // 2, 0, GRID_W - WIN_COLS)
    kcol = col[None, :]
    col_in = (kcol >= col_start[:, None]) & (kcol < col_start[:, None] + WIN_COLS)
    dcol = jnp.clip(kcol - col[:, None], 1 - WIN_COLS, WIN_COLS - 1) + WIN_COLS - 1
    onehot = (dcol[None] == jnp.arange(2 * WIN_COLS - 1)[:, None, None]).astype(F32)
    toep = jnp.einsum('hdc,cqk->hqdk', rpb.astype(F32), onehot, precision=lax.Precision.HIGHEST)
    toep = jnp.where(col_in[None, :, None, :], toep, NEG_INF)
    tbl = jnp.stack([toep[:, :, WIN_ROWS - 1 - v:2 * WIN_ROWS - 1 - v, :] for v in range(WIN_ROWS)])
    return tbl.reshape(WIN_ROWS, NA_HEADS // 2, 2 * GRID_W, WIN_ROWS * GRID_W)


def _lru_tile_starts():
    starts = []
    for j in range(D_RNN // LRU_TILE):
        lo = (LRU_TILE * j // LRU_BLOCK_W) * LRU_BLOCK_W
        hi = ((LRU_TILE * (j + 1) - 1) // LRU_BLOCK_W + 1) * LRU_BLOCK_W
        ks = min(lo // LANES * LANES, D_RNN - LRU_KW)
        assert ks <= lo and hi <= ks + LRU_KW
        starts.append(ks)
    return starts


def _shift_rows(x, s, row):
    n = x.shape[0]
    if s > 0:
        return jnp.where(row >= s, pltpu.roll(x, s, 0), 0.0)
    return jnp.where(row < n + s, pltpu.roll(x, n + s, 0), 0.0)


def _dwconv(u, w_ref, row, left):
    acc = None
    for i in range(w_ref.shape[0]):
        s = left - i
        term = w_ref[i:i + 1, :] * (u if s == 0 else _shift_rows(u, s, row))
        acc = term if acc is None else acc + term
    return acc


def _lru_kernel(kb_ref, u0_ref, u1_ref, u2_ref, ut_ref, gate_ref, wt_ref, bg_ref, cw_ref, cb_ref, cwt_ref,
                cbt_ref, lam_ref, h0_ref, y_ref, hfin_ref, a_scr, b_scr, pa_scr, pb_scr, cin_scr, y_scr):
    del kb_ref
    t_len = ut_ref.shape[0]
    n_tiles = t_len // SUBLANES
    left = LRU_CONV_W // 2
    row = lax.broadcasted_iota(jnp.int32, (t_len, 1), 0)

    u3 = jnp.concatenate([u0_ref[...], u1_ref[...], u2_ref[...]], axis=1)
    uc = _dwconv(u3, cw_ref, row, left) + cb_ref[...]
    gates = jnp.dot(uc.astype(BF16), wt_ref[...], preferred_element_type=F32) + bg_ref[...]
    ut = _dwconv(ut_ref[...], cwt_ref, row, left) + cbt_ref[...]

    for d in range(2):
        r = jax.nn.sigmoid(gates[:, (2 * d) * LRU_TILE:(2 * d + 1) * LRU_TILE])
        i = jax.nn.sigmoid(gates[:, (2 * d + 1) * LRU_TILE:(2 * d + 2) * LRU_TILE])
        neg_lam = -lam_ref[d:d + 1, :]
        softplus = jnp.maximum(neg_lam, 0.0) + jnp.log1p(jnp.exp(-jnp.abs(neg_lam)))
        a = jnp.exp(-LRU_C * r * softplus)
        a_scr[d] = a
        b_scr[d] = jnp.sqrt(1.0 - a * a) * (i * ut)

    rowk = lax.broadcasted_iota(jnp.int32, (n_tiles, 1), 0)
    for d, rev in ((0, False), (1, True)):
        acc_a = acc_b = None
        for n in range(SUBLANES):
            i = SUBLANES - 1 - n if rev else n
            ai = a_scr[d, pl.ds(i, n_tiles, stride=SUBLANES), :]
            bi = b_scr[d, pl.ds(i, n_tiles, stride=SUBLANES), :]
            if n == 0:
                acc_a, acc_b = ai, bi
            else:
                acc_b = ai * acc_b + bi
                acc_a = ai * acc_a
            pa_scr[d, i] = acc_a
            pb_scr[d, i] = acc_b
        s = 1
        while s < n_tiles:
            if rev:
                valid = rowk < n_tiles - s
                a_sh, b_sh = pltpu.roll(acc_a, n_tiles - s, 0), pltpu.roll(acc_b, n_tiles - s, 0)
            else:
                valid = rowk >= s
                a_sh, b_sh = pltpu.roll(acc_a, s, 0), pltpu.roll(acc_b, s, 0)
            acc_b = jnp.where(valid, acc_a * b_sh + acc_b, acc_b)
            acc_a = jnp.where(valid, acc_a * a_sh, acc_a)
            s *= 2
        h0 = h0_ref[d:d + 1, :]
        state = acc_a * h0 + acc_b
        if rev:
            cin_scr[d] = jnp.where(rowk < n_tiles - 1, pltpu.roll(state, n_tiles - 1, 0), h0)
            hfin_ref[d:d + 1, :] = state[0:1, :]
        else:
            cin_scr[d] = jnp.where(rowk >= 1, pltpu.roll(state, 1, 0), h0)
            hfin_ref[d:d + 1, :] = state[n_tiles - 1:n_tiles, :]

    for i in range(SUBLANES):
        h_sum = (pa_scr[0, i] * cin_scr[0] + pb_scr[0, i]) + (pa_scr[1, i] * cin_scr[1] + pb_scr[1, i])
        gi = gate_ref[pl.ds(i, n_tiles, stride=SUBLANES), :]
        y_scr[pl.ds(i, n_tiles, stride=SUBLANES), :] = jax.nn.gelu(gi) * h_sum
    y_ref[...] = y_scr[...].astype(y_ref.dtype)


def _lru_scan(proj, tables, h0, n_seq, seq_len, group0, per_group):
    kb, wt, bg, cw, cb, cwt, cbt, lam = tables
    n_tiles = D_RNN // LRU_TILE
    col0 = D_RNN // LRU_TILE
    rows = proj.shape[1]

    def seq_block(s):
        return group0 + s // per_group, s % per_group

    def u_spec(off):
        return pl.BlockSpec((None, seq_len, LRU_TILE),
                            lambda s, j, kb_ref: (*seq_block(s), col0 + kb_ref[j] + off))

    def tile_spec(shape):
        return pl.BlockSpec((None,) + shape, lambda s, j, kb_ref: (j,) + (0,) * len(shape))

    grid_spec = pltpu.PrefetchScalarGridSpec(
        num_scalar_prefetch=1,
        grid=(n_seq, n_tiles),
        in_specs=[
            u_spec(0), u_spec(1), u_spec(2),
            pl.BlockSpec((None, seq_len, LRU_TILE), lambda s, j, kb_ref: (*seq_block(s), col0 + j)),
            pl.BlockSpec((None, seq_len, LRU_TILE), lambda s, j, kb_ref: (*seq_block(s), j)),
            tile_spec((LRU_KW, 4 * LRU_TILE)),
            tile_spec((1, 4 * LRU_TILE)),
            tile_spec((LRU_CONV_W, LRU_KW)),
            tile_spec((1, LRU_KW)),
            tile_spec((LRU_CONV_W, LRU_TILE)),
            tile_spec((1, LRU_TILE)),
            tile_spec((2, LRU_TILE)),
            pl.BlockSpec((None, 2, LRU_TILE), lambda s, j, kb_ref: (s, 0, j)),
        ],
        out_specs=[
            pl.BlockSpec((None, seq_len, LRU_TILE), lambda s, j, kb_ref: (s // per_group, s % per_group, j)),
            pl.BlockSpec((None, 2, LRU_TILE), lambda s, j, kb_ref: (s, 0, j)),
        ],
        scratch_shapes=[
            pltpu.VMEM((2, seq_len, LRU_TILE), F32),
            pltpu.VMEM((2, seq_len, LRU_TILE), F32),
            pltpu.VMEM((2, SUBLANES, seq_len // SUBLANES, LRU_TILE), F32),
            pltpu.VMEM((2, SUBLANES, seq_len // SUBLANES, LRU_TILE), F32),
            pltpu.VMEM((2, seq_len // SUBLANES, LRU_TILE), F32),
            pltpu.VMEM((seq_len, LRU_TILE), F32),
        ],
    )
    return pl.pallas_call(
        _lru_kernel,
        grid_spec=grid_spec,
        out_shape=[jax.ShapeDtypeStruct((n_seq // per_group, rows, D_RNN), BF16),
                   jax.ShapeDtypeStruct((n_seq, 2, D_RNN), F32)],
        compiler_params=_params("arbitrary", "arbitrary"),
        name="rglru_scan",
    )(kb, proj, proj, proj, proj, proj, wt, bg, cw, cb, cwt, cbt, lam, h0)


def _lru_tables(conv_w, conv_b, w_gates, b_gates, lam):
    starts = _lru_tile_starts()
    eye = jnp.eye(LRU_BLOCKS, dtype=F32)
    dense = (w_gates[:, :, :, :, None, :] * eye[None, None, :, None, :, None]).reshape(2, 2, D_RNN, D_RNN)
    wt, bg, cw, cb, cwt, cbt, lm = [], [], [], [], [], [], []
    for j, ks in enumerate(starts):
        cols = slice(LRU_TILE * j, LRU_TILE * (j + 1))
        win = slice(ks, ks + LRU_KW)
        wt.append(jnp.concatenate([dense[d, g, win, cols] for d in range(2) for g in range(2)], axis=1))
        bg.append(jnp.concatenate([b_gates[d, g, cols] for d in range(2) for g in range(2)])[None, :])
        cw.append(conv_w[:, win])
        cb.append(conv_b[None, win])
        cwt.append(conv_w[:, cols])
        cbt.append(conv_b[None, cols])
        lm.append(lam[:, cols])
    kb = jnp.asarray([ks // LANES for ks in starts], jnp.int32)
    return (kb, jnp.stack(wt).astype(BF16), jnp.stack(bg), jnp.stack(cw), jnp.stack(cb), jnp.stack(cwt),
            jnp.stack(cbt), jnp.stack(lm))


def _sc_kernel(xm_ref, xp_ref, xn_ref, m_ref, w1_ref, w2_ref, cw_ref, g_ref, b_ref, o_ref, h_ref, acc_ref,
               *, n_lat, ctx_len, rows):
    tm = xm_ref.shape[0]
    ext = tm + 2 * HALO
    shift, scale, gate = _mod_rows(m_ref, 1)

    def mod(x):
        return (x * (1 + scale) + shift).astype(BF16)

    h_ref[0:HALO, :] = mod(xp_ref[...])
    h_ref[HALO:HALO + tm, :] = mod(xm_ref[...])
    h_ref[HALO + tm:ext, :] = mod(xn_ref[...])
    acc_ref[...] = jnp.zeros_like(acc_ref)

    pos = pl.program_id(1) * tm + lax.broadcasted_iota(jnp.int32, (tm, 1), 0)
    is_ctx = pl.program_id(0) == n_lat
    seq_pos = jnp.where(is_ctx, pos & (ctx_len - 1), pos)
    seq_last = jnp.where(is_ctx, ctx_len - 1, rows - 1)
    has_prev = seq_pos != 0
    has_next = seq_pos != seq_last

    def chunk(c, carry):
        t = jnp.dot(h_ref[...], w1_ref[c], preferred_element_type=F32)
        bgate = t[HALO:HALO + tm, :SC_CHUNK]
        w = t[:, SC_CHUNK:2 * SC_CHUNK] * t[:, 2 * SC_CHUNK:]
        w_prev = pltpu.roll(w, 1, 0)[HALO:HALO + tm]
        w_next = pltpu.roll(w, ext - 1, 0)[HALO:HALO + tm]
        cw = cw_ref[c]
        conv = (cw[0:1, :] * jnp.where(has_prev, w_prev, 0.0) + cw[1:2, :] * w[HALO:HALO + tm]
                + cw[2:3, :] * jnp.where(has_next, w_next, 0.0))
        acc_ref[...] += jnp.dot((bgate * conv).astype(BF16), w2_ref[c], preferred_element_type=F32)
        return carry

    lax.fori_loop(0, w1_ref.shape[0], chunk, 0)
    t = ALPHA * xm_ref[...] + gate * acc_ref[...]
    o_ref[...] = _layer_norm(t, g_ref[...], b_ref[...])


def _sc_mixer(x, m, w1, w2, cw, ln_g, ln_b, n_groups, n_lat, ctx_len):
    _, rows, d = x.shape
    tm = ROW_TILE
    halo_per_tile = tm // HALO
    n_halo = rows // HALO
    assert ctx_len & (ctx_len - 1) == 0 and SC_CONV_W == 3
    return pl.pallas_call(
        functools.partial(_sc_kernel, n_lat=n_lat, ctx_len=ctx_len, rows=rows),
        grid=(n_groups, rows // tm),
        in_specs=[
            pl.BlockSpec((None, tm, d), lambda g, i: (g, i, 0)),
            pl.BlockSpec((None, HALO, d), lambda g, i: (g, jnp.maximum(i * halo_per_tile - 1, 0), 0)),
            pl.BlockSpec((None, HALO, d), lambda g, i: (g, jnp.minimum((i + 1) * halo_per_tile, n_halo - 1), 0)),
            pl.BlockSpec((None, N_MOD, d), lambda g, i: (g, 0, 0)),
            _resident(w1.shape),
            _resident(w2.shape),
            _resident(cw.shape),
            _resident((1, d)),
            _resident((1, d)),
        ],
        out_specs=pl.BlockSpec((None, tm, d), lambda g, i: (g, i, 0)),
        out_shape=jax.ShapeDtypeStruct((n_groups, rows, d), F32),
        scratch_shapes=[pltpu.VMEM((tm + 2 * HALO, d), BF16), pltpu.VMEM((tm, d), F32)],
        compiler_params=_params("arbitrary", "arbitrary"),
        name="sc_mixer",
    )(x, x, x, m, w1, w2, cw, ln_g.reshape(1, d), ln_b.reshape(1, d))


def _chunk_cols(w, parts, chunk):
    *lead, k, pn = w.shape
    n = pn // parts
    w = w.reshape(*lead, k, parts, n // chunk, chunk)
    nl = len(lead)
    w = jnp.transpose(w, tuple(range(nl)) + (nl + 2, nl, nl + 1, nl + 3))
    return w.reshape(*lead, n // chunk, k, parts * chunk)


def kernel(x, c, ctx, c_ctx, mod_w, mod_b, ln_g, ln_b, ffn_w_in, ffn_w_out, na_w_qkv, na_w_o, na_rpb,
           lru_w_in, lru_conv_w, lru_conv_b, lru_w_gates, lru_b_gates, lru_lambda, lru_w_out,
           sc_w_in, sc_conv_w, sc_w_out):
    n_lat, rows, d = x.shape
    ctx_len = ctx.shape[1]
    assert d == D_MODEL and n_lat * ctx_len == rows and n_lat + 1 <= MOD_ROWS
    assert rows % ROW_TILE == 0 and rows % GRID_W == 0 and rows // GRID_W >= WIN_ROWS
    n_all = n_lat + 1

    xs = jnp.concatenate([x, ctx.reshape(1, rows, d)], axis=0)
    cond = jnp.concatenate([c, c_ctx[None], jnp.zeros((MOD_ROWS - n_all, d), F32)], axis=0)
    mods = _modulation(cond, mod_w, mod_b).reshape(DEPTH, MOD_ROWS, N_MOD, d)

    ffn_w1 = _chunk_cols(ffn_w_in, 2, FF_CHUNK).astype(BF16)
    ffn_w2 = ffn_w_out.reshape(DEPTH, 2, D_FF // FF_CHUNK, FF_CHUNK, d).astype(BF16)

    for l in range(DEPTH):
        kind = l % N_MIXERS
        idx = l // N_MIXERS
        ctx_out = l < DEPTH - 1
        ctx_in = ctx_out or kind != 2
        m = mods[l]
        n_in = n_all if ctx_in else n_lat
        n_out = n_all if ctx_out else n_lat

        xs = _ffn_half(xs, m, 0, ffn_w1[l, 0], ffn_w2[l, 0], ln_g[l, 0], ln_b[l, 0], n_in)

        if kind == 0:
            qkv = _mixer_proj(xs, m, na_w_qkv[idx].astype(BF16), BF16)
            a_lat = _na_attention(qkv, _na_bias_table(na_rpb[idx]), n_lat, ctx_len)
            a_ctx = _ctx_attention(qkv, n_lat, ctx_len) if ctx_out else None
            xs = _mixer_outproj(a_lat, a_ctx, xs, m, na_w_o[idx].astype(BF16), ln_g[l, 1], ln_b[l, 1])
        elif kind == 1:
            proj = _mixer_proj(xs, m, lru_w_in[idx].astype(BF16), F32)
            tables = _lru_tables(lru_conv_w[idx], lru_conv_b[idx], lru_w_gates[idx], lru_b_gates[idx],
                                 lru_lambda[idx])
            zeros = jnp.zeros((n_lat, 2, D_RNN), F32)
            a_ctx, h_ctx = _lru_scan(proj, tables, zeros, n_lat, ctx_len, n_lat, n_lat)
            a_lat, _ = _lru_scan(proj, tables, h_ctx, n_lat, rows, 0, 1)
            xs = _mixer_outproj(a_lat, a_ctx if ctx_out else None, xs, m, lru_w_out[idx].astype(BF16),
                                ln_g[l, 1], ln_b[l, 1])
        else:
            w1 = _chunk_cols(sc_w_in[idx], 3, SC_CHUNK).astype(BF16)
            w2 = sc_w_out[idx].reshape(d // SC_CHUNK, SC_CHUNK, d).astype(BF16)
            cw = jnp.transpose(sc_conv_w[idx].reshape(SC_CONV_W, d // SC_CHUNK, SC_CHUNK), (1, 0, 2))
            xs = _sc_mixer(xs, m, w1, w2, cw, ln_g[l, 1], ln_b[l, 1], n_out, n_lat, ctx_len)

        xs = _ffn_half(xs, m, 2, ffn_w1[l, 1], ffn_w2[l, 1], ln_g[l, 2], ln_b[l, 2], n_out)
    return xs[:n_lat]
```

```python
import functools

import jax
import jax.numpy as jnp
from jax import lax
from jax.experimental import pallas as pl
from jax.experimental.pallas import tpu as pltpu

F32 = jnp.float32
BF16 = jnp.bfloat16

D_MODEL = 1024
DEPTH = 4
GRID_W = 64
N_MIXERS = 3
NA_HEADS = 16
NA_HEAD_DIM = D_MODEL // NA_HEADS
WIN_ROWS = 8
WIN_COLS = 16
D_RNN = 1408
LRU_BLOCKS = 16
LRU_BLOCK_W = D_RNN // LRU_BLOCKS
LRU_CONV_W = 4
LRU_C = 8.0
SC_CONV_W = 3
D_FF = 2816
N_MOD = 9
ALPHA = (2 * DEPTH) ** 0.25
LN_EPS = 1e-5
NEG_INF = -1e30

LANES = 128
SUBLANES = 8
VMEM_LIMIT_BYTES = 56 * 1024 * 1024

ROW_TILE = 512
FF_CHUNK = 256
SC_CHUNK = 256
MOD_COLS = 1024
MOD_ROWS = 16
LRU_TILE = LANES
LRU_KW = 3 * LANES
HALO = SUBLANES
PAIR = 2 * NA_HEAD_DIM


def _params(*sem):
    return pltpu.CompilerParams(dimension_semantics=sem, vmem_limit_bytes=VMEM_LIMIT_BYTES)


def _resident(shape):
    zeros = (0,) * len(shape)
    return pl.BlockSpec(shape, lambda *_: zeros, pipeline_mode=pl.Buffered(1))


def _layer_norm(t, g, b):
    mu = jnp.mean(t, axis=-1, keepdims=True)
    d = t - mu
    var = jnp.mean(d * d, axis=-1, keepdims=True)
    return d * lax.rsqrt(var + LN_EPS) * g + b


def _mod_rows(m_ref, j):
    return m_ref[3 * j:3 * j + 1, :], m_ref[3 * j + 1:3 * j + 2, :], m_ref[3 * j + 2:3 * j + 3, :]


def _mod_kernel(c_ref, w_ref, b_ref, o_ref):
    c = c_ref[...]
    s = (c * jax.nn.sigmoid(c)).astype(BF16)
    o_ref[...] = jnp.dot(s, w_ref[...].astype(BF16), preferred_element_type=F32) + b_ref[...]


def _modulation(cond, mod_w, mod_b):
    depth, d, n = mod_w.shape
    return pl.pallas_call(
        _mod_kernel,
        grid=(depth, n // MOD_COLS),
        in_specs=[
            pl.BlockSpec((MOD_ROWS, d), lambda l, j: (0, 0)),
            pl.BlockSpec((None, d, MOD_COLS), lambda l, j: (l, 0, j)),
            pl.BlockSpec((None, 1, MOD_COLS), lambda l, j: (l, 0, j)),
        ],
        out_specs=pl.BlockSpec((None, MOD_ROWS, MOD_COLS), lambda l, j: (l, 0, j)),
        out_shape=jax.ShapeDtypeStruct((depth, MOD_ROWS, n), F32),
        compiler_params=_params("arbitrary", "arbitrary"),
        name="modulation",
    )(cond, mod_w, mod_b.reshape(depth, 1, n))


def _ffn_kernel(x_ref, m_ref, w1_ref, w2_ref, g_ref, b_ref, o_ref, h_ref, *, j):
    shift, scale, gate = _mod_rows(m_ref, j)
    h_ref[...] = (x_ref[...] * (1 + scale) + shift).astype(BF16)
    acc = None
    for c in range(w1_ref.shape[0]):
        gu = jnp.dot(h_ref[...], w1_ref[c], preferred_element_type=F32)
        g = gu[:, :FF_CHUNK]
        u = gu[:, FF_CHUNK:]
        a = (g * jax.nn.sigmoid(g) * u).astype(BF16)
        y = jnp.dot(a, w2_ref[c], preferred_element_type=F32)
        acc = y if acc is None else acc + y
    t = ALPHA * x_ref[...] + (0.5 * gate) * acc
    o_ref[...] = _layer_norm(t, g_ref[...], b_ref[...])


def _ffn_half(x, m, j, w1, w2, ln_g, ln_b, n_groups):
    _, rows, d = x.shape
    return pl.pallas_call(
        functools.partial(_ffn_kernel, j=j),
        grid=(n_groups, rows // ROW_TILE),
        in_specs=[
            pl.BlockSpec((None, ROW_TILE, d), lambda g, i: (g, i, 0)),
            pl.BlockSpec((None, N_MOD, d), lambda g, i: (g, 0, 0)),
            _resident(w1.shape),
            _resident(w2.shape),
            _resident((1, d)),
            _resident((1, d)),
        ],
        out_specs=pl.BlockSpec((None, ROW_TILE, d), lambda g, i: (g, i, 0)),
        out_shape=jax.ShapeDtypeStruct((n_groups, rows, d), F32),
        scratch_shapes=[pltpu.VMEM((ROW_TILE, d), BF16)],
        compiler_params=_params("arbitrary", "arbitrary"),
        name="ffn_half",
    )(x, m, w1, w2, ln_g.reshape(1, d), ln_b.reshape(1, d))


def _proj_kernel(x_ref, m_ref, w_ref, o_ref):
    shift, scale, _ = _mod_rows(m_ref, 1)
    h = (x_ref[...] * (1 + scale) + shift).astype(BF16)
    o_ref[...] = jnp.dot(h, w_ref[...], preferred_element_type=F32).astype(o_ref.dtype)


def _mixer_proj(x, m, w, out_dtype):
    n_groups, rows, d = x.shape
    n = w.shape[1]
    return pl.pallas_call(
        _proj_kernel,
        grid=(n_groups, rows // ROW_TILE),
        in_specs=[
            pl.BlockSpec((None, ROW_TILE, d), lambda g, i: (g, i, 0)),
            pl.BlockSpec((None, N_MOD, d), lambda g, i: (g, 0, 0)),
            _resident(w.shape),
        ],
        out_specs=pl.BlockSpec((None, ROW_TILE, n), lambda g, i: (g, i, 0)),
        out_shape=jax.ShapeDtypeStruct((n_groups, rows, n), out_dtype),
        compiler_params=_params("arbitrary", "arbitrary"),
        name="mixer_proj",
    )(x, m, w)


def _outproj_kernel(*refs, n_lat, has_ctx):
    if has_ctx:
        al_ref, ac_ref, x_ref, m_ref, w_ref, g_ref, b_ref, o_ref = refs
        a = jnp.where(pl.program_id(0) == n_lat, ac_ref[...], al_ref[...])
    else:
        al_ref, x_ref, m_ref, w_ref, g_ref, b_ref, o_ref = refs
        a = al_ref[...]
    _, _, gate = _mod_rows(m_ref, 1)
    y = jnp.dot(a, w_ref[...], preferred_element_type=F32)
    t = ALPHA * x_ref[...] + gate * y
    o_ref[...] = _layer_norm(t, g_ref[...], b_ref[...])


def _mixer_outproj(a_lat, a_ctx, x, m, w, ln_g, ln_b):
    n_lat, rows, k = a_lat.shape
    d = x.shape[-1]
    has_ctx = a_ctx is not None
    n_groups = n_lat + 1 if has_ctx else n_lat
    in_specs = [pl.BlockSpec((None, ROW_TILE, k), lambda g, i: (jnp.minimum(g, n_lat - 1), i, 0))]
    args = [a_lat]
    if has_ctx:
        in_specs.append(pl.BlockSpec((None, ROW_TILE, k), lambda g, i: (0, i, 0)))
        args.append(a_ctx)
    in_specs += [
        pl.BlockSpec((None, ROW_TILE, d), lambda g, i: (g, i, 0)),
        pl.BlockSpec((None, N_MOD, d), lambda g, i: (g, 0, 0)),
        _resident(w.shape),
        _resident((1, d)),
        _resident((1, d)),
    ]
    args += [x, m, w, ln_g.reshape(1, d), ln_b.reshape(1, d)]
    return pl.pallas_call(
        functools.partial(_outproj_kernel, n_lat=n_lat, has_ctx=has_ctx),
        grid=(n_groups, rows // ROW_TILE),
        in_specs=in_specs,
        out_specs=pl.BlockSpec((None, ROW_TILE, d), lambda g, i: (g, i, 0)),
        out_shape=jax.ShapeDtypeStruct((n_groups, rows, d), F32),
        compiler_params=_params("arbitrary", "arbitrary"),
        name="mixer_outproj",
    )(*args)


def _pair_queries(q2):
    lane = lax.broadcasted_iota(jnp.int32, q2.shape, 1)
    zero = jnp.zeros_like(q2)
    qs = q2 * jnp.asarray(NA_HEAD_DIM ** -0.5, q2.dtype)
    return jnp.concatenate([jnp.where(lane < NA_HEAD_DIM, qs, zero),
                            jnp.where(lane < NA_HEAD_DIM, zero, qs)], axis=0)


def _unpair(o2):
    n = o2.shape[0] // 2
    lane = lax.broadcasted_iota(jnp.int32, (n, PAIR), 1)
    return jnp.where(lane < NA_HEAD_DIM, o2[:n], o2[n:])


def _scores(q, k):
    return lax.dot_general(q, k, (((1,), (1,)), ((), ())), preferred_element_type=F32)


def _na_kernel(q_ref, k_ref, v_ref, kc_ref, vc_ref, bias_ref, o_ref, s_scr, p_scr, *, n_rows):
    r = pl.program_id(1)
    r0 = jnp.clip(r - WIN_ROWS // 2, 0, n_rows - WIN_ROWS)
    start = pl.multiple_of(r0 * GRID_W, GRID_W)
    n_loc = WIN_ROWS * GRID_W
    n_pairs = NA_HEADS // 2
    for p in range(n_pairs):
        sl = slice(PAIR * p, PAIR * (p + 1))
        q = _pair_queries(q_ref[:, sl])
        s_scr[p, :, :n_loc] = _scores(q, k_ref[pl.ds(start, n_loc), sl]) + bias_ref[p]
        s_scr[p, :, n_loc:] = _scores(q, kc_ref[:, sl])
    s = s_scr[...]
    e = jnp.exp(s - jnp.max(s, axis=-1, keepdims=True))
    p_scr[...] = e.astype(BF16)
    inv = 1.0 / jnp.sum(e, axis=-1, keepdims=True)
    for p in range(n_pairs):
        sl = slice(PAIR * p, PAIR * (p + 1))
        o2 = (jnp.dot(p_scr[p, :, :n_loc], v_ref[pl.ds(start, n_loc), sl], preferred_element_type=F32)
              + jnp.dot(p_scr[p, :, n_loc:], vc_ref[:, sl], preferred_element_type=F32))
        o_ref[:, sl] = _unpair(o2 * inv[p]).astype(o_ref.dtype)


def _na_attention(qkv, bias, n_lat, ctx_len):
    _, rows, _ = qkv.shape
    d = D_MODEL
    n_rows = rows // GRID_W
    half = WIN_ROWS // 2

    def variant(r):
        return r - jnp.clip(r - half, 0, n_rows - WIN_ROWS)

    return pl.pallas_call(
        functools.partial(_na_kernel, n_rows=n_rows),
        grid=(n_lat, n_rows),
        in_specs=[
            pl.BlockSpec((None, GRID_W, d), lambda b, r: (b, r, 0)),
            pl.BlockSpec((None, rows, d), lambda b, r: (b, 0, 1)),
            pl.BlockSpec((None, rows, d), lambda b, r: (b, 0, 2)),
            pl.BlockSpec((None, ctx_len, d), lambda b, r: (n_lat, b, 1)),
            pl.BlockSpec((None, ctx_len, d), lambda b, r: (n_lat, b, 2)),
            pl.BlockSpec((None,) + bias.shape[1:], lambda b, r: (variant(r), 0, 0, 0)),
        ],
        out_specs=pl.BlockSpec((None, GRID_W, d), lambda b, r: (b, r, 0)),
        out_shape=jax.ShapeDtypeStruct((n_lat, rows, d), BF16),
        scratch_shapes=[pltpu.VMEM((NA_HEADS // 2, PAIR, WIN_ROWS * GRID_W + ctx_len), F32),
                        pltpu.VMEM((NA_HEADS // 2, PAIR, WIN_ROWS * GRID_W + ctx_len), BF16)],
        compiler_params=_params("arbitrary", "arbitrary"),
        name="na_attention",
    )(qkv, qkv, qkv, qkv, qkv, bias)


def _ctx_attn_kernel(q_ref, k_ref, v_ref, o_ref):
    for p in range(NA_HEADS // 2):
        sl = slice(PAIR * p, PAIR * (p + 1))
        s = _scores(_pair_queries(q_ref[:, sl]), k_ref[:, sl])
        e = jnp.exp(s - jnp.max(s, axis=-1, keepdims=True))
        pr = (e * (1.0 / jnp.sum(e, axis=-1, keepdims=True))).astype(BF16)
        o_ref[:, sl] = _unpair(jnp.dot(pr, v_ref[:, sl], preferred_element_type=F32)).astype(o_ref.dtype)


def _ctx_attention(qkv, n_lat, ctx_len):
    _, rows, _ = qkv.shape
    d = D_MODEL
    return pl.pallas_call(
        _ctx_attn_kernel,
        grid=(n_lat,),
        in_specs=[pl.BlockSpec((None, ctx_len, d), lambda b, c=c: (n_lat, b, c)) for c in range(3)],
        out_specs=pl.BlockSpec((None, ctx_len, d), lambda b: (0, b, 0)),
        out_shape=jax.ShapeDtypeStruct((1, rows, d), BF16),
        compiler_params=_params("arbitrary"),
        name="ctx_attention",
    )(qkv, qkv, qkv)


def _na_bias_table(rpb):
    col = jnp.arange(GRID_W)
    col_start = jnp.clip(col - WIN_COLS // 2, 0, GRID_W - WIN_COLS)
    kcol = col[None, :]
    col_in = (kcol >= col_start[:, None]) & (kcol < col_start[:, None] + WIN_COLS)
    dcol = jnp.clip(kcol - col[:, None], 1 - WIN_COLS, WIN_COLS - 1) + WIN_COLS - 1
    onehot = (dcol[None] == jnp.arange(2 * WIN_COLS - 1)[:, None, None]).astype(F32)
    toep = jnp.einsum('hdc,cqk->hqdk', rpb.astype(F32), onehot, precision=lax.Precision.HIGHEST)
    toep = jnp.where(col_in[None, :, None, :], toep, NEG_INF)
    tbl = jnp.stack([toep[:, :, WIN_ROWS - 1 - v:2 * WIN_ROWS - 1 - v, :] for v in range(WIN_ROWS)])
    return tbl.reshape(WIN_ROWS, NA_HEADS // 2, 2 * GRID_W, WIN_ROWS * GRID_W)


def _lru_tile_starts():
    starts = []
    for j in range(D_RNN // LRU_TILE):
        lo = (LRU_TILE * j // LRU_BLOCK_W) * LRU_BLOCK_W
        hi = ((LRU_TILE * (j + 1) - 1) // LRU_BLOCK_W + 1) * LRU_BLOCK_W
        ks = min(lo // LANES * LANES, D_RNN - LRU_KW)
        assert ks <= lo and hi <= ks + LRU_KW
        starts.append(ks)
    return starts


def _shift_rows(x, s, row):
    n = x.shape[0]
    if s > 0:
        return jnp.where(row >= s, pltpu.roll(x, s, 0), 0.0)
    return jnp.where(row < n + s, pltpu.roll(x, n + s, 0), 0.0)


def _dwconv(u, w_ref, row, left):
    acc = None
    for i in range(w_ref.shape[0]):
        s = left - i
        term = w_ref[i:i + 1, :] * (u if s == 0 else _shift_rows(u, s, row))
        acc = term if acc is None else acc + term
    return acc


def _lru_kernel(kb_ref, u0_ref, u1_ref, u2_ref, ut_ref, gate_ref, wt_ref, bg_ref, cw_ref, cb_ref, cwt_ref,
                cbt_ref, lam_ref, h0_ref, y_ref, hfin_ref, a_scr, b_scr, pa_scr, pb_scr, cin_scr, y_scr):
    del kb_ref
    t_len = ut_ref.shape[0]
    n_tiles = t_len // SUBLANES
    left = LRU_CONV_W // 2
    row = lax.broadcasted_iota(jnp.int32, (t_len, 1), 0)

    u3 = jnp.concatenate([u0_ref[...], u1_ref[...], u2_ref[...]], axis=1)
    uc = _dwconv(u3, cw_ref, row, left) + cb_ref[...]
    gates = jnp.dot(uc.astype(BF16), wt_ref[...], preferred_element_type=F32) + bg_ref[...]
    ut = _dwconv(ut_ref[...], cwt_ref, row, left) + cbt_ref[...]

    for d in range(2):
        r = jax.nn.sigmoid(gates[:, (2 * d) * LRU_TILE:(2 * d + 1) * LRU_TILE])
        i = jax.nn.sigmoid(gates[:, (2 * d + 1) * LRU_TILE:(2 * d + 2) * LRU_TILE])
        neg_lam = -lam_ref[d:d + 1, :]
        softplus = jnp.maximum(neg_lam, 0.0) + jnp.log1p(jnp.exp(-jnp.abs(neg_lam)))
        a = jnp.exp(-LRU_C * r * softplus)
        a_scr[d] = a
        b_scr[d] = jnp.sqrt(1.0 - a * a) * (i * ut)

    rowk = lax.broadcasted_iota(jnp.int32, (n_tiles, 1), 0)
    for d, rev in ((0, False), (1, True)):
        acc_a = acc_b = None
        for n in range(SUBLANES):
            i = SUBLANES - 1 - n if rev else n
            ai = a_scr[d, pl.ds(i, n_tiles, stride=SUBLANES), :]
            bi = b_scr[d, pl.ds(i, n_tiles, stride=SUBLANES), :]
            if n == 0:
                acc_a, acc_b = ai, bi
            else:
                acc_b = ai * acc_b + bi
                acc_a = ai * acc_a
            pa_scr[d, i] = acc_a
            pb_scr[d, i] = acc_b
        s = 1
        while s < n_tiles:
            if rev:
                valid = rowk < n_tiles - s
                a_sh, b_sh = pltpu.roll(acc_a, n_tiles - s, 0), pltpu.roll(acc_b, n_tiles - s, 0)
            else:
                valid = rowk >= s
                a_sh, b_sh = pltpu.roll(acc_a, s, 0), pltpu.roll(acc_b, s, 0)
            acc_b = jnp.where(valid, acc_a * b_sh + acc_b, acc_b)
            acc_a = jnp.where(valid, acc_a * a_sh, acc_a)
            s *= 2
        h0 = h0_ref[d:d + 1, :]
        state = acc_a * h0 + acc_b
        if rev:
            cin_scr[d] = jnp.where(rowk < n_tiles - 1, pltpu.roll(state, n_tiles - 1, 0), h0)
            hfin_ref[d:d + 1, :] = state[0:1, :]
        else:
            cin_scr[d] = jnp.where(rowk >= 1, pltpu.roll(state, 1, 0), h0)
            hfin_ref[d:d + 1, :] = state[n_tiles - 1:n_tiles, :]

    for i in range(SUBLANES):
        h_sum = (pa_scr[0, i] * cin_scr[0] + pb_scr[0, i]) + (pa_scr[1, i] * cin_scr[1] + pb_scr[1, i])
        gi = gate_ref[pl.ds(i, n_tiles, stride=SUBLANES), :]
        y_scr[pl.ds(i, n_tiles, stride=SUBLANES), :] = jax.nn.gelu(gi) * h_sum
    y_ref[...] = y_scr[...].astype(y_ref.dtype)


def _lru_scan(proj, tables, h0, n_seq, seq_len, group0, per_group):
    kb, wt, bg, cw, cb, cwt, cbt, lam = tables
    n_tiles = D_RNN // LRU_TILE
    col0 = D_RNN // LRU_TILE
    rows = proj.shape[1]

    def seq_block(s):
        return group0 + s // per_group, s % per_group

    def u_spec(off):
        return pl.BlockSpec((None, seq_len, LRU_TILE),
                            lambda s, j, kb_ref: (*seq_block(s), col0 + kb_ref[j] + off))

    def tile_spec(shape):
        return pl.BlockSpec((None,) + shape, lambda s, j, kb_ref: (j,) + (0,) * len(shape))

    grid_spec = pltpu.PrefetchScalarGridSpec(
        num_scalar_prefetch=1,
        grid=(n_seq, n_tiles),
        in_specs=[
            u_spec(0), u_spec(1), u_spec(2),
            pl.BlockSpec((None, seq_len, LRU_TILE), lambda s, j, kb_ref: (*seq_block(s), col0 + j)),
            pl.BlockSpec((None, seq_len, LRU_TILE), lambda s, j, kb_ref: (*seq_block(s), j)),
            tile_spec((LRU_KW, 4 * LRU_TILE)),
            tile_spec((1, 4 * LRU_TILE)),
            tile_spec((LRU_CONV_W, LRU_KW)),
            tile_spec((1, LRU_KW)),
            tile_spec((LRU_CONV_W, LRU_TILE)),
            tile_spec((1, LRU_TILE)),
            tile_spec((2, LRU_TILE)),
            pl.BlockSpec((None, 2, LRU_TILE), lambda s, j, kb_ref: (s, 0, j)),
        ],
        out_specs=[
            pl.BlockSpec((None, seq_len, LRU_TILE), lambda s, j, kb_ref: (s // per_group, s % per_group, j)),
            pl.BlockSpec((None, 2, LRU_TILE), lambda s, j, kb_ref: (s, 0, j)),
        ],
        scratch_shapes=[
            pltpu.VMEM((2, seq_len, LRU_TILE), F32),
            pltpu.VMEM((2, seq_len, LRU_TILE), F32),
            pltpu.VMEM((2, SUBLANES, seq_len // SUBLANES, LRU_TILE), F32),
            pltpu.VMEM((2, SUBLANES, seq_len // SUBLANES, LRU_TILE), F32),
            pltpu.VMEM((2, seq_len // SUBLANES, LRU_TILE), F32),
            pltpu.VMEM((seq_len, LRU_TILE), F32),
        ],
    )
    return pl.pallas_call(
        _lru_kernel,
        grid_spec=grid_spec,
        out_shape=[jax.ShapeDtypeStruct((n_seq // per_group, rows, D_RNN), BF16),
                   jax.ShapeDtypeStruct((n_seq, 2, D_RNN), F32)],
        compiler_params=_params("arbitrary", "arbitrary"),
        name="rglru_scan",
    )(kb, proj, proj, proj, proj, proj, wt, bg, cw, cb, cwt, cbt, lam, h0)


def _lru_tables(conv_w, conv_b, w_gates, b_gates, lam):
    starts = _lru_tile_starts()
    eye = jnp.eye(LRU_BLOCKS, dtype=F32)
    dense = (w_gates[:, :, :, :, None, :] * eye[None, None, :, None, :, None]).reshape(2, 2, D_RNN, D_RNN)
    wt, bg, cw, cb, cwt, cbt, lm = [], [], [], [], [], [], []
    for j, ks in enumerate(starts):
        cols = slice(LRU_TILE * j, LRU_TILE * (j + 1))
        win = slice(ks, ks + LRU_KW)
        wt.append(jnp.concatenate([dense[d, g, win, cols] for d in range(2) for g in range(2)], axis=1))
        bg.append(jnp.concatenate([b_gates[d, g, cols] for d in range(2) for g in range(2)])[None, :])
        cw.append(conv_w[:, win])
        cb.append(conv_b[None, win])
        cwt.append(conv_w[:, cols])
        cbt.append(conv_b[None, cols])
        lm.append(lam[:, cols])
    kb = jnp.asarray([ks // LANES for ks in starts], jnp.int32)
    return (kb, jnp.stack(wt).astype(BF16), jnp.stack(bg), jnp.stack(cw), jnp.stack(cb), jnp.stack(cwt),
            jnp.stack(cbt), jnp.stack(lm))


def _sc_kernel(xm_ref, xp_ref, xn_ref, m_ref, w1_ref, w2_ref, cw_ref, g_ref, b_ref, o_ref, h_ref,
               *, n_lat, ctx_len, rows):
    tm = xm_ref.shape[0]
    ext = tm + 2 * HALO
    shift, scale, gate = _mod_rows(m_ref, 1)

    def mod(x):
        return (x * (1 + scale) + shift).astype(BF16)

    h_ref[0:HALO, :] = mod(xp_ref[...])
    h_ref[HALO:HALO + tm, :] = mod(xm_ref[...])
    h_ref[HALO + tm:ext, :] = mod(xn_ref[...])

    pos = pl.program_id(1) * tm + lax.broadcasted_iota(jnp.int32, (tm, 1), 0)
    is_ctx = pl.program_id(0) == n_lat
    seq_pos = jnp.where(is_ctx, pos & (ctx_len - 1), pos)
    seq_last = jnp.where(is_ctx, ctx_len - 1, rows - 1)
    has_prev = seq_pos != 0
    has_next = seq_pos != seq_last

    acc = None
    for c in range(w1_ref.shape[0]):
        t = jnp.dot(h_ref[...], w1_ref[c], preferred_element_type=F32)
        bgate = t[HALO:HALO + tm, :SC_CHUNK]
        w = t[:, SC_CHUNK:2 * SC_CHUNK] * t[:, 2 * SC_CHUNK:]
        w_prev = pltpu.roll(w, 1, 0)[HALO:HALO + tm]
        w_next = pltpu.roll(w, ext - 1, 0)[HALO:HALO + tm]
        cw = cw_ref[c]
        conv = (cw[0:1, :] * jnp.where(has_prev, w_prev, 0.0) + cw[1:2, :] * w[HALO:HALO + tm]
                + cw[2:3, :] * jnp.where(has_next, w_next, 0.0))
        y = jnp.dot((bgate * conv).astype(BF16), w2_ref[c], preferred_element_type=F32)
        acc = y if acc is None else acc + y
    t = ALPHA * xm_ref[...] + gate * acc
    o_ref[...] = _layer_norm(t, g_ref[...], b_ref[...])


def _sc_mixer(x, m, w1, w2, cw, ln_g, ln_b, n_groups, n_lat, ctx_len):
    _, rows, d = x.shape
    tm = ROW_TILE
    halo_per_tile = tm // HALO
    n_halo = rows // HALO
    assert ctx_len & (ctx_len - 1) == 0 and SC_CONV_W == 3
    return pl.pallas_call(
        functools.partial(_sc_kernel, n_lat=n_lat, ctx_len=ctx_len, rows=rows),
        grid=(n_groups, rows // tm),
        in_specs=[
            pl.BlockSpec((None, tm, d), lambda g, i: (g, i, 0)),
            pl.BlockSpec((None, HALO, d), lambda g, i: (g, jnp.maximum(i * halo_per_tile - 1, 0), 0)),
            pl.BlockSpec((None, HALO, d), lambda g, i: (g, jnp.minimum((i + 1) * halo_per_tile, n_halo - 1), 0)),
            pl.BlockSpec((None, N_MOD, d), lambda g, i: (g, 0, 0)),
            _resident(w1.shape),
            _resident(w2.shape),
            _resident(cw.shape),
            _resident((1, d)),
            _resident((1, d)),
        ],
        out_specs=pl.BlockSpec((None, tm, d), lambda g, i: (g, i, 0)),
        out_shape=jax.ShapeDtypeStruct((n_groups, rows, d), F32),
        scratch_shapes=[pltpu.VMEM((tm + 2 * HALO, d), BF16)],
        compiler_params=_params("arbitrary", "arbitrary"),
        name="sc_mixer",
    )(x, x, x, m, w1, w2, cw, ln_g.reshape(1, d), ln_b.reshape(1, d))


def _chunk_cols(w, parts, chunk):
    *lead, k, pn = w.shape
    n = pn // parts
    w = w.reshape(*lead, k, parts, n // chunk, chunk)
    nl = len(lead)
    w = jnp.transpose(w, tuple(range(nl)) + (nl + 2, nl, nl + 1, nl + 3))
    return w.reshape(*lead, n // chunk, k, parts * chunk)


def kernel(x, c, ctx, c_ctx, mod_w, mod_b, ln_g, ln_b, ffn_w_in, ffn_w_out, na_w_qkv, na_w_o, na_rpb,
           lru_w_in, lru_conv_w, lru_conv_b, lru_w_gates, lru_b_gates, lru_lambda, lru_w_out,
           sc_w_in, sc_conv_w, sc_w_out):
    n_lat, rows, d = x.shape
    ctx_len = ctx.shape[1]
    assert d == D_MODEL and n_lat * ctx_len == rows and n_lat + 1 <= MOD_ROWS
    assert rows % ROW_TILE == 0 and rows % GRID_W == 0 and rows // GRID_W >= WIN_ROWS
    n_all = n_lat + 1

    xs = jnp.concatenate([x, ctx.reshape(1, rows, d)], axis=0)
    cond = jnp.concatenate([c, c_ctx[None], jnp.zeros((MOD_ROWS - n_all, d), F32)], axis=0)
    mods = _modulation(cond, mod_w, mod_b).reshape(DEPTH, MOD_ROWS, N_MOD, d)

    ffn_w1 = _chunk_cols(ffn_w_in, 2, FF_CHUNK).astype(BF16)
    ffn_w2 = ffn_w_out.reshape(DEPTH, 2, D_FF // FF_CHUNK, FF_CHUNK, d).astype(BF16)

    for l in range(DEPTH):
        kind = l % N_MIXERS
        idx = l // N_MIXERS
        ctx_out = l < DEPTH - 1
        ctx_in = ctx_out or kind != 2
        m = mods[l]
        n_in = n_all if ctx_in else n_lat
        n_out = n_all if ctx_out else n_lat

        xs = _ffn_half(xs, m, 0, ffn_w1[l, 0], ffn_w2[l, 0], ln_g[l, 0], ln_b[l, 0], n_in)

        if kind == 0:
            qkv = _mixer_proj(xs, m, na_w_qkv[idx].astype(BF16), BF16)
            a_lat = _na_attention(qkv, _na_bias_table(na_rpb[idx]), n_lat, ctx_len)
            a_ctx = _ctx_attention(qkv, n_lat, ctx_len) if ctx_out else None
            xs = _mixer_outproj(a_lat, a_ctx, xs, m, na_w_o[idx].astype(BF16), ln_g[l, 1], ln_b[l, 1])
        elif kind == 1:
            proj = _mixer_proj(xs, m, lru_w_in[idx].astype(BF16), F32)
            tables = _lru_tables(lru_conv_w[idx], lru_conv_b[idx], lru_w_gates[idx], lru_b_gates[idx],
                                 lru_lambda[idx])
            zeros = jnp.zeros((n_lat, 2, D_RNN), F32)
            a_ctx, h_ctx = _lru_scan(proj, tables, zeros, n_lat, ctx_len, n_lat, n_lat)
            a_lat, _ = _lru_scan(proj, tables, h_ctx, n_lat, rows, 0, 1)
            xs = _mixer_outproj(a_lat, a_ctx if ctx_out else None, xs, m, lru_w_out[idx].astype(BF16),
                                ln_g[l, 1], ln_b[l, 1])
        else:
            w1 = _chunk_cols(sc_w_in[idx], 3, SC_CHUNK).astype(BF16)
            w2 = sc_w_out[idx].reshape(d // SC_CHUNK, SC_CHUNK, d).astype(BF16)
            cw = jnp.transpose(sc_conv_w[idx].reshape(SC_CONV_W, d // SC_CHUNK, SC_CHUNK), (1, 0, 2))
            xs = _sc_mixer(xs, m, w1, w2, cw, ln_g[l, 1], ln_b[l, 1], n_out, n_lat, ctx_len)

        xs = _ffn_half(xs, m, 2, ffn_w1[l, 1], ffn_w2[l, 1], ln_g[l, 2], ln_b[l, 2], n_out)
    return xs[:n_lat]
```

```python
import functools

import jax
import jax.numpy as jnp
from jax import lax
from jax.experimental import pallas as pl
from jax.experimental.pallas import tpu as pltpu

F32 = jnp.float32
BF16 = jnp.bfloat16

D_MODEL = 1024
DEPTH = 4
GRID_W = 64
N_MIXERS = 3
NA_HEADS = 16
NA_HEAD_DIM = D_MODEL // NA_HEADS
WIN_ROWS = 8
WIN_COLS = 16
D_RNN = 1408
LRU_BLOCKS = 16
LRU_BLOCK_W = D_RNN // LRU_BLOCKS
LRU_CONV_W = 4
LRU_C = 8.0
SC_CONV_W = 3
D_FF = 2816
N_MOD = 9
ALPHA = (2 * DEPTH) ** 0.25
LN_EPS = 1e-5
NEG_INF = -1e30

LANES = 128
SUBLANES = 8
VMEM_LIMIT_BYTES = 56 * 1024 * 1024

ROW_TILE = 512
FF_CHUNK = 256
SC_CHUNK = 256
W1_LAYOUT_ROWS = 256
MOD_COLS = 1024
MOD_ROWS = 16
LRU_TILE = LANES
LRU_KW = 3 * LANES
HALO = SUBLANES
PAIR = 2 * NA_HEAD_DIM


def _params(*sem):
    return pltpu.CompilerParams(dimension_semantics=sem, vmem_limit_bytes=VMEM_LIMIT_BYTES)


def _resident(shape):
    zeros = (0,) * len(shape)
    return pl.BlockSpec(shape, lambda *_: zeros, pipeline_mode=pl.Buffered(1))


def _layer_norm(t, g, b):
    mu = jnp.mean(t, axis=-1, keepdims=True)
    d = t - mu
    var = jnp.mean(d * d, axis=-1, keepdims=True)
    return d * lax.rsqrt(var + LN_EPS) * g + b


def _mod_spec(mods, l):
    return pl.BlockSpec((None, None) + mods.shape[2:], lambda g, i: (l, g, 0, 0))


def _ln_spec(ln, l, j):
    return pl.BlockSpec((None,) + ln.shape[1:], lambda g, i: (3 * l + j, 0, 0))


def _layer_weight_spec(w, idx):
    zeros = (0,) * (w.ndim - 1)
    return pl.BlockSpec((None,) + w.shape[1:], lambda *_: (idx,) + zeros, pipeline_mode=pl.Buffered(1))


def _mod_rows(m_ref, j):
    return m_ref[3 * j:3 * j + 1, :], m_ref[3 * j + 1:3 * j + 2, :], m_ref[3 * j + 2:3 * j + 3, :]


def _mod_kernel(c_ref, w_ref, b_ref, o_ref):
    c = c_ref[...]
    s = (c * jax.nn.sigmoid(c)).astype(BF16)
    o_ref[...] = jnp.dot(s, w_ref[...].astype(BF16), preferred_element_type=F32) + b_ref[...]


def _modulation(cond, mod_w, mod_b):
    depth, d, n = mod_w.shape
    return pl.pallas_call(
        _mod_kernel,
        grid=(depth, n // MOD_COLS),
        in_specs=[
            pl.BlockSpec((MOD_ROWS, d), lambda l, j: (0, 0)),
            pl.BlockSpec((None, d, MOD_COLS), lambda l, j: (l, 0, j)),
            pl.BlockSpec((None, 1, MOD_COLS), lambda l, j: (l, 0, j)),
        ],
        out_specs=pl.BlockSpec((None, MOD_ROWS, MOD_COLS), lambda l, j: (l, 0, j)),
        out_shape=jax.ShapeDtypeStruct((depth, MOD_ROWS, n), F32),
        compiler_params=_params("arbitrary", "arbitrary"),
        name="modulation",
    )(cond, mod_w, mod_b.reshape(depth, 1, n))


def _ffn_kernel(*refs, j, n_lat, has_ctx):
    if has_ctx:
        xl_ref, xc_ref, m_ref, w1_ref, w2_ref, g_ref, b_ref, o_ref, h_ref = refs
        x = jnp.where(pl.program_id(0) == n_lat, xc_ref[...], xl_ref[...])
    else:
        xl_ref, m_ref, w1_ref, w2_ref, g_ref, b_ref, o_ref, h_ref = refs
        x = xl_ref[...]
    shift, scale, gate = _mod_rows(m_ref, j)
    h_ref[...] = (x * (1 + scale) + shift).astype(BF16)
    acc = None
    for c in range(w1_ref.shape[0]):
        gu = jnp.dot(h_ref[...], w1_ref[c], preferred_element_type=F32)
        g = gu[:, :FF_CHUNK]
        u = gu[:, FF_CHUNK:]
        a = (g * jax.nn.sigmoid(g) * u).astype(BF16)
        y = jnp.dot(a, w2_ref[c], preferred_element_type=F32)
        acc = y if acc is None else acc + y
    t = ALPHA * x + (0.5 * gate) * acc
    o_ref[...] = _layer_norm(t, g_ref[...], b_ref[...])


def _ffn_half(x, x_ctx, mods, l, j, w1, w2, ln_g, ln_b, n_groups):
    n_lat, rows, d = x.shape
    has_ctx = x_ctx is not None
    if has_ctx:
        assert n_groups == n_lat + 1
        in_specs = [pl.BlockSpec((None, ROW_TILE, d), lambda g, i: (jnp.minimum(g, n_lat - 1), i, 0)),
                    pl.BlockSpec((None, ROW_TILE, d), lambda g, i: (0, i, 0))]
        args = [x, x_ctx]
    else:
        in_specs = [pl.BlockSpec((None, ROW_TILE, d), lambda g, i: (g, i, 0))]
        args = [x]
    slab = 2 * l + j // 2
    in_specs += [_mod_spec(mods, l), _layer_weight_spec(w1, slab), _layer_weight_spec(w2, slab),
                 _ln_spec(ln_g, l, j), _ln_spec(ln_b, l, j)]
    return pl.pallas_call(
        functools.partial(_ffn_kernel, j=j, n_lat=n_lat, has_ctx=has_ctx),
        grid=(n_groups, rows // ROW_TILE),
        in_specs=in_specs,
        out_specs=pl.BlockSpec((None, ROW_TILE, d), lambda g, i: (g, i, 0)),
        out_shape=jax.ShapeDtypeStruct((n_groups, rows, d), F32),
        scratch_shapes=[pltpu.VMEM((ROW_TILE, d), BF16)],
        compiler_params=_params("arbitrary", "arbitrary"),
        name="ffn_half",
    )(*args, mods, w1, w2, ln_g, ln_b)


def _w1_layout_kernel(g_ref, u_ref, o_ref):
    for c in range(o_ref.shape[0]):
        o_ref[c, :, :FF_CHUNK] = g_ref[:, FF_CHUNK * c:FF_CHUNK * (c + 1)].astype(BF16)
        o_ref[c, :, FF_CHUNK:] = u_ref[:, FF_CHUNK * c:FF_CHUNK * (c + 1)].astype(BF16)


def _ffn_w1_layout(w_in):
    n, d, _ = w_in.shape
    nc = D_FF // FF_CHUNK
    rt = W1_LAYOUT_ROWS
    return pl.pallas_call(
        _w1_layout_kernel,
        grid=(n, d // rt),
        in_specs=[pl.BlockSpec((None, rt, D_FF), lambda l, r: (l, r, 0)),
                  pl.BlockSpec((None, rt, D_FF), lambda l, r: (l, r, 1))],
        out_specs=pl.BlockSpec((None, nc, rt, 2 * FF_CHUNK), lambda l, r: (l, 0, r, 0)),
        out_shape=jax.ShapeDtypeStruct((n, nc, d, 2 * FF_CHUNK), BF16),
        compiler_params=_params("arbitrary", "arbitrary"),
        name="ffn_w1_layout",
    )(w_in, w_in)


def _proj_kernel(x_ref, m_ref, w_ref, o_ref):
    shift, scale, _ = _mod_rows(m_ref, 1)
    h = (x_ref[...] * (1 + scale) + shift).astype(BF16)
    o_ref[...] = jnp.dot(h, w_ref[...], preferred_element_type=F32).astype(o_ref.dtype)


def _mixer_proj(x, mods, l, w, out_dtype):
    n_groups, rows, d = x.shape
    n = w.shape[1]
    return pl.pallas_call(
        _proj_kernel,
        grid=(n_groups, rows // ROW_TILE),
        in_specs=[
            pl.BlockSpec((None, ROW_TILE, d), lambda g, i: (g, i, 0)),
            _mod_spec(mods, l),
            _resident(w.shape),
        ],
        out_specs=pl.BlockSpec((None, ROW_TILE, n), lambda g, i: (g, i, 0)),
        out_shape=jax.ShapeDtypeStruct((n_groups, rows, n), out_dtype),
        compiler_params=_params("arbitrary", "arbitrary"),
        name="mixer_proj",
    )(x, mods, w)


def _outproj_kernel(*refs, n_lat, has_ctx):
    if has_ctx:
        al_ref, ac_ref, x_ref, m_ref, w_ref, g_ref, b_ref, o_ref = refs
        a = jnp.where(pl.program_id(0) == n_lat, ac_ref[...], al_ref[...])
    else:
        al_ref, x_ref, m_ref, w_ref, g_ref, b_ref, o_ref = refs
        a = al_ref[...]
    _, _, gate = _mod_rows(m_ref, 1)
    y = jnp.dot(a, w_ref[...], preferred_element_type=F32)
    t = ALPHA * x_ref[...] + gate * y
    o_ref[...] = _layer_norm(t, g_ref[...], b_ref[...])


def _mixer_outproj(a_lat, a_ctx, x, mods, l, w, ln_g, ln_b):
    n_lat, rows, k = a_lat.shape
    d = x.shape[-1]
    has_ctx = a_ctx is not None
    n_groups = n_lat + 1 if has_ctx else n_lat
    in_specs = [pl.BlockSpec((None, ROW_TILE, k), lambda g, i: (jnp.minimum(g, n_lat - 1), i, 0))]
    args = [a_lat]
    if has_ctx:
        in_specs.append(pl.BlockSpec((None, ROW_TILE, k), lambda g, i: (0, i, 0)))
        args.append(a_ctx)
    in_specs += [
        pl.BlockSpec((None, ROW_TILE, d), lambda g, i: (g, i, 0)),
        _mod_spec(mods, l),
        _resident(w.shape),
        _ln_spec(ln_g, l, 1),
        _ln_spec(ln_b, l, 1),
    ]
    args += [x, mods, w, ln_g, ln_b]
    return pl.pallas_call(
        functools.partial(_outproj_kernel, n_lat=n_lat, has_ctx=has_ctx),
        grid=(n_groups, rows // ROW_TILE),
        in_specs=in_specs,
        out_specs=pl.BlockSpec((None, ROW_TILE, d), lambda g, i: (g, i, 0)),
        out_shape=jax.ShapeDtypeStruct((n_groups, rows, d), F32),
        compiler_params=_params("arbitrary", "arbitrary"),
        name="mixer_outproj",
    )(*args)


def _pair_queries(q2):
    lane = lax.broadcasted_iota(jnp.int32, q2.shape, 1)
    zero = jnp.zeros_like(q2)
    qs = q2 * jnp.asarray(NA_HEAD_DIM ** -0.5, q2.dtype)
    return jnp.concatenate([jnp.where(lane < NA_HEAD_DIM, qs, zero),
                            jnp.where(lane < NA_HEAD_DIM, zero, qs)], axis=0)


def _unpair(o2):
    n = o2.shape[0] // 2
    lane = lax.broadcasted_iota(jnp.int32, (n, PAIR), 1)
    return jnp.where(lane < NA_HEAD_DIM, o2[:n], o2[n:])


def _scores(q, k):
    return lax.dot_general(q, k, (((1,), (1,)), ((), ())), preferred_element_type=F32)


def _na_kernel(q_ref, k_ref, v_ref, kc_ref, vc_ref, bias_ref, o_ref, s_scr, p_scr, *, n_rows):
    r = pl.program_id(1)
    r0 = jnp.clip(r - WIN_ROWS // 2, 0, n_rows - WIN_ROWS)
    start = pl.multiple_of(r0 * GRID_W, GRID_W)
    n_loc = WIN_ROWS * GRID_W
    n_pairs = NA_HEADS // 2
    for p in range(n_pairs):
        sl = slice(PAIR * p, PAIR * (p + 1))
        q = _pair_queries(q_ref[:, sl])
        s_scr[p, :, :n_loc] = _scores(q, k_ref[pl.ds(start, n_loc), sl]) + bias_ref[p]
        s_scr[p, :, n_loc:] = _scores(q, kc_ref[:, sl])
    s = s_scr[...]
    e = jnp.exp(s - jnp.max(s, axis=-1, keepdims=True))
    p_scr[...] = e.astype(BF16)
    inv = 1.0 / jnp.sum(e, axis=-1, keepdims=True)
    for p in range(n_pairs):
        sl = slice(PAIR * p, PAIR * (p + 1))
        o2 = (jnp.dot(p_scr[p, :, :n_loc], v_ref[pl.ds(start, n_loc), sl], preferred_element_type=F32)
              + jnp.dot(p_scr[p, :, n_loc:], vc_ref[:, sl], preferred_element_type=F32))
        o_ref[:, sl] = _unpair(o2 * inv[p]).astype(o_ref.dtype)


def _na_attention(qkv, bias, n_lat, ctx_len):
    _, rows, _ = qkv.shape
    d = D_MODEL
    n_rows = rows // GRID_W
    half = WIN_ROWS // 2

    def variant(r):
        return r - jnp.clip(r - half, 0, n_rows - WIN_ROWS)

    return pl.pallas_call(
        functools.partial(_na_kernel, n_rows=n_rows),
        grid=(n_lat, n_rows),
        in_specs=[
            pl.BlockSpec((None, GRID_W, d), lambda b, r: (b, r, 0)),
            pl.BlockSpec((None, rows, d), lambda b, r: (b, 0, 1)),
            pl.BlockSpec((None, rows, d), lambda b, r: (b, 0, 2)),
            pl.BlockSpec((None, ctx_len, d), lambda b, r: (n_lat, b, 1)),
            pl.BlockSpec((None, ctx_len, d), lambda b, r: (n_lat, b, 2)),
            pl.BlockSpec((None,) + bias.shape[1:], lambda b, r: (variant(r), 0, 0, 0)),
        ],
        out_specs=pl.BlockSpec((None, GRID_W, d), lambda b, r: (b, r, 0)),
        out_shape=jax.ShapeDtypeStruct((n_lat, rows, d), BF16),
        scratch_shapes=[pltpu.VMEM((NA_HEADS // 2, PAIR, WIN_ROWS * GRID_W + ctx_len), F32),
                        pltpu.VMEM((NA_HEADS // 2, PAIR, WIN_ROWS * GRID_W + ctx_len), BF16)],
        compiler_params=_params("arbitrary", "arbitrary"),
        name="na_attention",
    )(qkv, qkv, qkv, qkv, qkv, bias)


def _ctx_attn_kernel(q_ref, k_ref, v_ref, o_ref):
    for p in range(NA_HEADS // 2):
        sl = slice(PAIR * p, PAIR * (p + 1))
        s = _scores(_pair_queries(q_ref[:, sl]), k_ref[:, sl])
        e = jnp.exp(s - jnp.max(s, axis=-1, keepdims=True))
        pr = (e * (1.0 / jnp.sum(e, axis=-1, keepdims=True))).astype(BF16)
        o_ref[:, sl] = _unpair(jnp.dot(pr, v_ref[:, sl], preferred_element_type=F32)).astype(o_ref.dtype)


def _ctx_attention(qkv, n_lat, ctx_len):
    _, rows, _ = qkv.shape
    d = D_MODEL
    return pl.pallas_call(
        _ctx_attn_kernel,
        grid=(n_lat,),
        in_specs=[pl.BlockSpec((None, ctx_len, d), lambda b, c=c: (n_lat, b, c)) for c in range(3)],
        out_specs=pl.BlockSpec((None, ctx_len, d), lambda b: (0, b, 0)),
        out_shape=jax.ShapeDtypeStruct((1, rows, d), BF16),
        compiler_params=_params("arbitrary"),
        name="ctx_attention",
    )(qkv, qkv, qkv)


def _na_bias_table(rpb):
    col = jnp.arange(GRID_W)
    col_start = jnp.clip(col - WIN_COLS // 2, 0, GRID_W - WIN_COLS)
    kcol = col[None, :]
    col_in = (kcol >= col_start[:, None]) & (kcol < col_start[:, None] + WIN_COLS)
    dcol = jnp.clip(kcol - col[:, None], 1 - WIN_COLS, WIN_COLS - 1) + WIN_COLS - 1
    onehot = (dcol[None] == jnp.arange(2 * WIN_COLS - 1)[:, None, None]).astype(F32)
    toep = jnp.einsum('hdc,cqk->hqdk', rpb.astype(F32), onehot, precision=lax.Precision.HIGHEST)
    toep = jnp.where(col_in[None, :, None, :], toep, NEG_INF)
    tbl = jnp.stack([toep[:, :, WIN_ROWS - 1 - v:2 * WIN_ROWS - 1 - v, :] for v in range(WIN_ROWS)])
    return tbl.reshape(WIN_ROWS, NA_HEADS // 2, 2 * GRID_W, WIN_ROWS * GRID_W)


def _lru_tile_starts():
    starts = []
    for j in range(D_RNN // LRU_TILE):
        lo = (LRU_TILE * j // LRU_BLOCK_W) * LRU_BLOCK_W
        hi = ((LRU_TILE * (j + 1) - 1) // LRU_BLOCK_W + 1) * LRU_BLOCK_W
        ks = min(lo // LANES * LANES, D_RNN - LRU_KW)
        assert ks <= lo and hi <= ks + LRU_KW
        starts.append(ks)
    return starts


def _shift_rows(x, s, row):
    n = x.shape[0]
    if s > 0:
        return jnp.where(row >= s, pltpu.roll(x, s, 0), 0.0)
    return jnp.where(row < n + s, pltpu.roll(x, n + s, 0), 0.0)


def _dwconv(u, w_ref, row, left):
    acc = None
    for i in range(w_ref.shape[0]):
        s = left - i
        term = w_ref[i:i + 1, :] * (u if s == 0 else _shift_rows(u, s, row))
        acc = term if acc is None else acc + term
    return acc


def _lru_kernel(kb_ref, u0_ref, u1_ref, u2_ref, ut_ref, gate_ref, w0_ref, w1_ref, w2_ref, bg_ref,
                cw0_ref, cw1_ref, cw2_ref, cwt_ref, cb0_ref, cb1_ref, cb2_ref, cbt_ref, lam_ref, h0_ref,
                y_ref, hfin_ref, a_scr, b_scr, pa_scr, pb_scr, cin_scr, y_scr):
    del kb_ref
    t_len = ut_ref.shape[0]
    n_tiles = t_len // SUBLANES
    left = LRU_CONV_W // 2
    row = lax.broadcasted_iota(jnp.int32, (t_len, 1), 0)

    uc = jnp.concatenate(
        [_dwconv(u_ref[...], cw_ref, row, left) + cb_ref[...]
         for u_ref, cw_ref, cb_ref in ((u0_ref, cw0_ref, cb0_ref), (u1_ref, cw1_ref, cb1_ref),
                                       (u2_ref, cw2_ref, cb2_ref))], axis=1)
    wt = jnp.concatenate([jnp.concatenate([w_ref[q] for q in range(4)], axis=1)
                          for w_ref in (w0_ref, w1_ref, w2_ref)], axis=0)
    bg = jnp.concatenate([bg_ref[q:q + 1, :] for q in range(4)], axis=1)
    gates = jnp.dot(uc.astype(BF16), wt, preferred_element_type=F32) + bg
    ut = _dwconv(ut_ref[...], cwt_ref, row, left) + cbt_ref[...]

    for d in range(2):
        r = jax.nn.sigmoid(gates[:, (2 * d) * LRU_TILE:(2 * d + 1) * LRU_TILE])
        i = jax.nn.sigmoid(gates[:, (2 * d + 1) * LRU_TILE:(2 * d + 2) * LRU_TILE])
        neg_lam = -lam_ref[d:d + 1, :]
        softplus = jnp.maximum(neg_lam, 0.0) + jnp.log1p(jnp.exp(-jnp.abs(neg_lam)))
        a = jnp.exp(-LRU_C * r * softplus)
        a_scr[d] = a
        b_scr[d] = jnp.sqrt(1.0 - a * a) * (i * ut)

    rowk = lax.broadcasted_iota(jnp.int32, (n_tiles, 1), 0)
    for d, rev in ((0, False), (1, True)):
        acc_a = acc_b = None
        for n in range(SUBLANES):
            i = SUBLANES - 1 - n if rev else n
            ai = a_scr[d, pl.ds(i, n_tiles, stride=SUBLANES), :]
            bi = b_scr[d, pl.ds(i, n_tiles, stride=SUBLANES), :]
            if n == 0:
                acc_a, acc_b = ai, bi
            else:
                acc_b = ai * acc_b + bi
                acc_a = ai * acc_a
            pa_scr[d, i] = acc_a
            pb_scr[d, i] = acc_b
        s = 1
        while s < n_tiles:
            if rev:
                valid = rowk < n_tiles - s
                a_sh, b_sh = pltpu.roll(acc_a, n_tiles - s, 0), pltpu.roll(acc_b, n_tiles - s, 0)
            else:
                valid = rowk >= s
                a_sh, b_sh = pltpu.roll(acc_a, s, 0), pltpu.roll(acc_b, s, 0)
            acc_b = jnp.where(valid, acc_a * b_sh + acc_b, acc_b)
            acc_a = jnp.where(valid, acc_a * a_sh, acc_a)
            s *= 2
        h0 = h0_ref[d:d + 1, :]
        state = acc_a * h0 + acc_b
        if rev:
            cin_scr[d] = jnp.where(rowk < n_tiles - 1, pltpu.roll(state, n_tiles - 1, 0), h0)
            hfin_ref[d:d + 1, :] = state[0:1, :]
        else:
            cin_scr[d] = jnp.where(rowk >= 1, pltpu.roll(state, 1, 0), h0)
            hfin_ref[d:d + 1, :] = state[n_tiles - 1:n_tiles, :]

    for i in range(SUBLANES):
        h_sum = (pa_scr[0, i] * cin_scr[0] + pb_scr[0, i]) + (pa_scr[1, i] * cin_scr[1] + pb_scr[1, i])
        gi = gate_ref[pl.ds(i, n_tiles, stride=SUBLANES), :]
        y_scr[pl.ds(i, n_tiles, stride=SUBLANES), :] = jax.nn.gelu(gi) * h_sum
    y_ref[...] = y_scr[...].astype(y_ref.dtype)


def _lru_scan(proj, params, h0, n_seq, seq_len, group0, per_group):
    kb, dense, bg, conv_w, conv_b, lam = params
    n_tiles = D_RNN // LRU_TILE
    col0 = D_RNN // LRU_TILE
    rows = proj.shape[1]

    def seq_block(s):
        return group0 + s // per_group, s % per_group

    def u_spec(off):
        return pl.BlockSpec((None, seq_len, LRU_TILE),
                            lambda s, j, kb_ref: (*seq_block(s), col0 + kb_ref[j] + off))

    def win_spec(n_rows, off):
        return pl.BlockSpec((n_rows, LRU_TILE), lambda s, j, kb_ref: (0, kb_ref[j] + off))

    def tile_spec(n_rows):
        return pl.BlockSpec((n_rows, LRU_TILE), lambda s, j, kb_ref: (0, j))

    def w_spec(off):
        return pl.BlockSpec((4, LRU_TILE, LRU_TILE), lambda s, j, kb_ref: (0, kb_ref[j] + off, j))

    grid_spec = pltpu.PrefetchScalarGridSpec(
        num_scalar_prefetch=1,
        grid=(n_seq, n_tiles),
        in_specs=[
            u_spec(0), u_spec(1), u_spec(2),
            pl.BlockSpec((None, seq_len, LRU_TILE), lambda s, j, kb_ref: (*seq_block(s), col0 + j)),
            pl.BlockSpec((None, seq_len, LRU_TILE), lambda s, j, kb_ref: (*seq_block(s), j)),
            w_spec(0), w_spec(1), w_spec(2),
            tile_spec(4),
            win_spec(LRU_CONV_W, 0), win_spec(LRU_CONV_W, 1), win_spec(LRU_CONV_W, 2), tile_spec(LRU_CONV_W),
            win_spec(1, 0), win_spec(1, 1), win_spec(1, 2), tile_spec(1),
            tile_spec(2),
            pl.BlockSpec((None, 2, LRU_TILE), lambda s, j, kb_ref: (s, 0, j)),
        ],
        out_specs=[
            pl.BlockSpec((None, seq_len, LRU_TILE), lambda s, j, kb_ref: (s // per_group, s % per_group, j)),
            pl.BlockSpec((None, 2, LRU_TILE), lambda s, j, kb_ref: (s, 0, j)),
        ],
        scratch_shapes=[
            pltpu.VMEM((2, seq_len, LRU_TILE), F32),
            pltpu.VMEM((2, seq_len, LRU_TILE), F32),
            pltpu.VMEM((2, SUBLANES, seq_len // SUBLANES, LRU_TILE), F32),
            pltpu.VMEM((2, SUBLANES, seq_len // SUBLANES, LRU_TILE), F32),
            pltpu.VMEM((2, seq_len // SUBLANES, LRU_TILE), F32),
            pltpu.VMEM((seq_len, LRU_TILE), F32),
        ],
    )
    return pl.pallas_call(
        _lru_kernel,
        grid_spec=grid_spec,
        out_shape=[jax.ShapeDtypeStruct((n_seq // per_group, rows, D_RNN), BF16),
                   jax.ShapeDtypeStruct((n_seq, 2, D_RNN), F32)],
        compiler_params=_params("arbitrary", "arbitrary"),
        name="rglru_scan",
    )(kb, proj, proj, proj, proj, proj, dense, dense, dense, bg, conv_w, conv_w, conv_w, conv_w,
      conv_b, conv_b, conv_b, conv_b, lam, h0)


def _lru_params(conv_w, conv_b, w_gates, b_gates, lam):
    bw, nb = LRU_BLOCK_W, LRU_BLOCKS
    w = jnp.transpose(w_gates.reshape(4, nb, bw, bw), (0, 2, 1, 3))
    w = jnp.pad(w, ((0, 0), (0, 0), (0, 0), (0, D_RNN))).reshape(4, bw, nb * (bw + D_RNN))
    w = w[:, :, :nb * D_RNN].reshape(4, bw, nb, D_RNN)
    dense = jnp.transpose(w, (0, 2, 1, 3)).reshape(4, D_RNN, D_RNN).astype(BF16)
    kb = jnp.asarray([ks // LANES for ks in _lru_tile_starts()], jnp.int32)
    return kb, dense, b_gates.reshape(4, D_RNN), conv_w, conv_b.reshape(1, D_RNN), lam


def _sc_kernel(xm_ref, xp_ref, xn_ref, m_ref, w1_ref, w2_ref, cw_ref, g_ref, b_ref, o_ref, h_ref,
               *, n_lat, ctx_len, rows):
    tm = xm_ref.shape[0]
    ext = tm + 2 * HALO
    shift, scale, gate = _mod_rows(m_ref, 1)

    def mod(x):
        return (x * (1 + scale) + shift).astype(BF16)

    h_ref[0:HALO, :] = mod(xp_ref[...])
    h_ref[HALO:HALO + tm, :] = mod(xm_ref[...])
    h_ref[HALO + tm:ext, :] = mod(xn_ref[...])

    pos = pl.program_id(1) * tm + lax.broadcasted_iota(jnp.int32, (tm, 1), 0)
    is_ctx = pl.program_id(0) == n_lat
    seq_pos = jnp.where(is_ctx, pos & (ctx_len - 1), pos)
    seq_last = jnp.where(is_ctx, ctx_len - 1, rows - 1)
    has_prev = seq_pos != 0
    has_next = seq_pos != seq_last

    acc = None
    for c in range(w1_ref.shape[0]):
        t = jnp.dot(h_ref[...], w1_ref[c], preferred_element_type=F32)
        bgate = t[HALO:HALO + tm, :SC_CHUNK]
        w = t[:, SC_CHUNK:2 * SC_CHUNK] * t[:, 2 * SC_CHUNK:]
        w_prev = pltpu.roll(w, 1, 0)[HALO:HALO + tm]
        w_next = pltpu.roll(w, ext - 1, 0)[HALO:HALO + tm]
        cw = cw_ref[c]
        conv = (cw[0:1, :] * jnp.where(has_prev, w_prev, 0.0) + cw[1:2, :] * w[HALO:HALO + tm]
                + cw[2:3, :] * jnp.where(has_next, w_next, 0.0))
        y = jnp.dot((bgate * conv).astype(BF16), w2_ref[c], preferred_element_type=F32)
        acc = y if acc is None else acc + y
    t = ALPHA * xm_ref[...] + gate * acc
    o_ref[...] = _layer_norm(t, g_ref[...], b_ref[...])


def _sc_mixer(x, mods, l, w1, w2, cw, ln_g, ln_b, n_groups, n_lat, ctx_len):
    _, rows, d = x.shape
    tm = ROW_TILE
    halo_per_tile = tm // HALO
    n_halo = rows // HALO
    assert ctx_len & (ctx_len - 1) == 0 and SC_CONV_W == 3
    return pl.pallas_call(
        functools.partial(_sc_kernel, n_lat=n_lat, ctx_len=ctx_len, rows=rows),
        grid=(n_groups, rows // tm),
        in_specs=[
            pl.BlockSpec((None, tm, d), lambda g, i: (g, i, 0)),
            pl.BlockSpec((None, HALO, d), lambda g, i: (g, jnp.maximum(i * halo_per_tile - 1, 0), 0)),
            pl.BlockSpec((None, HALO, d), lambda g, i: (g, jnp.minimum((i + 1) * halo_per_tile, n_halo - 1), 0)),
            _mod_spec(mods, l),
            _resident(w1.shape),
            _resident(w2.shape),
            _resident(cw.shape),
            _ln_spec(ln_g, l, 1),
            _ln_spec(ln_b, l, 1),
        ],
        out_specs=pl.BlockSpec((None, tm, d), lambda g, i: (g, i, 0)),
        out_shape=jax.ShapeDtypeStruct((n_groups, rows, d), F32),
        scratch_shapes=[pltpu.VMEM((tm + 2 * HALO, d), BF16)],
        compiler_params=_params("arbitrary", "arbitrary"),
        name="sc_mixer",
    )(x, x, x, mods, w1, w2, cw, ln_g, ln_b)


def _chunk_cols(w, parts, chunk):
    *lead, k, pn = w.shape
    n = pn // parts
    w = w.reshape(*lead, k, parts, n // chunk, chunk)
    nl = len(lead)
    w = jnp.transpose(w, tuple(range(nl)) + (nl + 2, nl, nl + 1, nl + 3))
    return w.reshape(*lead, n // chunk, k, parts * chunk)


def kernel(x, c, ctx, c_ctx, mod_w, mod_b, ln_g, ln_b, ffn_w_in, ffn_w_out, na_w_qkv, na_w_o, na_rpb,
           lru_w_in, lru_conv_w, lru_conv_b, lru_w_gates, lru_b_gates, lru_lambda, lru_w_out,
           sc_w_in, sc_conv_w, sc_w_out):
    n_lat, rows, d = x.shape
    ctx_len = ctx.shape[1]
    assert d == D_MODEL and n_lat * ctx_len == rows and n_lat + 1 <= MOD_ROWS
    assert rows % ROW_TILE == 0 and rows % GRID_W == 0 and rows // GRID_W >= WIN_ROWS
    n_all = n_lat + 1

    x_ctx = ctx.reshape(1, rows, d)
    cond = jnp.concatenate([c, c_ctx[None], jnp.zeros((MOD_ROWS - n_all, d), F32)], axis=0)
    mods = _modulation(cond, mod_w, mod_b).reshape(DEPTH, MOD_ROWS, N_MOD, d)
    ln_g = ln_g.reshape(DEPTH * 3, 1, d)
    ln_b = ln_b.reshape(DEPTH * 3, 1, d)

    ffn_w1 = _ffn_w1_layout(ffn_w_in.reshape(DEPTH * 2, d, 2 * D_FF))
    ffn_w2 = ffn_w_out.reshape(DEPTH * 2, D_FF // FF_CHUNK, FF_CHUNK, d).astype(BF16)

    xs = x
    for l in range(DEPTH):
        kind = l % N_MIXERS
        idx = l // N_MIXERS
        ctx_out = l < DEPTH - 1
        ctx_in = ctx_out or kind != 2
        n_in = n_all if ctx_in else n_lat
        n_out = n_all if ctx_out else n_lat

        first_ctx = x_ctx if (l == 0 and ctx_in) else None
        xs = _ffn_half(xs, first_ctx, mods, l, 0, ffn_w1, ffn_w2, ln_g, ln_b, n_in)

        if kind == 0:
            qkv = _mixer_proj(xs, mods, l, na_w_qkv[idx].astype(BF16), BF16)
            a_lat = _na_attention(qkv, _na_bias_table(na_rpb[idx]), n_lat, ctx_len)
            a_ctx = _ctx_attention(qkv, n_lat, ctx_len) if ctx_out else None
            xs = _mixer_outproj(a_lat, a_ctx, xs, mods, l, na_w_o[idx].astype(BF16), ln_g, ln_b)
        elif kind == 1:
            proj = _mixer_proj(xs, mods, l, lru_w_in[idx].astype(BF16), F32)
            params = _lru_params(lru_conv_w[idx], lru_conv_b[idx], lru_w_gates[idx], lru_b_gates[idx],
                                 lru_lambda[idx])
            zeros = jnp.zeros((n_lat, 2, D_RNN), F32)
            a_ctx, h_ctx = _lru_scan(proj, params, zeros, n_lat, ctx_len, n_lat, n_lat)
            a_lat, _ = _lru_scan(proj, params, h_ctx, n_lat, rows, 0, 1)
            xs = _mixer_outproj(a_lat, a_ctx if ctx_out else None, xs, mods, l, lru_w_out[idx].astype(BF16),
                                ln_g, ln_b)
        else:
            w1 = _chunk_cols(sc_w_in[idx], 3, SC_CHUNK).astype(BF16)
            w2 = sc_w_out[idx].reshape(d // SC_CHUNK, SC_CHUNK, d).astype(BF16)
            cw = jnp.transpose(sc_conv_w[idx].reshape(SC_CONV_W, d // SC_CHUNK, SC_CHUNK), (1, 0, 2))
            xs = _sc_mixer(xs, mods, l, w1, w2, cw, ln_g, ln_b, n_out, n_lat, ctx_len)

        xs = _ffn_half(xs, None, mods, l, 2, ffn_w1, ffn_w2, ln_g, ln_b, n_out)
    return xs
```

```python
import functools

import jax
import jax.numpy as jnp
from jax import lax
from jax.experimental import pallas as pl
from jax.experimental.pallas import tpu as pltpu

F32 = jnp.float32
BF16 = jnp.bfloat16

D_MODEL = 1024
DEPTH = 4
GRID_W = 64
N_MIXERS = 3
NA_HEADS = 16
NA_HEAD_DIM = D_MODEL // NA_HEADS
WIN_ROWS = 8
WIN_COLS = 16
D_RNN = 1408
LRU_BLOCKS = 16
LRU_BLOCK_W = D_RNN // LRU_BLOCKS
LRU_CONV_W = 4
LRU_C = 8.0
SQRT_FLOOR = 1e-30
SC_CONV_W = 3
D_FF = 2816
N_MOD = 9
ALPHA = (2 * DEPTH) ** 0.25
LN_EPS = 1e-5
NEG_INF = -1e30

LANES = 128
SUBLANES = 8
VMEM_LIMIT_BYTES = 56 * 1024 * 1024

ROW_TILE = 512
FF_CHUNK = 256
SC_CHUNK = 256
W1_LAYOUT_ROWS = 256
MOD_COLS = 1024
MOD_ROWS = 16
LRU_TILE = LANES
LRU_KW = 3 * LANES
HALO = SUBLANES
PAIR = 2 * NA_HEAD_DIM


def _params(*sem):
    return pltpu.CompilerParams(dimension_semantics=sem, vmem_limit_bytes=VMEM_LIMIT_BYTES)


def _resident(shape):
    zeros = (0,) * len(shape)
    return pl.BlockSpec(shape, lambda *_: zeros, pipeline_mode=pl.Buffered(1))


def _layer_norm(t, g, b):
    mu = jnp.mean(t, axis=-1, keepdims=True)
    d = t - mu
    var = jnp.mean(d * d, axis=-1, keepdims=True)
    return d * lax.rsqrt(var + LN_EPS) * g + b


def _mod_spec(mods, l):
    return pl.BlockSpec((None, None) + mods.shape[2:], lambda g, i: (l, g, 0, 0))


def _ln_spec(ln, l, j):
    return pl.BlockSpec((None,) + ln.shape[1:], lambda g, i: (3 * l + j, 0, 0))


def _layer_weight_spec(w, idx):
    zeros = (0,) * (w.ndim - 1)
    return pl.BlockSpec((None,) + w.shape[1:], lambda *_: (idx,) + zeros, pipeline_mode=pl.Buffered(1))


def _mod_rows(m_ref, j):
    return m_ref[3 * j:3 * j + 1, :], m_ref[3 * j + 1:3 * j + 2, :], m_ref[3 * j + 2:3 * j + 3, :]


def _mod_kernel(c_ref, w_ref, b_ref, o_ref):
    c = c_ref[...]
    s = (c * jax.nn.sigmoid(c)).astype(BF16)
    o_ref[...] = jnp.dot(s, w_ref[...].astype(BF16), preferred_element_type=F32) + b_ref[...]


def _modulation(cond, mod_w, mod_b):
    depth, d, n = mod_w.shape
    return pl.pallas_call(
        _mod_kernel,
        grid=(depth, n // MOD_COLS),
        in_specs=[
            pl.BlockSpec((MOD_ROWS, d), lambda l, j: (0, 0)),
            pl.BlockSpec((None, d, MOD_COLS), lambda l, j: (l, 0, j)),
            pl.BlockSpec((None, 1, MOD_COLS), lambda l, j: (l, 0, j)),
        ],
        out_specs=pl.BlockSpec((None, MOD_ROWS, MOD_COLS), lambda l, j: (l, 0, j)),
        out_shape=jax.ShapeDtypeStruct((depth, MOD_ROWS, n), F32),
        compiler_params=_params("arbitrary", "arbitrary"),
        name="modulation",
    )(cond, mod_w, mod_b.reshape(depth, 1, n))


def _ffn_kernel(*refs, j, n_lat, has_ctx):
    if has_ctx:
        xl_ref, xc_ref, m_ref, w1_ref, w2_ref, g_ref, b_ref, o_ref, h_ref = refs
        x = jnp.where(pl.program_id(0) == n_lat, xc_ref[...], xl_ref[...])
    else:
        xl_ref, m_ref, w1_ref, w2_ref, g_ref, b_ref, o_ref, h_ref = refs
        x = xl_ref[...]
    shift, scale, gate = _mod_rows(m_ref, j)
    h_ref[...] = (x * (1 + scale) + shift).astype(BF16)
    acc = None
    for c in range(w1_ref.shape[0]):
        gu = jnp.dot(h_ref[...], w1_ref[c], preferred_element_type=F32)
        g = gu[:, :FF_CHUNK]
        u = gu[:, FF_CHUNK:]
        a = (g * jax.nn.sigmoid(g) * u).astype(BF16)
        y = jnp.dot(a, w2_ref[c], preferred_element_type=F32)
        acc = y if acc is None else acc + y
    t = ALPHA * x + (0.5 * gate) * acc
    o_ref[...] = _layer_norm(t, g_ref[...], b_ref[...])


def _ffn_half(x, x_ctx, mods, l, j, w1, w2, ln_g, ln_b, n_groups):
    n_lat, rows, d = x.shape
    has_ctx = x_ctx is not None
    if has_ctx:
        assert n_groups == n_lat + 1
        in_specs = [pl.BlockSpec((None, ROW_TILE, d), lambda g, i: (jnp.minimum(g, n_lat - 1), i, 0)),
                    pl.BlockSpec((None, ROW_TILE, d), lambda g, i: (0, i, 0))]
        args = [x, x_ctx]
    else:
        in_specs = [pl.BlockSpec((None, ROW_TILE, d), lambda g, i: (g, i, 0))]
        args = [x]
    slab = 2 * l + j // 2
    in_specs += [_mod_spec(mods, l), _layer_weight_spec(w1, slab), _layer_weight_spec(w2, slab),
                 _ln_spec(ln_g, l, j), _ln_spec(ln_b, l, j)]
    return pl.pallas_call(
        functools.partial(_ffn_kernel, j=j, n_lat=n_lat, has_ctx=has_ctx),
        grid=(n_groups, rows // ROW_TILE),
        in_specs=in_specs,
        out_specs=pl.BlockSpec((None, ROW_TILE, d), lambda g, i: (g, i, 0)),
        out_shape=jax.ShapeDtypeStruct((n_groups, rows, d), F32),
        scratch_shapes=[pltpu.VMEM((ROW_TILE, d), BF16)],
        compiler_params=_params("arbitrary", "arbitrary"),
        name="ffn_half",
    )(*args, mods, w1, w2, ln_g, ln_b)


def _w1_layout_kernel(g_ref, u_ref, o_ref):
    for c in range(o_ref.shape[0]):
        o_ref[c, :, :FF_CHUNK] = g_ref[:, FF_CHUNK * c:FF_CHUNK * (c + 1)].astype(BF16)
        o_ref[c, :, FF_CHUNK:] = u_ref[:, FF_CHUNK * c:FF_CHUNK * (c + 1)].astype(BF16)


def _ffn_w1_layout(w_in):
    n, d, _ = w_in.shape
    nc = D_FF // FF_CHUNK
    rt = W1_LAYOUT_ROWS
    return pl.pallas_call(
        _w1_layout_kernel,
        grid=(n, d // rt),
        in_specs=[pl.BlockSpec((None, rt, D_FF), lambda l, r: (l, r, 0)),
                  pl.BlockSpec((None, rt, D_FF), lambda l, r: (l, r, 1))],
        out_specs=pl.BlockSpec((None, nc, rt, 2 * FF_CHUNK), lambda l, r: (l, 0, r, 0)),
        out_shape=jax.ShapeDtypeStruct((n, nc, d, 2 * FF_CHUNK), BF16),
        compiler_params=_params("arbitrary", "arbitrary"),
        name="ffn_w1_layout",
    )(w_in, w_in)


def _proj_kernel(x_ref, m_ref, w_ref, o_ref):
    shift, scale, _ = _mod_rows(m_ref, 1)
    h = (x_ref[...] * (1 + scale) + shift).astype(BF16)
    o_ref[...] = jnp.dot(h, w_ref[...], preferred_element_type=F32).astype(o_ref.dtype)


def _mixer_proj(x, mods, l, w, out_dtype):
    n_groups, rows, d = x.shape
    n = w.shape[1]
    return pl.pallas_call(
        _proj_kernel,
        grid=(n_groups, rows // ROW_TILE),
        in_specs=[
            pl.BlockSpec((None, ROW_TILE, d), lambda g, i: (g, i, 0)),
            _mod_spec(mods, l),
            _resident(w.shape),
        ],
        out_specs=pl.BlockSpec((None, ROW_TILE, n), lambda g, i: (g, i, 0)),
        out_shape=jax.ShapeDtypeStruct((n_groups, rows, n), out_dtype),
        compiler_params=_params("arbitrary", "arbitrary"),
        name="mixer_proj",
    )(x, mods, w)


def _outproj_kernel(*refs, n_lat, has_ctx):
    if has_ctx:
        al_ref, ac_ref, x_ref, m_ref, w_ref, g_ref, b_ref, o_ref = refs
        a = jnp.where(pl.program_id(0) == n_lat, ac_ref[...], al_ref[...])
    else:
        al_ref, x_ref, m_ref, w_ref, g_ref, b_ref, o_ref = refs
        a = al_ref[...]
    _, _, gate = _mod_rows(m_ref, 1)
    y = jnp.dot(a, w_ref[...], preferred_element_type=F32)
    t = ALPHA * x_ref[...] + gate * y
    o_ref[...] = _layer_norm(t, g_ref[...], b_ref[...])


def _mixer_outproj(a_lat, a_ctx, x, mods, l, w, ln_g, ln_b):
    n_lat, rows, k = a_lat.shape
    d = x.shape[-1]
    has_ctx = a_ctx is not None
    n_groups = n_lat + 1 if has_ctx else n_lat
    in_specs = [pl.BlockSpec((None, ROW_TILE, k), lambda g, i: (jnp.minimum(g, n_lat - 1), i, 0))]
    args = [a_lat]
    if has_ctx:
        in_specs.append(pl.BlockSpec((None, ROW_TILE, k), lambda g, i: (0, i, 0)))
        args.append(a_ctx)
    in_specs += [
        pl.BlockSpec((None, ROW_TILE, d), lambda g, i: (g, i, 0)),
        _mod_spec(mods, l),
        _resident(w.shape),
        _ln_spec(ln_g, l, 1),
        _ln_spec(ln_b, l, 1),
    ]
    args += [x, mods, w, ln_g, ln_b]
    return pl.pallas_call(
        functools.partial(_outproj_kernel, n_lat=n_lat, has_ctx=has_ctx),
        grid=(n_groups, rows // ROW_TILE),
        in_specs=in_specs,
        out_specs=pl.BlockSpec((None, ROW_TILE, d), lambda g, i: (g, i, 0)),
        out_shape=jax.ShapeDtypeStruct((n_groups, rows, d), F32),
        compiler_params=_params("arbitrary", "arbitrary"),
        name="mixer_outproj",
    )(*args)


def _pair_queries(q2):
    lane = lax.broadcasted_iota(jnp.int32, q2.shape, 1)
    zero = jnp.zeros_like(q2)
    qs = q2 * jnp.asarray(NA_HEAD_DIM ** -0.5, q2.dtype)
    return jnp.concatenate([jnp.where(lane < NA_HEAD_DIM, qs, zero),
                            jnp.where(lane < NA_HEAD_DIM, zero, qs)], axis=0)


def _unpair(o2):
    n = o2.shape[0] // 2
    lane = lax.broadcasted_iota(jnp.int32, (n, PAIR), 1)
    return jnp.where(lane < NA_HEAD_DIM, o2[:n], o2[n:])


def _scores(q, k):
    return lax.dot_general(q, k, (((1,), (1,)), ((), ())), preferred_element_type=F32)


def _na_kernel(q_ref, k_ref, v_ref, kc_ref, vc_ref, bias_ref, o_ref, s_scr, p_scr, *, n_rows):
    r = pl.program_id(1)
    r0 = jnp.clip(r - WIN_ROWS // 2, 0, n_rows - WIN_ROWS)
    start = pl.multiple_of(r0 * GRID_W, GRID_W)
    n_loc = WIN_ROWS * GRID_W
    n_pairs = NA_HEADS // 2
    for p in range(n_pairs):
        sl = slice(PAIR * p, PAIR * (p + 1))
        q = _pair_queries(q_ref[:, sl])
        s_scr[p, :, :n_loc] = _scores(q, k_ref[pl.ds(start, n_loc), sl]) + bias_ref[p]
        s_scr[p, :, n_loc:] = _scores(q, kc_ref[:, sl])
    s = s_scr[...]
    e = jnp.exp(s - jnp.max(s, axis=-1, keepdims=True))
    p_scr[...] = e.astype(BF16)
    inv = 1.0 / jnp.sum(e, axis=-1, keepdims=True)
    for p in range(n_pairs):
        sl = slice(PAIR * p, PAIR * (p + 1))
        o2 = (jnp.dot(p_scr[p, :, :n_loc], v_ref[pl.ds(start, n_loc), sl], preferred_element_type=F32)
              + jnp.dot(p_scr[p, :, n_loc:], vc_ref[:, sl], preferred_element_type=F32))
        o_ref[:, sl] = _unpair(o2 * inv[p]).astype(o_ref.dtype)


def _na_attention(qkv, bias, idx, n_lat, ctx_len):
    _, rows, _ = qkv.shape
    d = D_MODEL
    n_rows = rows // GRID_W
    half = WIN_ROWS // 2

    def variant(r):
        return r - jnp.clip(r - half, 0, n_rows - WIN_ROWS)

    return pl.pallas_call(
        functools.partial(_na_kernel, n_rows=n_rows),
        grid=(n_lat, n_rows),
        in_specs=[
            pl.BlockSpec((None, GRID_W, d), lambda b, r: (b, r, 0)),
            pl.BlockSpec((None, rows, d), lambda b, r: (b, 0, 1)),
            pl.BlockSpec((None, rows, d), lambda b, r: (b, 0, 2)),
            pl.BlockSpec((None, ctx_len, d), lambda b, r: (n_lat, b, 1)),
            pl.BlockSpec((None, ctx_len, d), lambda b, r: (n_lat, b, 2)),
            pl.BlockSpec((None, None) + bias.shape[2:], lambda b, r: (idx, variant(r), 0, 0, 0)),
        ],
        out_specs=pl.BlockSpec((None, GRID_W, d), lambda b, r: (b, r, 0)),
        out_shape=jax.ShapeDtypeStruct((n_lat, rows, d), BF16),
        scratch_shapes=[pltpu.VMEM((NA_HEADS // 2, PAIR, WIN_ROWS * GRID_W + ctx_len), F32),
                        pltpu.VMEM((NA_HEADS // 2, PAIR, WIN_ROWS * GRID_W + ctx_len), BF16)],
        compiler_params=_params("arbitrary", "arbitrary"),
        name="na_attention",
    )(qkv, qkv, qkv, qkv, qkv, bias)


def _ctx_attn_kernel(q_ref, k_ref, v_ref, o_ref):
    for p in range(NA_HEADS // 2):
        sl = slice(PAIR * p, PAIR * (p + 1))
        s = _scores(_pair_queries(q_ref[:, sl]), k_ref[:, sl])
        e = jnp.exp(s - jnp.max(s, axis=-1, keepdims=True))
        pr = (e * (1.0 / jnp.sum(e, axis=-1, keepdims=True))).astype(BF16)
        o_ref[:, sl] = _unpair(jnp.dot(pr, v_ref[:, sl], preferred_element_type=F32)).astype(o_ref.dtype)


def _ctx_attention(qkv, n_lat, ctx_len):
    _, rows, _ = qkv.shape
    d = D_MODEL
    return pl.pallas_call(
        _ctx_attn_kernel,
        grid=(n_lat,),
        in_specs=[pl.BlockSpec((None, ctx_len, d), lambda b, c=c: (n_lat, b, c)) for c in range(3)],
        out_specs=pl.BlockSpec((None, ctx_len, d), lambda b: (0, b, 0)),
        out_shape=jax.ShapeDtypeStruct((1, rows, d), BF16),
        compiler_params=_params("arbitrary"),
        name="ctx_attention",
    )(qkv, qkv, qkv)


def _na_bias_table(rpb):
    col = jnp.arange(GRID_W)
    col_start = jnp.clip(col - WIN_COLS // 2, 0, GRID_W - WIN_COLS)
    kcol = col[None, :]
    col_in = (kcol >= col_start[:, None]) & (kcol < col_start[:, None] + WIN_COLS)
    dcol = jnp.clip(kcol - col[:, None], 1 - WIN_COLS, WIN_COLS - 1) + WIN_COLS - 1
    drow = jnp.arange(WIN_ROWS)[None, :] - jnp.arange(WIN_ROWS)[:, None] + WIN_ROWS - 1
    hot_c = (dcol[None] == jnp.arange(2 * WIN_COLS - 1)[:, None, None]).astype(F32)
    hot_d = (drow[:, :, None] == jnp.arange(2 * WIN_ROWS - 1)).astype(F32)
    tbl = jnp.einsum('nhdc,vjd,cqk->nvhqjk', rpb.astype(F32), hot_d, hot_c, precision=lax.Precision.HIGHEST)
    tbl = jnp.where(col_in[:, None, :], tbl, NEG_INF)
    return tbl.reshape(rpb.shape[0], WIN_ROWS, NA_HEADS // 2, 2 * GRID_W, WIN_ROWS * GRID_W)


def _lru_tile_starts():
    starts = []
    for j in range(D_RNN // LRU_TILE):
        lo = (LRU_TILE * j // LRU_BLOCK_W) * LRU_BLOCK_W
        hi = ((LRU_TILE * (j + 1) - 1) // LRU_BLOCK_W + 1) * LRU_BLOCK_W
        ks = min(lo // LANES * LANES, D_RNN - LRU_KW)
        assert ks <= lo and hi <= ks + LRU_KW
        starts.append(ks)
    return starts


def _shift_fill(x, s, row):
    n = x.shape[0]
    if s > 0:
        return jnp.where(row >= s, pltpu.roll(x, s, 0), 0.0)
    return jnp.where(row < n + s, pltpu.roll(x, n + s, 0), 0.0)


def _dwconv_slabs(slabs, w_ref, b_ref, row):
    n = len(slabs)
    left = w_ref.shape[0] // 2
    wrap = {}

    def at(i):
        if 0 <= i < n:
            return slabs[i]
        if i not in wrap:
            wrap[i] = _shift_fill(slabs[i % n], 1 if i < 0 else -1, row)
        return wrap[i]

    out = []
    for i in range(n):
        acc = b_ref[...]
        for tap in range(w_ref.shape[0]):
            acc = acc + w_ref[tap:tap + 1, :] * at(i + tap - left)
        out.append(acc)
    return out


def _lru_kernel(*refs):
    n_sub = SUBLANES
    pos = 1
    u_refs = [refs[pos + w * n_sub:pos + (w + 1) * n_sub] for w in range(3)]
    pos += 3 * n_sub
    ut_refs = refs[pos:pos + n_sub]
    pos += n_sub
    gate_refs = refs[pos:pos + n_sub]
    pos += n_sub
    (w0_ref, w1_ref, w2_ref, bg_ref, cw0_ref, cw1_ref, cw2_ref, cwt_ref, cb0_ref, cb1_ref, cb2_ref, cbt_ref,
     lam_ref, h0_ref) = refs[pos:pos + 14]
    pos += 14
    y_ref, hfin_ref, pa_scr, pb_scr, y_scr = refs[pos:]

    n_tiles = ut_refs[0].shape[0]
    rowk = lax.broadcasted_iota(jnp.int32, (n_tiles, 1), 0)

    conv_w = [_dwconv_slabs([r[...] for r in u_refs[w]], cw, cb, rowk)
              for w, (cw, cb) in enumerate(((cw0_ref, cb0_ref), (cw1_ref, cb1_ref), (cw2_ref, cb2_ref)))]
    uc = jnp.concatenate([jnp.concatenate([conv_w[w][i] for w in range(3)], axis=1) for i in range(n_sub)],
                         axis=0)
    wt = jnp.concatenate([jnp.concatenate([w_ref[q] for q in range(4)], axis=1)
                          for w_ref in (w0_ref, w1_ref, w2_ref)], axis=0)
    bg = jnp.concatenate([bg_ref[q:q + 1, :] for q in range(4)], axis=1)
    gates = jnp.dot(uc.astype(BF16), wt, preferred_element_type=F32) + bg
    ut = _dwconv_slabs([r[...] for r in ut_refs], cwt_ref, cbt_ref, rowk)

    cin = []
    for d, rev in ((0, False), (1, True)):
        neg_lam = -lam_ref[d:d + 1, :]
        softplus = jnp.maximum(neg_lam, 0.0) + jnp.log1p(jnp.exp(-jnp.abs(neg_lam)))
        acc_a = acc_b = None
        for n in range(n_sub):
            i = n_sub - 1 - n if rev else n
            g_i = gates[i * n_tiles:(i + 1) * n_tiles]
            r = jax.nn.sigmoid(g_i[:, (2 * d) * LRU_TILE:(2 * d + 1) * LRU_TILE])
            gi = jax.nn.sigmoid(g_i[:, (2 * d + 1) * LRU_TILE:(2 * d + 2) * LRU_TILE])
            ai = jnp.exp(-LRU_C * r * softplus)
            v = 1.0 - ai * ai
            bi = (v * lax.rsqrt(jnp.maximum(v, SQRT_FLOOR))) * (gi * ut[i])
            if n == 0:
                acc_a, acc_b = ai, bi
            else:
                acc_b = ai * acc_b + bi
                acc_a = ai * acc_a
            pa_scr[d, i] = acc_a
            pb_scr[d, i] = acc_b
        s = 1
        while s < n_tiles:
            if rev:
                valid = rowk < n_tiles - s
                a_sh, b_sh = pltpu.roll(acc_a, n_tiles - s, 0), pltpu.roll(acc_b, n_tiles - s, 0)
            else:
                valid = rowk >= s
                a_sh, b_sh = pltpu.roll(acc_a, s, 0), pltpu.roll(acc_b, s, 0)
            acc_b = jnp.where(valid, acc_a * b_sh + acc_b, acc_b)
            acc_a = jnp.where(valid, acc_a * a_sh, acc_a)
            s *= 2
        h0 = h0_ref[d:d + 1, :]
        state = acc_a * h0 + acc_b
        if rev:
            cin.append(jnp.where(rowk < n_tiles - 1, pltpu.roll(state, n_tiles - 1, 0), h0))
            hfin_ref[d:d + 1, :] = state[0:1, :]
        else:
            cin.append(jnp.where(rowk >= 1, pltpu.roll(state, 1, 0), h0))
            hfin_ref[d:d + 1, :] = state[n_tiles - 1:n_tiles, :]

    for i in range(n_sub):
        h_sum = (pa_scr[0, i] * cin[0] + pb_scr[0, i]) + (pa_scr[1, i] * cin[1] + pb_scr[1, i])
        y_scr[pl.ds(i, n_tiles, stride=n_sub), :] = jax.nn.gelu(gate_refs[i][...]) * h_sum
    y_ref[...] = y_scr[...].astype(y_ref.dtype)


def _lru_scan(proj, params, h0, n_seq, seq_len, group0, per_group):
    kb, dense, bg, conv_w, conv_b, lam = params
    n_sub = SUBLANES
    n_tiles = D_RNN // LRU_TILE
    in_blocks = 2 * D_RNN // LRU_TILE
    col0 = D_RNN // LRU_TILE
    g_all, rows, width = proj.shape
    nt = seq_len // n_sub
    proj8 = proj.reshape(g_all, rows // n_sub, n_sub * width)

    def seq_block(s):
        return group0 + s // per_group, s % per_group

    def sub_spec(i, col):
        return pl.BlockSpec((None, nt, LRU_TILE),
                            lambda s, j, kb_ref: (*seq_block(s), i * in_blocks + col(j, kb_ref)))

    def win_spec(n_rows, off):
        return pl.BlockSpec((n_rows, LRU_TILE), lambda s, j, kb_ref: (0, kb_ref[j] + off))

    def tile_spec(n_rows):
        return pl.BlockSpec((n_rows, LRU_TILE), lambda s, j, kb_ref: (0, j))

    def w_spec(off):
        return pl.BlockSpec((4, LRU_TILE, LRU_TILE), lambda s, j, kb_ref: (0, kb_ref[j] + off, j))

    in_specs = []
    for off in range(3):
        in_specs += [sub_spec(i, lambda j, kb_ref, off=off: col0 + kb_ref[j] + off) for i in range(n_sub)]
    in_specs += [sub_spec(i, lambda j, kb_ref: col0 + j) for i in range(n_sub)]
    in_specs += [sub_spec(i, lambda j, kb_ref: j) for i in range(n_sub)]
    in_specs += [
        w_spec(0), w_spec(1), w_spec(2),
        tile_spec(4),
        win_spec(LRU_CONV_W, 0), win_spec(LRU_CONV_W, 1), win_spec(LRU_CONV_W, 2), tile_spec(LRU_CONV_W),
        win_spec(1, 0), win_spec(1, 1), win_spec(1, 2), tile_spec(1),
        tile_spec(2),
        pl.BlockSpec((None, 2, LRU_TILE), lambda s, j, kb_ref: (s, 0, j)),
    ]
    grid_spec = pltpu.PrefetchScalarGridSpec(
        num_scalar_prefetch=1,
        grid=(n_seq, n_tiles),
        in_specs=in_specs,
        out_specs=[
            pl.BlockSpec((None, seq_len, LRU_TILE), lambda s, j, kb_ref: (s // per_group, s % per_group, j)),
            pl.BlockSpec((None, 2, LRU_TILE), lambda s, j, kb_ref: (s, 0, j)),
        ],
        scratch_shapes=[pltpu.VMEM((2, n_sub, nt, LRU_TILE), F32), pltpu.VMEM((2, n_sub, nt, LRU_TILE), F32),
                        pltpu.VMEM((seq_len, LRU_TILE), F32)],
    )
    return pl.pallas_call(
        _lru_kernel,
        grid_spec=grid_spec,
        out_shape=[jax.ShapeDtypeStruct((n_seq // per_group, rows, D_RNN), BF16),
                   jax.ShapeDtypeStruct((n_seq, 2, D_RNN), F32)],
        compiler_params=_params("arbitrary", "arbitrary"),
        name="rglru_scan",
    )(kb, *([proj8] * (5 * n_sub)), dense, dense, dense, bg, conv_w, conv_w, conv_w, conv_w,
      conv_b, conv_b, conv_b, conv_b, lam, h0)


def _lru_params(conv_w, conv_b, w_gates, b_gates, lam):
    bw, nb = LRU_BLOCK_W, LRU_BLOCKS
    w = jnp.transpose(w_gates.reshape(4, nb, bw, bw), (0, 2, 1, 3))
    w = jnp.pad(w, ((0, 0), (0, 0), (0, 0), (0, D_RNN))).reshape(4, bw, nb * (bw + D_RNN))
    w = w[:, :, :nb * D_RNN].reshape(4, bw, nb, D_RNN)
    dense = jnp.transpose(w, (0, 2, 1, 3)).reshape(4, D_RNN, D_RNN).astype(BF16)
    kb = jnp.asarray([ks // LANES for ks in _lru_tile_starts()], jnp.int32)
    return kb, dense, b_gates.reshape(4, D_RNN), conv_w, conv_b.reshape(1, D_RNN), lam


def _sc_kernel(xm_ref, xp_ref, xn_ref, m_ref, w1_ref, w2_ref, cw_ref, g_ref, b_ref, o_ref, h_ref,
               *, n_lat, ctx_len, rows):
    tm = xm_ref.shape[0]
    ext = tm + 2 * HALO
    shift, scale, gate = _mod_rows(m_ref, 1)

    def mod(x):
        return (x * (1 + scale) + shift).astype(BF16)

    h_ref[0:HALO, :] = mod(xp_ref[...])
    h_ref[HALO:HALO + tm, :] = mod(xm_ref[...])
    h_ref[HALO + tm:ext, :] = mod(xn_ref[...])

    pos = pl.program_id(1) * tm + lax.broadcasted_iota(jnp.int32, (tm, 1), 0)
    is_ctx = pl.program_id(0) == n_lat
    seq_pos = jnp.where(is_ctx, pos & (ctx_len - 1), pos)
    seq_last = jnp.where(is_ctx, ctx_len - 1, rows - 1)
    has_prev = seq_pos != 0
    has_next = seq_pos != seq_last

    acc = None
    for c in range(w1_ref.shape[0]):
        t = jnp.dot(h_ref[...], w1_ref[c], preferred_element_type=F32)
        bgate = t[HALO:HALO + tm, :SC_CHUNK]
        w = t[:, SC_CHUNK:2 * SC_CHUNK] * t[:, 2 * SC_CHUNK:]
        w_prev = pltpu.roll(w, 1, 0)[HALO:HALO + tm]
        w_next = pltpu.roll(w, ext - 1, 0)[HALO:HALO + tm]
        cw = cw_ref[c]
        conv = (cw[0:1, :] * jnp.where(has_prev, w_prev, 0.0) + cw[1:2, :] * w[HALO:HALO + tm]
                + cw[2:3, :] * jnp.where(has_next, w_next, 0.0))
        y = jnp.dot((bgate * conv).astype(BF16), w2_ref[c], preferred_element_type=F32)
        acc = y if acc is None else acc + y
    t = ALPHA * xm_ref[...] + gate * acc
    o_ref[...] = _layer_norm(t, g_ref[...], b_ref[...])


def _sc_mixer(x, mods, l, w1, w2, cw, ln_g, ln_b, n_groups, n_lat, ctx_len):
    _, rows, d = x.shape
    tm = ROW_TILE
    halo_per_tile = tm // HALO
    n_halo = rows // HALO
    assert ctx_len & (ctx_len - 1) == 0 and SC_CONV_W == 3
    return pl.pallas_call(
        functools.partial(_sc_kernel, n_lat=n_lat, ctx_len=ctx_len, rows=rows),
        grid=(n_groups, rows // tm),
        in_specs=[
            pl.BlockSpec((None, tm, d), lambda g, i: (g, i, 0)),
            pl.BlockSpec((None, HALO, d), lambda g, i: (g, jnp.maximum(i * halo_per_tile - 1, 0), 0)),
            pl.BlockSpec((None, HALO, d), lambda g, i: (g, jnp.minimum((i + 1) * halo_per_tile, n_halo - 1), 0)),
            _mod_spec(mods, l),
            _resident(w1.shape),
            _resident(w2.shape),
            _resident(cw.shape),
            _ln_spec(ln_g, l, 1),
            _ln_spec(ln_b, l, 1),
        ],
        out_specs=pl.BlockSpec((None, tm, d), lambda g, i: (g, i, 0)),
        out_shape=jax.ShapeDtypeStruct((n_groups, rows, d), F32),
        scratch_shapes=[pltpu.VMEM((tm + 2 * HALO, d), BF16)],
        compiler_params=_params("arbitrary", "arbitrary"),
        name="sc_mixer",
    )(x, x, x, mods, w1, w2, cw, ln_g, ln_b)


def _chunk_cols(w, parts, chunk):
    *lead, k, pn = w.shape
    n = pn // parts
    w = w.reshape(*lead, k, parts, n // chunk, chunk)
    nl = len(lead)
    w = jnp.transpose(w, tuple(range(nl)) + (nl + 2, nl, nl + 1, nl + 3))
    return w.reshape(*lead, n // chunk, k, parts * chunk)


def kernel(x, c, ctx, c_ctx, mod_w, mod_b, ln_g, ln_b, ffn_w_in, ffn_w_out, na_w_qkv, na_w_o, na_rpb,
           lru_w_in, lru_conv_w, lru_conv_b, lru_w_gates, lru_b_gates, lru_lambda, lru_w_out,
           sc_w_in, sc_conv_w, sc_w_out):
    n_lat, rows, d = x.shape
    ctx_len = ctx.shape[1]
    assert d == D_MODEL and n_lat * ctx_len == rows and n_lat + 1 <= MOD_ROWS
    assert rows % ROW_TILE == 0 and rows % GRID_W == 0 and rows // GRID_W >= WIN_ROWS
    n_all = n_lat + 1

    x_ctx = ctx.reshape(1, rows, d)
    cond = jnp.concatenate([c, c_ctx[None], jnp.zeros((MOD_ROWS - n_all, d), F32)], axis=0)
    mods = _modulation(cond, mod_w, mod_b).reshape(DEPTH, MOD_ROWS, N_MOD, d)
    ln_g = ln_g.reshape(DEPTH * 3, 1, d)
    ln_b = ln_b.reshape(DEPTH * 3, 1, d)

    ffn_w1 = _ffn_w1_layout(ffn_w_in.reshape(DEPTH * 2, d, 2 * D_FF))
    ffn_w2 = ffn_w_out.reshape(DEPTH * 2, D_FF // FF_CHUNK, FF_CHUNK, d).astype(BF16)

    na_bias = _na_bias_table(na_rpb)

    xs = x
    for l in range(DEPTH):
        kind = l % N_MIXERS
        idx = l // N_MIXERS
        ctx_out = l < DEPTH - 1
        ctx_in = ctx_out or kind != 2
        n_in = n_all if ctx_in else n_lat
        n_out = n_all if ctx_out else n_lat

        first_ctx = x_ctx if (l == 0 and ctx_in) else None
        xs = _ffn_half(xs, first_ctx, mods, l, 0, ffn_w1, ffn_w2, ln_g, ln_b, n_in)

        if kind == 0:
            qkv = _mixer_proj(xs, mods, l, na_w_qkv[idx].astype(BF16), BF16)
            a_lat = _na_attention(qkv, na_bias, idx, n_lat, ctx_len)
            a_ctx = _ctx_attention(qkv, n_lat, ctx_len) if ctx_out else None
            xs = _mixer_outproj(a_lat, a_ctx, xs, mods, l, na_w_o[idx].astype(BF16), ln_g, ln_b)
        elif kind == 1:
            proj = _mixer_proj(xs, mods, l, lru_w_in[idx].astype(BF16), F32)
            params = _lru_params(lru_conv_w[idx], lru_conv_b[idx], lru_w_gates[idx], lru_b_gates[idx],
                                 lru_lambda[idx])
            zeros = jnp.zeros((n_lat, 2, D_RNN), F32)
            a_ctx, h_ctx = _lru_scan(proj, params, zeros, n_lat, ctx_len, n_lat, n_lat)
            a_lat, _ = _lru_scan(proj, params, h_ctx, n_lat, rows, 0, 1)
            xs = _mixer_outproj(a_lat, a_ctx if ctx_out else None, xs, mods, l, lru_w_out[idx].astype(BF16),
                                ln_g, ln_b)
        else:
            w1 = _chunk_cols(sc_w_in[idx], 3, SC_CHUNK).astype(BF16)
            w2 = sc_w_out[idx].reshape(d // SC_CHUNK, SC_CHUNK, d).astype(BF16)
            cw = jnp.transpose(sc_conv_w[idx].reshape(SC_CONV_W, d // SC_CHUNK, SC_CHUNK), (1, 0, 2))
            xs = _sc_mixer(xs, mods, l, w1, w2, cw, ln_g, ln_b, n_out, n_lat, ctx_len)

        xs = _ffn_half(xs, None, mods, l, 2, ffn_w1, ffn_w2, ln_g, ln_b, n_out)
    return xs
```

```python
import functools

import jax
import jax.numpy as jnp
from jax import lax
from jax.experimental import pallas as pl
from jax.experimental.pallas import tpu as pltpu

F32 = jnp.float32
BF16 = jnp.bfloat16

D_MODEL = 1024
DEPTH = 4
GRID_W = 64
N_MIXERS = 3
NA_HEADS = 16
NA_HEAD_DIM = D_MODEL // NA_HEADS
WIN_ROWS = 8
WIN_COLS = 16
D_RNN = 1408
LRU_BLOCKS = 16
LRU_BLOCK_W = D_RNN // LRU_BLOCKS
LRU_CONV_W = 4
LRU_C = 8.0
SQRT_FLOOR = 1e-30
SC_CONV_W = 3
D_FF = 2816
N_MOD = 9
ALPHA = (2 * DEPTH) ** 0.25
LN_EPS = 1e-5
NEG_INF = -1e30

LANES = 128
SUBLANES = 8
VMEM_LIMIT_BYTES = 56 * 1024 * 1024

ROW_TILE = 512
FF_CHUNK = 256
SC_CHUNK = 256
W1_LAYOUT_ROWS = 256
MOD_COLS = 1024
MOD_ROWS = 16
LRU_TILE = LANES
LRU_KW = 3 * LANES
HALO = SUBLANES
PAIR = 2 * NA_HEAD_DIM


def _params(*sem):
    return pltpu.CompilerParams(dimension_semantics=sem, vmem_limit_bytes=VMEM_LIMIT_BYTES)


def _resident(shape):
    zeros = (0,) * len(shape)
    return pl.BlockSpec(shape, lambda *_: zeros, pipeline_mode=pl.Buffered(1))


def _layer_norm(t, g, b):
    mu = jnp.mean(t, axis=-1, keepdims=True)
    d = t - mu
    var = jnp.mean(d * d, axis=-1, keepdims=True)
    return d * lax.rsqrt(var + LN_EPS) * g + b


def _mod_spec(mods, l):
    return pl.BlockSpec((None, None) + mods.shape[2:], lambda g, i: (l, g, 0, 0))


def _ln_spec(ln, l, j):
    return pl.BlockSpec((None,) + ln.shape[1:], lambda g, i: (3 * l + j, 0, 0))


def _layer_weight_spec(w, idx):
    zeros = (0,) * (w.ndim - 1)
    return pl.BlockSpec((None,) + w.shape[1:], lambda *_: (idx,) + zeros, pipeline_mode=pl.Buffered(1))


def _mod_rows(m_ref, j):
    return m_ref[3 * j:3 * j + 1, :], m_ref[3 * j + 1:3 * j + 2, :], m_ref[3 * j + 2:3 * j + 3, :]


def _mod_kernel(c_ref, w_ref, b_ref, o_ref):
    c = c_ref[...]
    s = (c * jax.nn.sigmoid(c)).astype(BF16)
    o_ref[...] = jnp.dot(s, w_ref[...].astype(BF16), preferred_element_type=F32) + b_ref[...]


def _modulation(cond, mod_w, mod_b):
    depth, d, n = mod_w.shape
    return pl.pallas_call(
        _mod_kernel,
        grid=(depth, n // MOD_COLS),
        in_specs=[
            pl.BlockSpec((MOD_ROWS, d), lambda l, j: (0, 0)),
            pl.BlockSpec((None, d, MOD_COLS), lambda l, j: (l, 0, j)),
            pl.BlockSpec((None, 1, MOD_COLS), lambda l, j: (l, 0, j)),
        ],
        out_specs=pl.BlockSpec((None, MOD_ROWS, MOD_COLS), lambda l, j: (l, 0, j)),
        out_shape=jax.ShapeDtypeStruct((depth, MOD_ROWS, n), F32),
        compiler_params=_params("arbitrary", "arbitrary"),
        name="modulation",
    )(cond, mod_w, mod_b.reshape(depth, 1, n))


def _ffn_kernel(*refs, j, n_lat, has_ctx):
    if has_ctx:
        xl_ref, xc_ref, m_ref, w1_ref, w2_ref, g_ref, b_ref, o_ref, h_ref = refs
        x = jnp.where(pl.program_id(0) == n_lat, xc_ref[...], xl_ref[...])
    else:
        xl_ref, m_ref, w1_ref, w2_ref, g_ref, b_ref, o_ref, h_ref = refs
        x = xl_ref[...]
    shift, scale, gate = _mod_rows(m_ref, j)
    h_ref[...] = (x * (1 + scale) + shift).astype(BF16)
    acc = None
    for c in range(w1_ref.shape[0]):
        gu = jnp.dot(h_ref[...], w1_ref[c], preferred_element_type=F32)
        g = gu[:, :FF_CHUNK]
        u = gu[:, FF_CHUNK:]
        a = (g * jax.nn.sigmoid(g) * u).astype(BF16)
        y = jnp.dot(a, w2_ref[c], preferred_element_type=F32)
        acc = y if acc is None else acc + y
    t = ALPHA * x + (0.5 * gate) * acc
    o_ref[...] = _layer_norm(t, g_ref[...], b_ref[...])


def _ffn_half(x, x_ctx, mods, l, j, w1, w2, ln_g, ln_b, n_groups):
    n_lat, rows, d = x.shape
    has_ctx = x_ctx is not None
    if has_ctx:
        assert n_groups == n_lat + 1
        in_specs = [pl.BlockSpec((None, ROW_TILE, d), lambda g, i: (jnp.minimum(g, n_lat - 1), i, 0)),
                    pl.BlockSpec((None, ROW_TILE, d), lambda g, i: (0, i, 0))]
        args = [x, x_ctx]
    else:
        in_specs = [pl.BlockSpec((None, ROW_TILE, d), lambda g, i: (g, i, 0))]
        args = [x]
    slab = 2 * l + j // 2
    in_specs += [_mod_spec(mods, l), _layer_weight_spec(w1, slab), _layer_weight_spec(w2, slab),
                 _ln_spec(ln_g, l, j), _ln_spec(ln_b, l, j)]
    return pl.pallas_call(
        functools.partial(_ffn_kernel, j=j, n_lat=n_lat, has_ctx=has_ctx),
        grid=(n_groups, rows // ROW_TILE),
        in_specs=in_specs,
        out_specs=pl.BlockSpec((None, ROW_TILE, d), lambda g, i: (g, i, 0)),
        out_shape=jax.ShapeDtypeStruct((n_groups, rows, d), F32),
        scratch_shapes=[pltpu.VMEM((ROW_TILE, d), BF16)],
        compiler_params=_params("arbitrary", "arbitrary"),
        name="ffn_half",
    )(*args, mods, w1, w2, ln_g, ln_b)


def _w1_layout_kernel(g_ref, u_ref, o_ref):
    for c in range(o_ref.shape[0]):
        o_ref[c, :, :FF_CHUNK] = g_ref[:, FF_CHUNK * c:FF_CHUNK * (c + 1)].astype(BF16)
        o_ref[c, :, FF_CHUNK:] = u_ref[:, FF_CHUNK * c:FF_CHUNK * (c + 1)].astype(BF16)


def _ffn_w1_layout(w_in):
    n, d, _ = w_in.shape
    nc = D_FF // FF_CHUNK
    rt = W1_LAYOUT_ROWS
    return pl.pallas_call(
        _w1_layout_kernel,
        grid=(n, d // rt),
        in_specs=[pl.BlockSpec((None, rt, D_FF), lambda l, r: (l, r, 0)),
                  pl.BlockSpec((None, rt, D_FF), lambda l, r: (l, r, 1))],
        out_specs=pl.BlockSpec((None, nc, rt, 2 * FF_CHUNK), lambda l, r: (l, 0, r, 0)),
        out_shape=jax.ShapeDtypeStruct((n, nc, d, 2 * FF_CHUNK), BF16),
        compiler_params=_params("arbitrary", "arbitrary"),
        name="ffn_w1_layout",
    )(w_in, w_in)


def _proj_kernel(x_ref, m_ref, w_ref, o_ref):
    shift, scale, _ = _mod_rows(m_ref, 1)
    h = (x_ref[...] * (1 + scale) + shift).astype(BF16)
    o_ref[...] = jnp.dot(h, w_ref[...], preferred_element_type=F32).astype(o_ref.dtype)


def _mixer_proj(x, mods, l, w, out_dtype):
    n_groups, rows, d = x.shape
    n = w.shape[1]
    return pl.pallas_call(
        _proj_kernel,
        grid=(n_groups, rows // ROW_TILE),
        in_specs=[
            pl.BlockSpec((None, ROW_TILE, d), lambda g, i: (g, i, 0)),
            _mod_spec(mods, l),
            _resident(w.shape),
        ],
        out_specs=pl.BlockSpec((None, ROW_TILE, n), lambda g, i: (g, i, 0)),
        out_shape=jax.ShapeDtypeStruct((n_groups, rows, n), out_dtype),
        compiler_params=_params("arbitrary", "arbitrary"),
        name="mixer_proj",
    )(x, mods, w)


def _outproj_kernel(*refs, n_lat, has_ctx):
    if has_ctx:
        al_ref, ac_ref, x_ref, m_ref, w_ref, g_ref, b_ref, o_ref = refs
        a = jnp.where(pl.program_id(0) == n_lat, ac_ref[...], al_ref[...])
    else:
        al_ref, x_ref, m_ref, w_ref, g_ref, b_ref, o_ref = refs
        a = al_ref[...]
    _, _, gate = _mod_rows(m_ref, 1)
    y = jnp.dot(a, w_ref[...], preferred_element_type=F32)
    t = ALPHA * x_ref[...] + gate * y
    o_ref[...] = _layer_norm(t, g_ref[...], b_ref[...])


def _mixer_outproj(a_lat, a_ctx, x, mods, l, w, ln_g, ln_b):
    n_lat, rows, k = a_lat.shape
    d = x.shape[-1]
    has_ctx = a_ctx is not None
    n_groups = n_lat + 1 if has_ctx else n_lat
    in_specs = [pl.BlockSpec((None, ROW_TILE, k), lambda g, i: (jnp.minimum(g, n_lat - 1), i, 0))]
    args = [a_lat]
    if has_ctx:
        in_specs.append(pl.BlockSpec((None, ROW_TILE, k), lambda g, i: (0, i, 0)))
        args.append(a_ctx)
    in_specs += [
        pl.BlockSpec((None, ROW_TILE, d), lambda g, i: (g, i, 0)),
        _mod_spec(mods, l),
        _resident(w.shape),
        _ln_spec(ln_g, l, 1),
        _ln_spec(ln_b, l, 1),
    ]
    args += [x, mods, w, ln_g, ln_b]
    return pl.pallas_call(
        functools.partial(_outproj_kernel, n_lat=n_lat, has_ctx=has_ctx),
        grid=(n_groups, rows // ROW_TILE),
        in_specs=in_specs,
        out_specs=pl.BlockSpec((None, ROW_TILE, d), lambda g, i: (g, i, 0)),
        out_shape=jax.ShapeDtypeStruct((n_groups, rows, d), F32),
        compiler_params=_params("arbitrary", "arbitrary"),
        name="mixer_outproj",
    )(*args)


def _pair_queries(q2):
    lane = lax.broadcasted_iota(jnp.int32, q2.shape, 1)
    zero = jnp.zeros_like(q2)
    qs = q2 * jnp.asarray(NA_HEAD_DIM ** -0.5, q2.dtype)
    return jnp.concatenate([jnp.where(lane < NA_HEAD_DIM, qs, zero),
                            jnp.where(lane < NA_HEAD_DIM, zero, qs)], axis=0)


def _unpair(o2):
    n = o2.shape[0] // 2
    lane = lax.broadcasted_iota(jnp.int32, (n, PAIR), 1)
    return jnp.where(lane < NA_HEAD_DIM, o2[:n], o2[n:])


def _scores(q, k):
    return lax.dot_general(q, k, (((1,), (1,)), ((), ())), preferred_element_type=F32)


def _na_kernel(q_ref, k_ref, v_ref, kc_ref, vc_ref, bias_ref, o_ref, s_scr, p_scr, *, n_rows):
    r = pl.program_id(1)
    r0 = jnp.clip(r - WIN_ROWS // 2, 0, n_rows - WIN_ROWS)
    start = pl.multiple_of(r0 * GRID_W, GRID_W)
    n_loc = WIN_ROWS * GRID_W
    n_pairs = NA_HEADS // 2
    for p in range(n_pairs):
        sl = slice(PAIR * p, PAIR * (p + 1))
        q = _pair_queries(q_ref[:, sl])
        s_scr[p, :, :n_loc] = _scores(q, k_ref[pl.ds(start, n_loc), sl]) + bias_ref[p]
        s_scr[p, :, n_loc:] = _scores(q, kc_ref[:, sl])
    s = s_scr[...]
    e = jnp.exp(s - jnp.max(s, axis=-1, keepdims=True))
    p_scr[...] = e.astype(BF16)
    inv = 1.0 / jnp.sum(e, axis=-1, keepdims=True)
    for p in range(n_pairs):
        sl = slice(PAIR * p, PAIR * (p + 1))
        o2 = (jnp.dot(p_scr[p, :, :n_loc], v_ref[pl.ds(start, n_loc), sl], preferred_element_type=F32)
              + jnp.dot(p_scr[p, :, n_loc:], vc_ref[:, sl], preferred_element_type=F32))
        o_ref[:, sl] = _unpair(o2 * inv[p]).astype(o_ref.dtype)


def _na_attention(qkv, bias, idx, n_lat, ctx_len):
    _, rows, _ = qkv.shape
    d = D_MODEL
    n_rows = rows // GRID_W
    half = WIN_ROWS // 2

    def variant(r):
        return r - jnp.clip(r - half, 0, n_rows - WIN_ROWS)

    return pl.pallas_call(
        functools.partial(_na_kernel, n_rows=n_rows),
        grid=(n_lat, n_rows),
        in_specs=[
            pl.BlockSpec((None, GRID_W, d), lambda b, r: (b, r, 0)),
            pl.BlockSpec((None, rows, d), lambda b, r: (b, 0, 1)),
            pl.BlockSpec((None, rows, d), lambda b, r: (b, 0, 2)),
            pl.BlockSpec((None, ctx_len, d), lambda b, r: (n_lat, b, 1)),
            pl.BlockSpec((None, ctx_len, d), lambda b, r: (n_lat, b, 2)),
            pl.BlockSpec((None, None) + bias.shape[2:], lambda b, r: (idx, variant(r), 0, 0, 0)),
        ],
        out_specs=pl.BlockSpec((None, GRID_W, d), lambda b, r: (b, r, 0)),
        out_shape=jax.ShapeDtypeStruct((n_lat, rows, d), BF16),
        scratch_shapes=[pltpu.VMEM((NA_HEADS // 2, PAIR, WIN_ROWS * GRID_W + ctx_len), F32),
                        pltpu.VMEM((NA_HEADS // 2, PAIR, WIN_ROWS * GRID_W + ctx_len), BF16)],
        compiler_params=_params("arbitrary", "arbitrary"),
        name="na_attention",
    )(qkv, qkv, qkv, qkv, qkv, bias)


def _ctx_attn_kernel(q_ref, k_ref, v_ref, o_ref):
    for p in range(NA_HEADS // 2):
        sl = slice(PAIR * p, PAIR * (p + 1))
        s = _scores(_pair_queries(q_ref[:, sl]), k_ref[:, sl])
        e = jnp.exp(s - jnp.max(s, axis=-1, keepdims=True))
        pr = (e * (1.0 / jnp.sum(e, axis=-1, keepdims=True))).astype(BF16)
        o_ref[:, sl] = _unpair(jnp.dot(pr, v_ref[:, sl], preferred_element_type=F32)).astype(o_ref.dtype)


def _ctx_attention(qkv, n_lat, ctx_len):
    _, rows, _ = qkv.shape
    d = D_MODEL
    return pl.pallas_call(
        _ctx_attn_kernel,
        grid=(n_lat,),
        in_specs=[pl.BlockSpec((None, ctx_len, d), lambda b, c=c: (n_lat, b, c)) for c in range(3)],
        out_specs=pl.BlockSpec((None, ctx_len, d), lambda b: (0, b, 0)),
        out_shape=jax.ShapeDtypeStruct((1, rows, d), BF16),
        compiler_params=_params("arbitrary"),
        name="ctx_attention",
    )(qkv, qkv, qkv)


def _na_bias_table(rpb):
    col = jnp.arange(GRID_W)
    col_start = jnp.clip(col - WIN_COLS // 2, 0, GRID_W - WIN_COLS)
    kcol = col[None, :]
    col_in = (kcol >= col_start[:, None]) & (kcol < col_start[:, None] + WIN_COLS)
    dcol = jnp.clip(kcol - col[:, None], 1 - WIN_COLS, WIN_COLS - 1) + WIN_COLS - 1
    drow = jnp.arange(WIN_ROWS)[None, :] - jnp.arange(WIN_ROWS)[:, None] + WIN_ROWS - 1
    hot_c = (dcol[None] == jnp.arange(2 * WIN_COLS - 1)[:, None, None]).astype(F32)
    hot_d = (drow[:, :, None] == jnp.arange(2 * WIN_ROWS - 1)).astype(F32)
    tbl = jnp.einsum('nhdc,vjd,cqk->nvhqjk', rpb.astype(F32), hot_d, hot_c, precision=lax.Precision.HIGHEST)
    tbl = jnp.where(col_in[:, None, :], tbl, NEG_INF)
    return tbl.reshape(rpb.shape[0], WIN_ROWS, NA_HEADS // 2, 2 * GRID_W, WIN_ROWS * GRID_W)


def _lru_tile_starts():
    starts = []
    for j in range(D_RNN // LRU_TILE):
        lo = (LRU_TILE * j // LRU_BLOCK_W) * LRU_BLOCK_W
        hi = ((LRU_TILE * (j + 1) - 1) // LRU_BLOCK_W + 1) * LRU_BLOCK_W
        ks = min(lo // LANES * LANES, D_RNN - LRU_KW)
        assert ks <= lo and hi <= ks + LRU_KW
        starts.append(ks)
    return starts


def _shift_rows(x, s, row):
    n = x.shape[0]
    if s > 0:
        return jnp.where(row >= s, pltpu.roll(x, s, 0), 0.0)
    return jnp.where(row < n + s, pltpu.roll(x, n + s, 0), 0.0)


def _dwconv(u, w_ref, row, left):
    acc = None
    for i in range(w_ref.shape[0]):
        s = left - i
        term = w_ref[i:i + 1, :] * (u if s == 0 else _shift_rows(u, s, row))
        acc = term if acc is None else acc + term
    return acc


def _lru_kernel(kb_ref, u0_ref, u1_ref, u2_ref, ut_ref, gate_ref, w0_ref, w1_ref, w2_ref, bg_ref,
                cw0_ref, cw1_ref, cw2_ref, cwt_ref, cb0_ref, cb1_ref, cb2_ref, cbt_ref, lam_ref, h0_ref,
                y_ref, hfin_ref, a_scr, b_scr, pa_scr, pb_scr, cin_scr, y_scr, l1a_scr, l1b_scr):
    del kb_ref
    t_len = ut_ref.shape[0]
    n_tiles = t_len // SUBLANES
    left = LRU_CONV_W // 2
    row = lax.broadcasted_iota(jnp.int32, (t_len, 1), 0)

    uc = jnp.concatenate(
        [_dwconv(u_ref[...], cw_ref, row, left) + cb_ref[...]
         for u_ref, cw_ref, cb_ref in ((u0_ref, cw0_ref, cb0_ref), (u1_ref, cw1_ref, cb1_ref),
                                       (u2_ref, cw2_ref, cb2_ref))], axis=1)
    wt = jnp.concatenate([jnp.concatenate([w_ref[q] for q in range(4)], axis=1)
                          for w_ref in (w0_ref, w1_ref, w2_ref)], axis=0)
    bg = jnp.concatenate([bg_ref[q:q + 1, :] for q in range(4)], axis=1)
    gates = jnp.dot(uc.astype(BF16), wt, preferred_element_type=F32) + bg
    ut = _dwconv(ut_ref[...], cwt_ref, row, left) + cbt_ref[...]

    for d in range(2):
        r = jax.nn.sigmoid(gates[:, (2 * d) * LRU_TILE:(2 * d + 1) * LRU_TILE])
        i = jax.nn.sigmoid(gates[:, (2 * d + 1) * LRU_TILE:(2 * d + 2) * LRU_TILE])
        neg_lam = -lam_ref[d:d + 1, :]
        softplus = jnp.maximum(neg_lam, 0.0) + jnp.log1p(jnp.exp(-jnp.abs(neg_lam)))
        a = jnp.exp(-LRU_C * r * softplus)
        a_scr[d] = a
        v = 1.0 - a * a
        b_scr[d] = (v * lax.rsqrt(jnp.maximum(v, SQRT_FLOOR))) * (i * ut)

    rowk = lax.broadcasted_iota(jnp.int32, (n_tiles, 1), 0)
    for d, rev in ((0, False), (1, True)):
        acc_a = acc_b = None
        for n in range(SUBLANES):
            i = SUBLANES - 1 - n if rev else n
            ai = a_scr[d, pl.ds(i, n_tiles, stride=SUBLANES), :]
            bi = b_scr[d, pl.ds(i, n_tiles, stride=SUBLANES), :]
            if n == 0:
                acc_a, acc_b = ai, bi
            else:
                acc_b = ai * acc_b + bi
                acc_a = ai * acc_a
            pa_scr[d, i] = acc_a
            pb_scr[d, i] = acc_b
        h0 = h0_ref[d:d + 1, :]
        if n_tiles >= SUBLANES * SUBLANES:
            n2 = n_tiles // SUBLANES
            l1a_scr[...] = acc_a
            l1b_scr[...] = acc_b
            row2 = lax.broadcasted_iota(jnp.int32, (n2, 1), 0)
            p2a, p2b = [None] * SUBLANES, [None] * SUBLANES
            ca = cb = None
            for n in range(SUBLANES):
                i = SUBLANES - 1 - n if rev else n
                ai = l1a_scr[pl.ds(i, n2, stride=SUBLANES), :]
                bi = l1b_scr[pl.ds(i, n2, stride=SUBLANES), :]
                if n == 0:
                    ca, cb = ai, bi
                else:
                    cb = ai * cb + bi
                    ca = ai * ca
                p2a[i], p2b[i] = ca, cb
            s = 1
            while s < n2:
                if rev:
                    valid = row2 < n2 - s
                    a_sh, b_sh = pltpu.roll(ca, n2 - s, 0), pltpu.roll(cb, n2 - s, 0)
                else:
                    valid = row2 >= s
                    a_sh, b_sh = pltpu.roll(ca, s, 0), pltpu.roll(cb, s, 0)
                cb = jnp.where(valid, ca * b_sh + cb, cb)
                ca = jnp.where(valid, ca * a_sh, ca)
                s *= 2
            st2 = ca * h0 + cb
            if rev:
                c2 = jnp.where(row2 < n2 - 1, pltpu.roll(st2, n2 - 1, 0), h0)
                hfin_ref[d:d + 1, :] = st2[0:1, :]
            else:
                c2 = jnp.where(row2 >= 1, pltpu.roll(st2, 1, 0), h0)
                hfin_ref[d:d + 1, :] = st2[n2 - 1:n2, :]
            prev = c2
            for n in range(SUBLANES):
                i = SUBLANES - 1 - n if rev else n
                cin_scr[d, pl.ds(i, n2, stride=SUBLANES), :] = prev
                prev = p2a[i] * c2 + p2b[i]
        else:
            s = 1
            while s < n_tiles:
                if rev:
                    valid = rowk < n_tiles - s
                    a_sh, b_sh = pltpu.roll(acc_a, n_tiles - s, 0), pltpu.roll(acc_b, n_tiles - s, 0)
                else:
                    valid = rowk >= s
                    a_sh, b_sh = pltpu.roll(acc_a, s, 0), pltpu.roll(acc_b, s, 0)
                acc_b = jnp.where(valid, acc_a * b_sh + acc_b, acc_b)
                acc_a = jnp.where(valid, acc_a * a_sh, acc_a)
                s *= 2
            state = acc_a * h0 + acc_b
            if rev:
                cin_scr[d] = jnp.where(rowk < n_tiles - 1, pltpu.roll(state, n_tiles - 1, 0), h0)
                hfin_ref[d:d + 1, :] = state[0:1, :]
            else:
                cin_scr[d] = jnp.where(rowk >= 1, pltpu.roll(state, 1, 0), h0)
                hfin_ref[d:d + 1, :] = state[n_tiles - 1:n_tiles, :]

    for i in range(SUBLANES):
        h_sum = (pa_scr[0, i] * cin_scr[0] + pb_scr[0, i]) + (pa_scr[1, i] * cin_scr[1] + pb_scr[1, i])
        gi = gate_ref[pl.ds(i, n_tiles, stride=SUBLANES), :]
        y_scr[pl.ds(i, n_tiles, stride=SUBLANES), :] = jax.nn.gelu(gi) * h_sum
    y_ref[...] = y_scr[...].astype(y_ref.dtype)


def _lru_scan(proj, params, h0, n_seq, seq_len, group0, per_group):
    kb, dense, bg, conv_w, conv_b, lam = params
    n_tiles = D_RNN // LRU_TILE
    col0 = D_RNN // LRU_TILE
    rows = proj.shape[1]

    def seq_block(s):
        return group0 + s // per_group, s % per_group

    def u_spec(off):
        return pl.BlockSpec((None, seq_len, LRU_TILE),
                            lambda s, j, kb_ref: (*seq_block(s), col0 + kb_ref[j] + off))

    def win_spec(n_rows, off):
        return pl.BlockSpec((n_rows, LRU_TILE), lambda s, j, kb_ref: (0, kb_ref[j] + off))

    def tile_spec(n_rows):
        return pl.BlockSpec((n_rows, LRU_TILE), lambda s, j, kb_ref: (0, j))

    def w_spec(off):
        return pl.BlockSpec((4, LRU_TILE, LRU_TILE), lambda s, j, kb_ref: (0, kb_ref[j] + off, j))

    grid_spec = pltpu.PrefetchScalarGridSpec(
        num_scalar_prefetch=1,
        grid=(n_seq, n_tiles),
        in_specs=[
            u_spec(0), u_spec(1), u_spec(2),
            pl.BlockSpec((None, seq_len, LRU_TILE), lambda s, j, kb_ref: (*seq_block(s), col0 + j)),
            pl.BlockSpec((None, seq_len, LRU_TILE), lambda s, j, kb_ref: (*seq_block(s), j)),
            w_spec(0), w_spec(1), w_spec(2),
            tile_spec(4),
            win_spec(LRU_CONV_W, 0), win_spec(LRU_CONV_W, 1), win_spec(LRU_CONV_W, 2), tile_spec(LRU_CONV_W),
            win_spec(1, 0), win_spec(1, 1), win_spec(1, 2), tile_spec(1),
            tile_spec(2),
            pl.BlockSpec((None, 2, LRU_TILE), lambda s, j, kb_ref: (s, 0, j)),
        ],
        out_specs=[
            pl.BlockSpec((None, seq_len, LRU_TILE), lambda s, j, kb_ref: (s // per_group, s % per_group, j)),
            pl.BlockSpec((None, 2, LRU_TILE), lambda s, j, kb_ref: (s, 0, j)),
        ],
        scratch_shapes=[
            pltpu.VMEM((2, seq_len, LRU_TILE), F32),
            pltpu.VMEM((2, seq_len, LRU_TILE), F32),
            pltpu.VMEM((2, SUBLANES, seq_len // SUBLANES, LRU_TILE), F32),
            pltpu.VMEM((2, SUBLANES, seq_len // SUBLANES, LRU_TILE), F32),
            pltpu.VMEM((2, seq_len // SUBLANES, LRU_TILE), F32),
            pltpu.VMEM((seq_len, LRU_TILE), F32),
            pltpu.VMEM((seq_len // SUBLANES, LRU_TILE), F32),
            pltpu.VMEM((seq_len // SUBLANES, LRU_TILE), F32),
        ],
    )
    return pl.pallas_call(
        _lru_kernel,
        grid_spec=grid_spec,
        out_shape=[jax.ShapeDtypeStruct((n_seq // per_group, rows, D_RNN), BF16),
                   jax.ShapeDtypeStruct((n_seq, 2, D_RNN), F32)],
        compiler_params=_params("arbitrary", "arbitrary"),
        name="rglru_scan",
    )(kb, proj, proj, proj, proj, proj, dense, dense, dense, bg, conv_w, conv_w, conv_w, conv_w,
      conv_b, conv_b, conv_b, conv_b, lam, h0)


def _lru_params(conv_w, conv_b, w_gates, b_gates, lam):
    bw, nb = LRU_BLOCK_W, LRU_BLOCKS
    w = jnp.transpose(w_gates.reshape(4, nb, bw, bw), (0, 2, 1, 3))
    w = jnp.pad(w, ((0, 0), (0, 0), (0, 0), (0, D_RNN))).reshape(4, bw, nb * (bw + D_RNN))
    w = w[:, :, :nb * D_RNN].reshape(4, bw, nb, D_RNN)
    dense = jnp.transpose(w, (0, 2, 1, 3)).reshape(4, D_RNN, D_RNN).astype(BF16)
    kb = jnp.asarray([ks // LANES for ks in _lru_tile_starts()], jnp.int32)
    return kb, dense, b_gates.reshape(4, D_RNN), conv_w, conv_b.reshape(1, D_RNN), lam


def _sc_kernel(xm_ref, xp_ref, xn_ref, m_ref, w1_ref, w2_ref, cw_ref, g_ref, b_ref, o_ref, h_ref,
               *, n_lat, ctx_len, rows):
    tm = xm_ref.shape[0]
    ext = tm + 2 * HALO
    shift, scale, gate = _mod_rows(m_ref, 1)

    def mod(x):
        return (x * (1 + scale) + shift).astype(BF16)

    h_ref[0:HALO, :] = mod(xp_ref[...])
    h_ref[HALO:HALO + tm, :] = mod(xm_ref[...])
    h_ref[HALO + tm:ext, :] = mod(xn_ref[...])

    pos = pl.program_id(1) * tm + lax.broadcasted_iota(jnp.int32, (tm, 1), 0)
    is_ctx = pl.program_id(0) == n_lat
    seq_pos = jnp.where(is_ctx, pos & (ctx_len - 1), pos)
    seq_last = jnp.where(is_ctx, ctx_len - 1, rows - 1)
    has_prev = seq_pos != 0
    has_next = seq_pos != seq_last

    acc = None
    for c in range(w1_ref.shape[0]):
        t = jnp.dot(h_ref[...], w1_ref[c], preferred_element_type=F32)
        bgate = t[HALO:HALO + tm, :SC_CHUNK]
        w = t[:, SC_CHUNK:2 * SC_CHUNK] * t[:, 2 * SC_CHUNK:]
        w_prev = pltpu.roll(w, 1, 0)[HALO:HALO + tm]
        w_next = pltpu.roll(w, ext - 1, 0)[HALO:HALO + tm]
        cw = cw_ref[c]
        conv = (cw[0:1, :] * jnp.where(has_prev, w_prev, 0.0) + cw[1:2, :] * w[HALO:HALO + tm]
                + cw[2:3, :] * jnp.where(has_next, w_next, 0.0))
        y = jnp.dot((bgate * conv).astype(BF16), w2_ref[c], preferred_element_type=F32)
        acc = y if acc is None else acc + y
    t = ALPHA * xm_ref[...] + gate * acc
    o_ref[...] = _layer_norm(t, g_ref[...], b_ref[...])


def _sc_mixer(x, mods, l, w1, w2, cw, ln_g, ln_b, n_groups, n_lat, ctx_len):
    _, rows, d = x.shape
    tm = ROW_TILE
    halo_per_tile = tm // HALO
    n_halo = rows // HALO
    assert ctx_len & (ctx_len - 1) == 0 and SC_CONV_W == 3
    return pl.pallas_call(
        functools.partial(_sc_kernel, n_lat=n_lat, ctx_len=ctx_len, rows=rows),
        grid=(n_groups, rows // tm),
        in_specs=[
            pl.BlockSpec((None, tm, d), lambda g, i: (g, i, 0)),
            pl.BlockSpec((None, HALO, d), lambda g, i: (g, jnp.maximum(i * halo_per_tile - 1, 0), 0)),
            pl.BlockSpec((None, HALO, d), lambda g, i: (g, jnp.minimum((i + 1) * halo_per_tile, n_halo - 1), 0)),
            _mod_spec(mods, l),
            _resident(w1.shape),
            _resident(w2.shape),
            _resident(cw.shape),
            _ln_spec(ln_g, l, 1),
            _ln_spec(ln_b, l, 1),
        ],
        out_specs=pl.BlockSpec((None, tm, d), lambda g, i: (g, i, 0)),
        out_shape=jax.ShapeDtypeStruct((n_groups, rows, d), F32),
        scratch_shapes=[pltpu.VMEM((tm + 2 * HALO, d), BF16)],
        compiler_params=_params("arbitrary", "arbitrary"),
        name="sc_mixer",
    )(x, x, x, mods, w1, w2, cw, ln_g, ln_b)


def _chunk_cols(w, parts, chunk):
    *lead, k, pn = w.shape
    n = pn // parts
    w = w.reshape(*lead, k, parts, n // chunk, chunk)
    nl = len(lead)
    w = jnp.transpose(w, tuple(range(nl)) + (nl + 2, nl, nl + 1, nl + 3))
    return w.reshape(*lead, n // chunk, k, parts * chunk)


def kernel(x, c, ctx, c_ctx, mod_w, mod_b, ln_g, ln_b, ffn_w_in, ffn_w_out, na_w_qkv, na_w_o, na_rpb,
           lru_w_in, lru_conv_w, lru_conv_b, lru_w_gates, lru_b_gates, lru_lambda, lru_w_out,
           sc_w_in, sc_conv_w, sc_w_out):
    n_lat, rows, d = x.shape
    ctx_len = ctx.shape[1]
    assert d == D_MODEL and n_lat * ctx_len == rows and n_lat + 1 <= MOD_ROWS
    assert rows % ROW_TILE == 0 and rows % GRID_W == 0 and rows // GRID_W >= WIN_ROWS
    n_all = n_lat + 1

    x_ctx = ctx.reshape(1, rows, d)
    cond = jnp.concatenate([c, c_ctx[None], jnp.zeros((MOD_ROWS - n_all, d), F32)], axis=0)
    mods = _modulation(cond, mod_w, mod_b).reshape(DEPTH, MOD_ROWS, N_MOD, d)
    ln_g = ln_g.reshape(DEPTH * 3, 1, d)
    ln_b = ln_b.reshape(DEPTH * 3, 1, d)

    ffn_w1 = _ffn_w1_layout(ffn_w_in.reshape(DEPTH * 2, d, 2 * D_FF))
    ffn_w2 = ffn_w_out.reshape(DEPTH * 2, D_FF // FF_CHUNK, FF_CHUNK, d).astype(BF16)

    na_bias = _na_bias_table(na_rpb)

    xs = x
    for l in range(DEPTH):
        kind = l % N_MIXERS
        idx = l // N_MIXERS
        ctx_out = l < DEPTH - 1
        ctx_in = ctx_out or kind != 2
        n_in = n_all if ctx_in else n_lat
        n_out = n_all if ctx_out else n_lat

        first_ctx = x_ctx if (l == 0 and ctx_in) else None
        xs = _ffn_half(xs, first_ctx, mods, l, 0, ffn_w1, ffn_w2, ln_g, ln_b, n_in)

        if kind == 0:
            qkv = _mixer_proj(xs, mods, l, na_w_qkv[idx].astype(BF16), BF16)
            a_lat = _na_attention(qkv, na_bias, idx, n_lat, ctx_len)
            a_ctx = _ctx_attention(qkv, n_lat, ctx_len) if ctx_out else None
            xs = _mixer_outproj(a_lat, a_ctx, xs, mods, l, na_w_o[idx].astype(BF16), ln_g, ln_b)
        elif kind == 1:
            proj = _mixer_proj(xs, mods, l, lru_w_in[idx].astype(BF16), F32)
            params = _lru_params(lru_conv_w[idx], lru_conv_b[idx], lru_w_gates[idx], lru_b_gates[idx],
                                 lru_lambda[idx])
            zeros = jnp.zeros((n_lat, 2, D_RNN), F32)
            a_ctx, h_ctx = _lru_scan(proj, params, zeros, n_lat, ctx_len, n_lat, n_lat)
            a_lat, _ = _lru_scan(proj, params, h_ctx, n_lat, rows, 0, 1)
            xs = _mixer_outproj(a_lat, a_ctx if ctx_out else None, xs, mods, l, lru_w_out[idx].astype(BF16),
                                ln_g, ln_b)
        else:
            w1 = _chunk_cols(sc_w_in[idx], 3, SC_CHUNK).astype(BF16)
            w2 = sc_w_out[idx].reshape(d // SC_CHUNK, SC_CHUNK, d).astype(BF16)
            cw = jnp.transpose(sc_conv_w[idx].reshape(SC_CONV_W, d // SC_CHUNK, SC_CHUNK), (1, 0, 2))
            xs = _sc_mixer(xs, mods, l, w1, w2, cw, ln_g, ln_b, n_out, n_lat, ctx_len)

        xs = _ffn_half(xs, None, mods, l, 2, ffn_w1, ffn_w2, ln_g, ln_b, n_out)
    return xs
```

```python
import functools

import jax
import jax.numpy as jnp
from jax import lax
from jax.experimental import pallas as pl
from jax.experimental.pallas import tpu as pltpu

F32 = jnp.float32
BF16 = jnp.bfloat16

D_MODEL = 1024
DEPTH = 4
GRID_W = 64
N_MIXERS = 3
NA_HEADS = 16
NA_HEAD_DIM = D_MODEL // NA_HEADS
WIN_ROWS = 8
WIN_COLS = 16
D_RNN = 1408
LRU_BLOCKS = 16
LRU_BLOCK_W = D_RNN // LRU_BLOCKS
LRU_CONV_W = 4
LRU_C = 8.0
SQRT_FLOOR = 1e-30
SC_CONV_W = 3
D_FF = 2816
N_MOD = 9
ALPHA = (2 * DEPTH) ** 0.25
LN_EPS = 1e-5
NEG_INF = -1e30

LANES = 128
SUBLANES = 8
VMEM_LIMIT_BYTES = 56 * 1024 * 1024

ROW_TILE = 512
FF_CHUNK = 256
SC_CHUNK = 512
W1_LAYOUT_ROWS = 256
MOD_COLS = 2304
MOD_ROWS = 16
LRU_TILE = LANES
LRU_KW = 3 * LANES
HALO = SUBLANES
PAIR = 2 * NA_HEAD_DIM
NA_ROWS_PER_STEP = 2


def _params(*sem):
    return pltpu.CompilerParams(dimension_semantics=sem, vmem_limit_bytes=VMEM_LIMIT_BYTES)


def _resident(shape):
    zeros = (0,) * len(shape)
    return pl.BlockSpec(shape, lambda *_: zeros, pipeline_mode=pl.Buffered(1))


def _layer_norm(t, g, b):
    mu = jnp.mean(t, axis=-1, keepdims=True)
    d = t - mu
    var = jnp.mean(d * d, axis=-1, keepdims=True)
    return d * lax.rsqrt(var + LN_EPS) * g + b


def _mod_spec(mods, l):
    return pl.BlockSpec((None, None) + mods.shape[2:], lambda g, i: (l, g, 0, 0))


def _ln_spec(ln, l, j):
    return pl.BlockSpec((None,) + ln.shape[1:], lambda g, i: (3 * l + j, 0, 0))


def _layer_weight_spec(w, idx):
    zeros = (0,) * (w.ndim - 1)
    return pl.BlockSpec((None,) + w.shape[1:], lambda *_: (idx,) + zeros, pipeline_mode=pl.Buffered(1))


def _mod_rows(m_ref, j):
    return m_ref[3 * j:3 * j + 1, :], m_ref[3 * j + 1:3 * j + 2, :], m_ref[3 * j + 2:3 * j + 3, :]


def _mod_kernel(c_ref, w_ref, b_ref, o_ref):
    c = c_ref[...]
    s = (c * jax.nn.sigmoid(c)).astype(BF16)
    o_ref[...] = jnp.dot(s, w_ref[...].astype(BF16), preferred_element_type=F32) + b_ref[...]


def _modulation(cond, mod_w, mod_b):
    depth, d, n = mod_w.shape
    return pl.pallas_call(
        _mod_kernel,
        grid=(depth, n // MOD_COLS),
        in_specs=[
            pl.BlockSpec((MOD_ROWS, d), lambda l, j: (0, 0)),
            pl.BlockSpec((None, d, MOD_COLS), lambda l, j: (l, 0, j)),
            pl.BlockSpec((None, 1, MOD_COLS), lambda l, j: (l, 0, j)),
        ],
        out_specs=pl.BlockSpec((None, MOD_ROWS, MOD_COLS), lambda l, j: (l, 0, j)),
        out_shape=jax.ShapeDtypeStruct((depth, MOD_ROWS, n), F32),
        compiler_params=_params("arbitrary", "arbitrary"),
        name="modulation",
    )(cond, mod_w, mod_b.reshape(depth, 1, n))


def _ffn_kernel(*refs, j, n_lat, has_ctx):
    if has_ctx:
        xl_ref, xc_ref, m_ref, w1_ref, w2_ref, g_ref, b_ref, o_ref, h_ref = refs
        x = jnp.where(pl.program_id(0) == n_lat, xc_ref[...], xl_ref[...])
    else:
        xl_ref, m_ref, w1_ref, w2_ref, g_ref, b_ref, o_ref, h_ref = refs
        x = xl_ref[...]
    shift, scale, gate = _mod_rows(m_ref, j)
    h_ref[...] = (x * (1 + scale) + shift).astype(BF16)
    acc = None
    for c in range(w1_ref.shape[0]):
        gu = jnp.dot(h_ref[...], w1_ref[c], preferred_element_type=F32)
        g = gu[:, :FF_CHUNK]
        u = gu[:, FF_CHUNK:]
        a = (g * jax.nn.sigmoid(g) * u).astype(BF16)
        y = jnp.dot(a, w2_ref[c], preferred_element_type=F32)
        acc = y if acc is None else acc + y
    t = ALPHA * x + (0.5 * gate) * acc
    o_ref[...] = _layer_norm(t, g_ref[...], b_ref[...])


def _ffn_half(x, x_ctx, mods, l, j, w1, w2, ln_g, ln_b, n_groups):
    n_lat, rows, d = x.shape
    has_ctx = x_ctx is not None
    if has_ctx:
        assert n_groups == n_lat + 1
        in_specs = [pl.BlockSpec((None, ROW_TILE, d), lambda g, i: (jnp.minimum(g, n_lat - 1), i, 0)),
                    pl.BlockSpec((None, ROW_TILE, d), lambda g, i: (0, i, 0))]
        args = [x, x_ctx]
    else:
        in_specs = [pl.BlockSpec((None, ROW_TILE, d), lambda g, i: (g, i, 0))]
        args = [x]
    slab = 2 * l + j // 2
    in_specs += [_mod_spec(mods, l), _layer_weight_spec(w1, slab), _layer_weight_spec(w2, slab),
                 _ln_spec(ln_g, l, j), _ln_spec(ln_b, l, j)]
    return pl.pallas_call(
        functools.partial(_ffn_kernel, j=j, n_lat=n_lat, has_ctx=has_ctx),
        grid=(n_groups, rows // ROW_TILE),
        in_specs=in_specs,
        out_specs=pl.BlockSpec((None, ROW_TILE, d), lambda g, i: (g, i, 0)),
        out_shape=jax.ShapeDtypeStruct((n_groups, rows, d), F32),
        scratch_shapes=[pltpu.VMEM((ROW_TILE, d), BF16)],
        compiler_params=_params("arbitrary", "arbitrary"),
        name="ffn_half",
    )(*args, mods, w1, w2, ln_g, ln_b)


def _w1_layout_kernel(g_ref, u_ref, o_ref):
    for c in range(o_ref.shape[0]):
        o_ref[c, :, :FF_CHUNK] = g_ref[:, FF_CHUNK * c:FF_CHUNK * (c + 1)].astype(BF16)
        o_ref[c, :, FF_CHUNK:] = u_ref[:, FF_CHUNK * c:FF_CHUNK * (c + 1)].astype(BF16)


def _ffn_w1_layout(w_in):
    n, d, _ = w_in.shape
    nc = D_FF // FF_CHUNK
    rt = W1_LAYOUT_ROWS
    return pl.pallas_call(
        _w1_layout_kernel,
        grid=(n, d // rt),
        in_specs=[pl.BlockSpec((None, rt, D_FF), lambda l, r: (l, r, 0)),
                  pl.BlockSpec((None, rt, D_FF), lambda l, r: (l, r, 1))],
        out_specs=pl.BlockSpec((None, nc, rt, 2 * FF_CHUNK), lambda l, r: (l, 0, r, 0)),
        out_shape=jax.ShapeDtypeStruct((n, nc, d, 2 * FF_CHUNK), BF16),
        compiler_params=_params("arbitrary", "arbitrary"),
        name="ffn_w1_layout",
    )(w_in, w_in)


def _proj_kernel(x_ref, m_ref, w_ref, o_ref):
    shift, scale, _ = _mod_rows(m_ref, 1)
    h = (x_ref[...] * (1 + scale) + shift).astype(BF16)
    o_ref[...] = jnp.dot(h, w_ref[...], preferred_element_type=F32).astype(o_ref.dtype)


def _mixer_proj(x, mods, l, w, out_dtype):
    n_groups, rows, d = x.shape
    n = w.shape[1]
    return pl.pallas_call(
        _proj_kernel,
        grid=(n_groups, rows // ROW_TILE),
        in_specs=[
            pl.BlockSpec((None, ROW_TILE, d), lambda g, i: (g, i, 0)),
            _mod_spec(mods, l),
            _resident(w.shape),
        ],
        out_specs=pl.BlockSpec((None, ROW_TILE, n), lambda g, i: (g, i, 0)),
        out_shape=jax.ShapeDtypeStruct((n_groups, rows, n), out_dtype),
        compiler_params=_params("arbitrary", "arbitrary"),
        name="mixer_proj",
    )(x, mods, w)


def _outproj_kernel(*refs, n_lat, has_ctx):
    if has_ctx:
        al_ref, ac_ref, x_ref, m_ref, w_ref, g_ref, b_ref, o_ref = refs
        a = jnp.where(pl.program_id(0) == n_lat, ac_ref[...], al_ref[...])
    else:
        al_ref, x_ref, m_ref, w_ref, g_ref, b_ref, o_ref = refs
        a = al_ref[...]
    _, _, gate = _mod_rows(m_ref, 1)
    y = jnp.dot(a, w_ref[...], preferred_element_type=F32)
    t = ALPHA * x_ref[...] + gate * y
    o_ref[...] = _layer_norm(t, g_ref[...], b_ref[...])


def _mixer_outproj(a_lat, a_ctx, x, mods, l, w, ln_g, ln_b):
    n_lat, rows, k = a_lat.shape
    d = x.shape[-1]
    has_ctx = a_ctx is not None
    n_groups = n_lat + 1 if has_ctx else n_lat
    in_specs = [pl.BlockSpec((None, ROW_TILE, k), lambda g, i: (jnp.minimum(g, n_lat - 1), i, 0))]
    args = [a_lat]
    if has_ctx:
        in_specs.append(pl.BlockSpec((None, ROW_TILE, k), lambda g, i: (0, i, 0)))
        args.append(a_ctx)
    in_specs += [
        pl.BlockSpec((None, ROW_TILE, d), lambda g, i: (g, i, 0)),
        _mod_spec(mods, l),
        _resident(w.shape),
        _ln_spec(ln_g, l, 1),
        _ln_spec(ln_b, l, 1),
    ]
    args += [x, mods, w, ln_g, ln_b]
    return pl.pallas_call(
        functools.partial(_outproj_kernel, n_lat=n_lat, has_ctx=has_ctx),
        grid=(n_groups, rows // ROW_TILE),
        in_specs=in_specs,
        out_specs=pl.BlockSpec((None, ROW_TILE, d), lambda g, i: (g, i, 0)),
        out_shape=jax.ShapeDtypeStruct((n_groups, rows, d), F32),
        compiler_params=_params("arbitrary", "arbitrary"),
        name="mixer_outproj",
    )(*args)


def _pair_queries(q2):
    lane = lax.broadcasted_iota(jnp.int32, q2.shape, 1)
    zero = jnp.zeros_like(q2)
    qs = q2 * jnp.asarray(NA_HEAD_DIM ** -0.5, q2.dtype)
    return jnp.concatenate([jnp.where(lane < NA_HEAD_DIM, qs, zero),
                            jnp.where(lane < NA_HEAD_DIM, zero, qs)], axis=0)


def _unpair(o2):
    n = o2.shape[0] // 2
    lane = lax.broadcasted_iota(jnp.int32, (n, PAIR), 1)
    return jnp.where(lane < NA_HEAD_DIM, o2[:n], o2[n:])


def _scores(q, k):
    return lax.dot_general(q, k, (((1,), (1,)), ((), ())), preferred_element_type=F32)


def _na_kernel(q_ref, k_ref, v_ref, kc_ref, vc_ref, *rest, n_rows):
    bias_refs = rest[:NA_ROWS_PER_STEP]
    o_ref = rest[NA_ROWS_PER_STEP]
    scr = rest[NA_ROWS_PER_STEP + 1:]
    n_loc = WIN_ROWS * GRID_W
    n_pairs = NA_HEADS // 2
    starts = []
    for i in range(NA_ROWS_PER_STEP):
        r = pl.program_id(1) * NA_ROWS_PER_STEP + i
        r0 = jnp.clip(r - WIN_ROWS // 2, 0, n_rows - WIN_ROWS)
        start = pl.multiple_of(r0 * GRID_W, GRID_W)
        starts.append(start)
        s_scr = scr[2 * i]
        for p in range(n_pairs):
            sl = slice(PAIR * p, PAIR * (p + 1))
            q = _pair_queries(q_ref[i * GRID_W:(i + 1) * GRID_W, sl])
            s_scr[p, :, :n_loc] = _scores(q, k_ref[pl.ds(start, n_loc), sl]) + bias_refs[i][p]
            s_scr[p, :, n_loc:] = _scores(q, kc_ref[:, sl])
    for i in range(NA_ROWS_PER_STEP):
        s_scr, p_scr = scr[2 * i], scr[2 * i + 1]
        start = starts[i]
        s = s_scr[...]
        e = jnp.exp(s - jnp.max(s, axis=-1, keepdims=True))
        p_scr[...] = e.astype(BF16)
        inv = 1.0 / jnp.sum(e, axis=-1, keepdims=True)
        for p in range(n_pairs):
            sl = slice(PAIR * p, PAIR * (p + 1))
            o2 = (jnp.dot(p_scr[p, :, :n_loc], v_ref[pl.ds(start, n_loc), sl], preferred_element_type=F32)
                  + jnp.dot(p_scr[p, :, n_loc:], vc_ref[:, sl], preferred_element_type=F32))
            o_ref[i * GRID_W:(i + 1) * GRID_W, sl] = _unpair(o2 * inv[p]).astype(o_ref.dtype)


def _na_attention(qkv, bias, idx, n_lat, ctx_len):
    _, rows, _ = qkv.shape
    d = D_MODEL
    n_rows = rows // GRID_W
    half = WIN_ROWS // 2
    nr = NA_ROWS_PER_STEP
    assert n_rows % nr == 0

    def variant(r):
        return r - jnp.clip(r - half, 0, n_rows - WIN_ROWS)

    n_keys = WIN_ROWS * GRID_W + ctx_len
    scratch = []
    for _ in range(nr):
        scratch += [pltpu.VMEM((NA_HEADS // 2, PAIR, n_keys), F32), pltpu.VMEM((NA_HEADS // 2, PAIR, n_keys), BF16)]
    return pl.pallas_call(
        functools.partial(_na_kernel, n_rows=n_rows),
        grid=(n_lat, n_rows // nr),
        in_specs=[
            pl.BlockSpec((None, nr * GRID_W, d), lambda b, t: (b, t, 0)),
            pl.BlockSpec((None, rows, d), lambda b, t: (b, 0, 1)),
            pl.BlockSpec((None, rows, d), lambda b, t: (b, 0, 2)),
            pl.BlockSpec((None, ctx_len, d), lambda b, t: (n_lat, b, 1)),
            pl.BlockSpec((None, ctx_len, d), lambda b, t: (n_lat, b, 2)),
        ] + [pl.BlockSpec((None, None) + bias.shape[2:], lambda b, t, i=i: (idx, variant(t * nr + i), 0, 0, 0))
             for i in range(nr)],
        out_specs=pl.BlockSpec((None, nr * GRID_W, d), lambda b, t: (b, t, 0)),
        out_shape=jax.ShapeDtypeStruct((n_lat, rows, d), BF16),
        scratch_shapes=scratch,
        compiler_params=_params("arbitrary", "arbitrary"),
        name="na_attention",
    )(qkv, qkv, qkv, qkv, qkv, *([bias] * nr))


def _ctx_attn_kernel(q_ref, k_ref, v_ref, o_ref):
    for p in range(NA_HEADS // 2):
        sl = slice(PAIR * p, PAIR * (p + 1))
        s = _scores(_pair_queries(q_ref[:, sl]), k_ref[:, sl])
        e = jnp.exp(s - jnp.max(s, axis=-1, keepdims=True))
        pr = (e * (1.0 / jnp.sum(e, axis=-1, keepdims=True))).astype(BF16)
        o_ref[:, sl] = _unpair(jnp.dot(pr, v_ref[:, sl], preferred_element_type=F32)).astype(o_ref.dtype)


def _ctx_attention(qkv, n_lat, ctx_len):
    _, rows, _ = qkv.shape
    d = D_MODEL
    return pl.pallas_call(
        _ctx_attn_kernel,
        grid=(n_lat,),
        in_specs=[pl.BlockSpec((None, ctx_len, d), lambda b, c=c: (n_lat, b, c)) for c in range(3)],
        out_specs=pl.BlockSpec((None, ctx_len, d), lambda b: (0, b, 0)),
        out_shape=jax.ShapeDtypeStruct((1, rows, d), BF16),
        compiler_params=_params("arbitrary"),
        name="ctx_attention",
    )(qkv, qkv, qkv)


def _na_bias_table(rpb):
    col = jnp.arange(GRID_W)
    col_start = jnp.clip(col - WIN_COLS // 2, 0, GRID_W - WIN_COLS)
    kcol = col[None, :]
    col_in = (kcol >= col_start[:, None]) & (kcol < col_start[:, None] + WIN_COLS)
    dcol = jnp.clip(kcol - col[:, None], 1 - WIN_COLS, WIN_COLS - 1) + WIN_COLS - 1
    drow = jnp.arange(WIN_ROWS)[None, :] - jnp.arange(WIN_ROWS)[:, None] + WIN_ROWS - 1
    hot_c = (dcol[None] == jnp.arange(2 * WIN_COLS - 1)[:, None, None]).astype(F32)
    hot_d = (drow[:, :, None] == jnp.arange(2 * WIN_ROWS - 1)).astype(F32)
    tbl = jnp.einsum('nhdc,vjd,cqk->nvhqjk', rpb.astype(F32), hot_d, hot_c, precision=lax.Precision.HIGHEST)
    tbl = jnp.where(col_in[:, None, :], tbl, NEG_INF)
    return tbl.reshape(rpb.shape[0], WIN_ROWS, NA_HEADS // 2, 2 * GRID_W, WIN_ROWS * GRID_W)


def _lru_tile_starts():
    starts = []
    for j in range(D_RNN // LRU_TILE):
        lo = (LRU_TILE * j // LRU_BLOCK_W) * LRU_BLOCK_W
        hi = ((LRU_TILE * (j + 1) - 1) // LRU_BLOCK_W + 1) * LRU_BLOCK_W
        ks = min(lo // LANES * LANES, D_RNN - LRU_KW)
        assert ks <= lo and hi <= ks + LRU_KW
        starts.append(ks)
    return starts


def _shift_rows(x, s, row):
    n = x.shape[0]
    if s > 0:
        return jnp.where(row >= s, pltpu.roll(x, s, 0), 0.0)
    return jnp.where(row < n + s, pltpu.roll(x, n + s, 0), 0.0)


def _dwconv(u, w_ref, row, left):
    acc = None
    for i in range(w_ref.shape[0]):
        s = left - i
        term = w_ref[i:i + 1, :] * (u if s == 0 else _shift_rows(u, s, row))
        acc = term if acc is None else acc + term
    return acc


def _lru_kernel(kb_ref, u0_ref, u1_ref, u2_ref, ut_ref, gate_ref, w0_ref, w1_ref, w2_ref, bg_ref,
                cw0_ref, cw1_ref, cw2_ref, cwt_ref, cb0_ref, cb1_ref, cb2_ref, cbt_ref, lam_ref, h0_ref,
                y_ref, hfin_ref, a_scr, b_scr, pa_scr, pb_scr, cin_scr, y_scr, l1a_scr, l1b_scr):
    del kb_ref
    t_len = ut_ref.shape[0]
    n_tiles = t_len // SUBLANES
    left = LRU_CONV_W // 2
    row = lax.broadcasted_iota(jnp.int32, (t_len, 1), 0)

    uc = jnp.concatenate(
        [_dwconv(u_ref[...], cw_ref, row, left) + cb_ref[...]
         for u_ref, cw_ref, cb_ref in ((u0_ref, cw0_ref, cb0_ref), (u1_ref, cw1_ref, cb1_ref),
                                       (u2_ref, cw2_ref, cb2_ref))], axis=1)
    wt = jnp.concatenate([jnp.concatenate([w_ref[q] for q in range(4)], axis=1)
                          for w_ref in (w0_ref, w1_ref, w2_ref)], axis=0)
    bg = jnp.concatenate([bg_ref[q:q + 1, :] for q in range(4)], axis=1)
    gates = jnp.dot(uc.astype(BF16), wt, preferred_element_type=F32) + bg
    ut = _dwconv(ut_ref[...], cwt_ref, row, left) + cbt_ref[...]

    for d in range(2):
        r = jax.nn.sigmoid(gates[:, (2 * d) * LRU_TILE:(2 * d + 1) * LRU_TILE])
        i = jax.nn.sigmoid(gates[:, (2 * d + 1) * LRU_TILE:(2 * d + 2) * LRU_TILE])
        neg_lam = -lam_ref[d:d + 1, :]
        softplus = jnp.maximum(neg_lam, 0.0) + jnp.log1p(jnp.exp(-jnp.abs(neg_lam)))
        a = jnp.exp(-LRU_C * r * softplus)
        a_scr[d] = a
        v = 1.0 - a * a
        b_scr[d] = (v * lax.rsqrt(jnp.maximum(v, SQRT_FLOOR))) * (i * ut)

    rowk = lax.broadcasted_iota(jnp.int32, (n_tiles, 1), 0)
    for d, rev in ((0, False), (1, True)):
        acc_a = acc_b = None
        for n in range(SUBLANES):
            i = SUBLANES - 1 - n if rev else n
            ai = a_scr[d, pl.ds(i, n_tiles, stride=SUBLANES), :]
            bi = b_scr[d, pl.ds(i, n_tiles, stride=SUBLANES), :]
            if n == 0:
                acc_a, acc_b = ai, bi
            else:
                acc_b = ai * acc_b + bi
                acc_a = ai * acc_a
            pa_scr[d, i] = acc_a
            pb_scr[d, i] = acc_b
        h0 = h0_ref[d:d + 1, :]
        if n_tiles >= SUBLANES * SUBLANES:
            n2 = n_tiles // SUBLANES
            l1a_scr[...] = acc_a
            l1b_scr[...] = acc_b
            row2 = lax.broadcasted_iota(jnp.int32, (n2, 1), 0)
            p2a, p2b = [None] * SUBLANES, [None] * SUBLANES
            ca = cb = None
            for n in range(SUBLANES):
                i = SUBLANES - 1 - n if rev else n
                ai = l1a_scr[pl.ds(i, n2, stride=SUBLANES), :]
                bi = l1b_scr[pl.ds(i, n2, stride=SUBLANES), :]
                if n == 0:
                    ca, cb = ai, bi
                else:
                    cb = ai * cb + bi
                    ca = ai * ca
                p2a[i], p2b[i] = ca, cb
            s = 1
            while s < n2:
                if rev:
                    valid = row2 < n2 - s
                    a_sh, b_sh = pltpu.roll(ca, n2 - s, 0), pltpu.roll(cb, n2 - s, 0)
                else:
                    valid = row2 >= s
                    a_sh, b_sh = pltpu.roll(ca, s, 0), pltpu.roll(cb, s, 0)
                cb = jnp.where(valid, ca * b_sh + cb, cb)
                ca = jnp.where(valid, ca * a_sh, ca)
                s *= 2
            st2 = ca * h0 + cb
            if rev:
                c2 = jnp.where(row2 < n2 - 1, pltpu.roll(st2, n2 - 1, 0), h0)
                hfin_ref[d:d + 1, :] = st2[0:1, :]
            else:
                c2 = jnp.where(row2 >= 1, pltpu.roll(st2, 1, 0), h0)
                hfin_ref[d:d + 1, :] = st2[n2 - 1:n2, :]
            prev = c2
            for n in range(SUBLANES):
                i = SUBLANES - 1 - n if rev else n
                cin_scr[d, pl.ds(i, n2, stride=SUBLANES), :] = prev
                prev = p2a[i] * c2 + p2b[i]
        else:
            s = 1
            while s < n_tiles:
                if rev:
                    valid = rowk < n_tiles - s
                    a_sh, b_sh = pltpu.roll(acc_a, n_tiles - s, 0), pltpu.roll(acc_b, n_tiles - s, 0)
                else:
                    valid = rowk >= s
                    a_sh, b_sh = pltpu.roll(acc_a, s, 0), pltpu.roll(acc_b, s, 0)
                acc_b = jnp.where(valid, acc_a * b_sh + acc_b, acc_b)
                acc_a = jnp.where(valid, acc_a * a_sh, acc_a)
                s *= 2
            state = acc_a * h0 + acc_b
            if rev:
                cin_scr[d] = jnp.where(rowk < n_tiles - 1, pltpu.roll(state, n_tiles - 1, 0), h0)
                hfin_ref[d:d + 1, :] = state[0:1, :]
            else:
                cin_scr[d] = jnp.where(rowk >= 1, pltpu.roll(state, 1, 0), h0)
                hfin_ref[d:d + 1, :] = state[n_tiles - 1:n_tiles, :]

    for i in range(SUBLANES):
        h_sum = (pa_scr[0, i] * cin_scr[0] + pb_scr[0, i]) + (pa_scr[1, i] * cin_scr[1] + pb_scr[1, i])
        gi = gate_ref[pl.ds(i, n_tiles, stride=SUBLANES), :]
        y_scr[pl.ds(i, n_tiles, stride=SUBLANES), :] = jax.nn.gelu(gi) * h_sum
    y_ref[...] = y_scr[...].astype(y_ref.dtype)


def _lru_scan(proj, params, h0, n_seq, seq_len, group0, per_group):
    kb, dense, bg, conv_w, conv_b, lam = params
    n_tiles = D_RNN // LRU_TILE
    col0 = D_RNN // LRU_TILE
    rows = proj.shape[1]

    def seq_block(s):
        return group0 + s // per_group, s % per_group

    def u_spec(off):
        return pl.BlockSpec((None, seq_len, LRU_TILE),
                            lambda s, j, kb_ref: (*seq_block(s), col0 + kb_ref[j] + off))

    def win_spec(n_rows, off):
        return pl.BlockSpec((n_rows, LRU_TILE), lambda s, j, kb_ref: (0, kb_ref[j] + off))

    def tile_spec(n_rows):
        return pl.BlockSpec((n_rows, LRU_TILE), lambda s, j, kb_ref: (0, j))

    def w_spec(off):
        return pl.BlockSpec((4, LRU_TILE, LRU_TILE), lambda s, j, kb_ref: (0, kb_ref[j] + off, j))

    grid_spec = pltpu.PrefetchScalarGridSpec(
        num_scalar_prefetch=1,
        grid=(n_seq, n_tiles),
        in_specs=[
            u_spec(0), u_spec(1), u_spec(2),
            pl.BlockSpec((None, seq_len, LRU_TILE), lambda s, j, kb_ref: (*seq_block(s), col0 + j)),
            pl.BlockSpec((None, seq_len, LRU_TILE), lambda s, j, kb_ref: (*seq_block(s), j)),
            w_spec(0), w_spec(1), w_spec(2),
            tile_spec(4),
            win_spec(LRU_CONV_W, 0), win_spec(LRU_CONV_W, 1), win_spec(LRU_CONV_W, 2), tile_spec(LRU_CONV_W),
            win_spec(1, 0), win_spec(1, 1), win_spec(1, 2), tile_spec(1),
            tile_spec(2),
            pl.BlockSpec((None, 2, LRU_TILE), lambda s, j, kb_ref: (s, 0, j)),
        ],
        out_specs=[
            pl.BlockSpec((None, seq_len, LRU_TILE), lambda s, j, kb_ref: (s // per_group, s % per_group, j)),
            pl.BlockSpec((None, 2, LRU_TILE), lambda s, j, kb_ref: (s, 0, j)),
        ],
        scratch_shapes=[
            pltpu.VMEM((2, seq_len, LRU_TILE), F32),
            pltpu.VMEM((2, seq_len, LRU_TILE), F32),
            pltpu.VMEM((2, SUBLANES, seq_len // SUBLANES, LRU_TILE), F32),
            pltpu.VMEM((2, SUBLANES, seq_len // SUBLANES, LRU_TILE), F32),
            pltpu.VMEM((2, seq_len // SUBLANES, LRU_TILE), F32),
            pltpu.VMEM((seq_len, LRU_TILE), F32),
            pltpu.VMEM((seq_len // SUBLANES, LRU_TILE), F32),
            pltpu.VMEM((seq_len // SUBLANES, LRU_TILE), F32),
        ],
    )
    return pl.pallas_call(
        _lru_kernel,
        grid_spec=grid_spec,
        out_shape=[jax.ShapeDtypeStruct((n_seq // per_group, rows, D_RNN), BF16),
                   jax.ShapeDtypeStruct((n_seq, 2, D_RNN), F32)],
        compiler_params=_params("arbitrary", "arbitrary"),
        name="rglru_scan",
    )(kb, proj, proj, proj, proj, proj, dense, dense, dense, bg, conv_w, conv_w, conv_w, conv_w,
      conv_b, conv_b, conv_b, conv_b, lam, h0)


def _lru_params(conv_w, conv_b, w_gates, b_gates, lam):
    bw, nb = LRU_BLOCK_W, LRU_BLOCKS
    w = jnp.transpose(w_gates.reshape(4, nb, bw, bw), (0, 2, 1, 3))
    w = jnp.pad(w, ((0, 0), (0, 0), (0, 0), (0, D_RNN))).reshape(4, bw, nb * (bw + D_RNN))
    w = w[:, :, :nb * D_RNN].reshape(4, bw, nb, D_RNN)
    dense = jnp.transpose(w, (0, 2, 1, 3)).reshape(4, D_RNN, D_RNN).astype(BF16)
    kb = jnp.asarray([ks // LANES for ks in _lru_tile_starts()], jnp.int32)
    return kb, dense, b_gates.reshape(4, D_RNN), conv_w, conv_b.reshape(1, D_RNN), lam


def _sc_kernel(xm_ref, xp_ref, xn_ref, m_ref, w1_ref, w2_ref, cw_ref, g_ref, b_ref, o_ref, h_ref,
               *, n_lat, ctx_len, rows):
    tm = xm_ref.shape[0]
    ext = tm + 2 * HALO
    shift, scale, gate = _mod_rows(m_ref, 1)

    def mod(x):
        return (x * (1 + scale) + shift).astype(BF16)

    h_ref[0:HALO, :] = mod(xp_ref[...])
    h_ref[HALO:HALO + tm, :] = mod(xm_ref[...])
    h_ref[HALO + tm:ext, :] = mod(xn_ref[...])

    pos = pl.program_id(1) * tm + lax.broadcasted_iota(jnp.int32, (tm, 1), 0)
    is_ctx = pl.program_id(0) == n_lat
    seq_pos = jnp.where(is_ctx, pos & (ctx_len - 1), pos)
    seq_last = jnp.where(is_ctx, ctx_len - 1, rows - 1)
    has_prev = seq_pos != 0
    has_next = seq_pos != seq_last

    acc = None
    for c in range(w1_ref.shape[0]):
        t = jnp.dot(h_ref[...], w1_ref[c], preferred_element_type=F32)
        bgate = t[HALO:HALO + tm, :SC_CHUNK]
        w = t[:, SC_CHUNK:2 * SC_CHUNK] * t[:, 2 * SC_CHUNK:]
        w_prev = pltpu.roll(w, 1, 0)[HALO:HALO + tm]
        w_next = pltpu.roll(w, ext - 1, 0)[HALO:HALO + tm]
        cw = cw_ref[c]
        conv = (cw[0:1, :] * jnp.where(has_prev, w_prev, 0.0) + cw[1:2, :] * w[HALO:HALO + tm]
                + cw[2:3, :] * jnp.where(has_next, w_next, 0.0))
        y = jnp.dot((bgate * conv).astype(BF16), w2_ref[c], preferred_element_type=F32)
        acc = y if acc is None else acc + y
    t = ALPHA * xm_ref[...] + gate * acc
    o_ref[...] = _layer_norm(t, g_ref[...], b_ref[...])


def _sc_mixer(x, mods, l, w1, w2, cw, ln_g, ln_b, n_groups, n_lat, ctx_len):
    _, rows, d = x.shape
    tm = ROW_TILE
    halo_per_tile = tm // HALO
    n_halo = rows // HALO
    assert ctx_len & (ctx_len - 1) == 0 and SC_CONV_W == 3
    return pl.pallas_call(
        functools.partial(_sc_kernel, n_lat=n_lat, ctx_len=ctx_len, rows=rows),
        grid=(n_groups, rows // tm),
        in_specs=[
            pl.BlockSpec((None, tm, d), lambda g, i: (g, i, 0)),
            pl.BlockSpec((None, HALO, d), lambda g, i: (g, jnp.maximum(i * halo_per_tile - 1, 0), 0)),
            pl.BlockSpec((None, HALO, d), lambda g, i: (g, jnp.minimum((i + 1) * halo_per_tile, n_halo - 1), 0)),
            _mod_spec(mods, l),
            _resident(w1.shape),
            _resident(w2.shape),
            _resident(cw.shape),
            _ln_spec(ln_g, l, 1),
            _ln_spec(ln_b, l, 1),
        ],
        out_specs=pl.BlockSpec((None, tm, d), lambda g, i: (g, i, 0)),
        out_shape=jax.ShapeDtypeStruct((n_groups, rows, d), F32),
        scratch_shapes=[pltpu.VMEM((tm + 2 * HALO, d), BF16)],
        compiler_params=_params("arbitrary", "arbitrary"),
        name="sc_mixer",
    )(x, x, x, mods, w1, w2, cw, ln_g, ln_b)


def _chunk_cols(w, parts, chunk):
    *lead, k, pn = w.shape
    n = pn // parts
    w = w.reshape(*lead, k, parts, n // chunk, chunk)
    nl = len(lead)
    w = jnp.transpose(w, tuple(range(nl)) + (nl + 2, nl, nl + 1, nl + 3))
    return w.reshape(*lead, n // chunk, k, parts * chunk)


def kernel(x, c, ctx, c_ctx, mod_w, mod_b, ln_g, ln_b, ffn_w_in, ffn_w_out, na_w_qkv, na_w_o, na_rpb,
           lru_w_in, lru_conv_w, lru_conv_b, lru_w_gates, lru_b_gates, lru_lambda, lru_w_out,
           sc_w_in, sc_conv_w, sc_w_out):
    n_lat, rows, d = x.shape
    ctx_len = ctx.shape[1]
    assert d == D_MODEL and n_lat * ctx_len == rows and n_lat + 1 <= MOD_ROWS
    assert rows % ROW_TILE == 0 and rows % GRID_W == 0 and rows // GRID_W >= WIN_ROWS
    n_all = n_lat + 1

    x_ctx = ctx.reshape(1, rows, d)
    cond = jnp.concatenate([c, c_ctx[None], jnp.zeros((MOD_ROWS - n_all, d), F32)], axis=0)
    mods = _modulation(cond, mod_w, mod_b).reshape(DEPTH, MOD_ROWS, N_MOD, d)
    ln_g = ln_g.reshape(DEPTH * 3, 1, d)
    ln_b = ln_b.reshape(DEPTH * 3, 1, d)

    ffn_w1 = _ffn_w1_layout(ffn_w_in.reshape(DEPTH * 2, d, 2 * D_FF))
    ffn_w2 = ffn_w_out.reshape(DEPTH * 2, D_FF // FF_CHUNK, FF_CHUNK, d).astype(BF16)

    na_bias = _na_bias_table(na_rpb)

    xs = x
    for l in range(DEPTH):
        kind = l % N_MIXERS
        idx = l // N_MIXERS
        ctx_out = l < DEPTH - 1
        ctx_in = ctx_out or kind != 2
        n_in = n_all if ctx_in else n_lat
        n_out = n_all if ctx_out else n_lat

        first_ctx = x_ctx if (l == 0 and ctx_in) else None
        xs = _ffn_half(xs, first_ctx, mods, l, 0, ffn_w1, ffn_w2, ln_g, ln_b, n_in)

        if kind == 0:
            qkv = _mixer_proj(xs, mods, l, na_w_qkv[idx].astype(BF16), BF16)
            a_lat = _na_attention(qkv, na_bias, idx, n_lat, ctx_len)
            a_ctx = _ctx_attention(qkv, n_lat, ctx_len) if ctx_out else None
            xs = _mixer_outproj(a_lat, a_ctx, xs, mods, l, na_w_o[idx].astype(BF16), ln_g, ln_b)
        elif kind == 1:
            proj = _mixer_proj(xs, mods, l, lru_w_in[idx].astype(BF16), F32)
            params = _lru_params(lru_conv_w[idx], lru_conv_b[idx], lru_w_gates[idx], lru_b_gates[idx],
                                 lru_lambda[idx])
            zeros = jnp.zeros((n_lat, 2, D_RNN), F32)
            a_ctx, h_ctx = _lru_scan(proj, params, zeros, n_lat, ctx_len, n_lat, n_lat)
            a_lat, _ = _lru_scan(proj, params, h_ctx, n_lat, rows, 0, 1)
            xs = _mixer_outproj(a_lat, a_ctx if ctx_out else None, xs, mods, l, lru_w_out[idx].astype(BF16),
                                ln_g, ln_b)
        else:
            w1 = _chunk_cols(sc_w_in[idx], 3, SC_CHUNK).astype(BF16)
            w2 = sc_w_out[idx].reshape(d // SC_CHUNK, SC_CHUNK, d).astype(BF16)
            cw = jnp.transpose(sc_conv_w[idx].reshape(SC_CONV_W, d // SC_CHUNK, SC_CHUNK), (1, 0, 2))
            xs = _sc_mixer(xs, mods, l, w1, w2, cw, ln_g, ln_b, n_out, n_lat, ctx_len)

        xs = _ffn_half(xs, None, mods, l, 2, ffn_w1, ffn_w2, ln_g, ln_b, n_out)
    return xs
```

```python
import functools

import jax
import jax.numpy as jnp
from jax import lax
from jax.experimental import pallas as pl
from jax.experimental.pallas import tpu as pltpu

F32 = jnp.float32
BF16 = jnp.bfloat16

D_MODEL = 1024
DEPTH = 4
GRID_W = 64
N_MIXERS = 3
NA_HEADS = 16
NA_HEAD_DIM = D_MODEL // NA_HEADS
WIN_ROWS = 8
WIN_COLS = 16
D_RNN = 1408
LRU_BLOCKS = 16
LRU_BLOCK_W = D_RNN // LRU_BLOCKS
LRU_CONV_W = 4
LRU_C = 8.0
SQRT_FLOOR = 1e-30
SC_CONV_W = 3
D_FF = 2816
N_MOD = 9
ALPHA = (2 * DEPTH) ** 0.25
LN_EPS = 1e-5
NEG_INF = -1e30

LANES = 128
SUBLANES = 8
VMEM_LIMIT_BYTES = 56 * 1024 * 1024

ROW_TILE = 512
FF_CHUNK = 256
SC_CHUNK = 512
W1_LAYOUT_ROWS = 256
MOD_COLS = 2304
MOD_ROWS = 16
LRU_TILE = LANES
LRU_KW = 3 * LANES
HALO = SUBLANES
PAIR = 2 * NA_HEAD_DIM
NA_ROWS_PER_STEP = 4


def _params(*sem):
    return pltpu.CompilerParams(dimension_semantics=sem, vmem_limit_bytes=VMEM_LIMIT_BYTES)


def _resident(shape):
    zeros = (0,) * len(shape)
    return pl.BlockSpec(shape, lambda *_: zeros, pipeline_mode=pl.Buffered(1))


def _layer_norm(t, g, b):
    mu = jnp.mean(t, axis=-1, keepdims=True)
    d = t - mu
    var = jnp.mean(d * d, axis=-1, keepdims=True)
    return d * lax.rsqrt(var + LN_EPS) * g + b


def _mod_spec(mods, l):
    return pl.BlockSpec((None, None) + mods.shape[2:], lambda g, i: (l, g, 0, 0))


def _ln_spec(ln, l, j):
    return pl.BlockSpec((None,) + ln.shape[1:], lambda g, i: (3 * l + j, 0, 0))


def _layer_weight_spec(w, idx):
    zeros = (0,) * (w.ndim - 1)
    return pl.BlockSpec((None,) + w.shape[1:], lambda *_: (idx,) + zeros, pipeline_mode=pl.Buffered(1))


def _mod_rows(m_ref, j):
    return m_ref[3 * j:3 * j + 1, :], m_ref[3 * j + 1:3 * j + 2, :], m_ref[3 * j + 2:3 * j + 3, :]


def _mod_kernel(c_ref, w_ref, b_ref, o_ref):
    c = c_ref[...]
    s = (c * jax.nn.sigmoid(c)).astype(BF16)
    o_ref[...] = jnp.dot(s, w_ref[...].astype(BF16), preferred_element_type=F32) + b_ref[...]


def _modulation(cond, mod_w, mod_b):
    depth, d, n = mod_w.shape
    return pl.pallas_call(
        _mod_kernel,
        grid=(depth, n // MOD_COLS),
        in_specs=[
            pl.BlockSpec((MOD_ROWS, d), lambda l, j: (0, 0)),
            pl.BlockSpec((None, d, MOD_COLS), lambda l, j: (l, 0, j)),
            pl.BlockSpec((None, 1, MOD_COLS), lambda l, j: (l, 0, j)),
        ],
        out_specs=pl.BlockSpec((None, MOD_ROWS, MOD_COLS), lambda l, j: (l, 0, j)),
        out_shape=jax.ShapeDtypeStruct((depth, MOD_ROWS, n), F32),
        compiler_params=_params("arbitrary", "arbitrary"),
        name="modulation",
    )(cond, mod_w, mod_b.reshape(depth, 1, n))


def _ffn_kernel(*refs, j, n_lat, has_ctx):
    if has_ctx:
        xl_ref, xc_ref, m_ref, w1_ref, w2_ref, g_ref, b_ref, o_ref, h_ref = refs
        x = jnp.where(pl.program_id(0) == n_lat, xc_ref[...], xl_ref[...])
    else:
        xl_ref, m_ref, w1_ref, w2_ref, g_ref, b_ref, o_ref, h_ref = refs
        x = xl_ref[...]
    shift, scale, gate = _mod_rows(m_ref, j)
    h_ref[...] = (x * (1 + scale) + shift).astype(BF16)
    acc = None
    for c in range(w1_ref.shape[0]):
        gu = jnp.dot(h_ref[...], w1_ref[c], preferred_element_type=F32)
        g = gu[:, :FF_CHUNK]
        u = gu[:, FF_CHUNK:]
        a = (g * jax.nn.sigmoid(g) * u).astype(BF16)
        y = jnp.dot(a, w2_ref[c], preferred_element_type=F32)
        acc = y if acc is None else acc + y
    t = ALPHA * x + (0.5 * gate) * acc
    o_ref[...] = _layer_norm(t, g_ref[...], b_ref[...])


def _ffn_half(x, x_ctx, mods, l, j, w1, w2, ln_g, ln_b, n_groups):
    n_lat, rows, d = x.shape
    has_ctx = x_ctx is not None
    if has_ctx:
        assert n_groups == n_lat + 1
        in_specs = [pl.BlockSpec((None, ROW_TILE, d), lambda g, i: (jnp.minimum(g, n_lat - 1), i, 0)),
                    pl.BlockSpec((None, ROW_TILE, d), lambda g, i: (0, i, 0))]
        args = [x, x_ctx]
    else:
        in_specs = [pl.BlockSpec((None, ROW_TILE, d), lambda g, i: (g, i, 0))]
        args = [x]
    slab = 2 * l + j // 2
    in_specs += [_mod_spec(mods, l), _layer_weight_spec(w1, slab), _layer_weight_spec(w2, slab),
                 _ln_spec(ln_g, l, j), _ln_spec(ln_b, l, j)]
    return pl.pallas_call(
        functools.partial(_ffn_kernel, j=j, n_lat=n_lat, has_ctx=has_ctx),
        grid=(n_groups, rows // ROW_TILE),
        in_specs=in_specs,
        out_specs=pl.BlockSpec((None, ROW_TILE, d), lambda g, i: (g, i, 0)),
        out_shape=jax.ShapeDtypeStruct((n_groups, rows, d), F32),
        scratch_shapes=[pltpu.VMEM((ROW_TILE, d), BF16)],
        compiler_params=_params("arbitrary", "arbitrary"),
        name="ffn_half",
    )(*args, mods, w1, w2, ln_g, ln_b)


def _w1_layout_kernel(g_ref, u_ref, o_ref):
    for c in range(o_ref.shape[0]):
        o_ref[c, :, :FF_CHUNK] = g_ref[:, FF_CHUNK * c:FF_CHUNK * (c + 1)].astype(BF16)
        o_ref[c, :, FF_CHUNK:] = u_ref[:, FF_CHUNK * c:FF_CHUNK * (c + 1)].astype(BF16)


def _ffn_w1_layout(w_in):
    n, d, _ = w_in.shape
    nc = D_FF // FF_CHUNK
    rt = W1_LAYOUT_ROWS
    return pl.pallas_call(
        _w1_layout_kernel,
        grid=(n, d // rt),
        in_specs=[pl.BlockSpec((None, rt, D_FF), lambda l, r: (l, r, 0)),
                  pl.BlockSpec((None, rt, D_FF), lambda l, r: (l, r, 1))],
        out_specs=pl.BlockSpec((None, nc, rt, 2 * FF_CHUNK), lambda l, r: (l, 0, r, 0)),
        out_shape=jax.ShapeDtypeStruct((n, nc, d, 2 * FF_CHUNK), BF16),
        compiler_params=_params("arbitrary", "arbitrary"),
        name="ffn_w1_layout",
    )(w_in, w_in)


def _proj_kernel(x_ref, m_ref, w_ref, o_ref):
    shift, scale, _ = _mod_rows(m_ref, 1)
    h = (x_ref[...] * (1 + scale) + shift).astype(BF16)
    o_ref[...] = jnp.dot(h, w_ref[...], preferred_element_type=F32).astype(o_ref.dtype)


def _mixer_proj(x, mods, l, w, out_dtype):
    n_groups, rows, d = x.shape
    n = w.shape[1]
    return pl.pallas_call(
        _proj_kernel,
        grid=(n_groups, rows // ROW_TILE),
        in_specs=[
            pl.BlockSpec((None, ROW_TILE, d), lambda g, i: (g, i, 0)),
            _mod_spec(mods, l),
            _resident(w.shape),
        ],
        out_specs=pl.BlockSpec((None, ROW_TILE, n), lambda g, i: (g, i, 0)),
        out_shape=jax.ShapeDtypeStruct((n_groups, rows, n), out_dtype),
        compiler_params=_params("arbitrary", "arbitrary"),
        name="mixer_proj",
    )(x, mods, w)


def _outproj_kernel(*refs, n_lat, has_ctx):
    if has_ctx:
        al_ref, ac_ref, x_ref, m_ref, w_ref, g_ref, b_ref, o_ref = refs
        a = jnp.where(pl.program_id(0) == n_lat, ac_ref[...], al_ref[...])
    else:
        al_ref, x_ref, m_ref, w_ref, g_ref, b_ref, o_ref = refs
        a = al_ref[...]
    _, _, gate = _mod_rows(m_ref, 1)
    y = jnp.dot(a, w_ref[...], preferred_element_type=F32)
    t = ALPHA * x_ref[...] + gate * y
    o_ref[...] = _layer_norm(t, g_ref[...], b_ref[...])


def _mixer_outproj(a_lat, a_ctx, x, mods, l, w, ln_g, ln_b):
    n_lat, rows, k = a_lat.shape
    d = x.shape[-1]
    has_ctx = a_ctx is not None
    n_groups = n_lat + 1 if has_ctx else n_lat
    in_specs = [pl.BlockSpec((None, ROW_TILE, k), lambda g, i: (jnp.minimum(g, n_lat - 1), i, 0))]
    args = [a_lat]
    if has_ctx:
        in_specs.append(pl.BlockSpec((None, ROW_TILE, k), lambda g, i: (0, i, 0)))
        args.append(a_ctx)
    in_specs += [
        pl.BlockSpec((None, ROW_TILE, d), lambda g, i: (g, i, 0)),
        _mod_spec(mods, l),
        _resident(w.shape),
        _ln_spec(ln_g, l, 1),
        _ln_spec(ln_b, l, 1),
    ]
    args += [x, mods, w, ln_g, ln_b]
    return pl.pallas_call(
        functools.partial(_outproj_kernel, n_lat=n_lat, has_ctx=has_ctx),
        grid=(n_groups, rows // ROW_TILE),
        in_specs=in_specs,
        out_specs=pl.BlockSpec((None, ROW_TILE, d), lambda g, i: (g, i, 0)),
        out_shape=jax.ShapeDtypeStruct((n_groups, rows, d), F32),
        compiler_params=_params("arbitrary", "arbitrary"),
        name="mixer_outproj",
    )(*args)


def _pair_queries(q2):
    lane = lax.broadcasted_iota(jnp.int32, q2.shape, 1)
    zero = jnp.zeros_like(q2)
    qs = q2 * jnp.asarray(NA_HEAD_DIM ** -0.5, q2.dtype)
    return jnp.concatenate([jnp.where(lane < NA_HEAD_DIM, qs, zero),
                            jnp.where(lane < NA_HEAD_DIM, zero, qs)], axis=0)


def _unpair(o2):
    n = o2.shape[0] // 2
    lane = lax.broadcasted_iota(jnp.int32, (n, PAIR), 1)
    return jnp.where(lane < NA_HEAD_DIM, o2[:n], o2[n:])


def _scores(q, k):
    return lax.dot_general(q, k, (((1,), (1,)), ((), ())), preferred_element_type=F32)


def _na_kernel(q_ref, k_ref, v_ref, kc_ref, vc_ref, *rest, n_rows):
    bias_refs = rest[:NA_ROWS_PER_STEP]
    o_ref = rest[NA_ROWS_PER_STEP]
    scr = rest[NA_ROWS_PER_STEP + 1:]
    n_loc = WIN_ROWS * GRID_W
    n_pairs = NA_HEADS // 2
    starts = []
    for i in range(NA_ROWS_PER_STEP):
        r = pl.program_id(1) * NA_ROWS_PER_STEP + i
        r0 = jnp.clip(r - WIN_ROWS // 2, 0, n_rows - WIN_ROWS)
        start = pl.multiple_of(r0 * GRID_W, GRID_W)
        starts.append(start)
        s_scr = scr[2 * i]
        for p in range(n_pairs):
            sl = slice(PAIR * p, PAIR * (p + 1))
            q = _pair_queries(q_ref[i * GRID_W:(i + 1) * GRID_W, sl])
            s_scr[p, :, :n_loc] = _scores(q, k_ref[pl.ds(start, n_loc), sl]) + bias_refs[i][p]
            s_scr[p, :, n_loc:] = _scores(q, kc_ref[:, sl])
    for i in range(NA_ROWS_PER_STEP):
        s_scr, p_scr = scr[2 * i], scr[2 * i + 1]
        start = starts[i]
        s = s_scr[...]
        e = jnp.exp(s - jnp.max(s, axis=-1, keepdims=True))
        p_scr[...] = e.astype(BF16)
        inv = 1.0 / jnp.sum(e, axis=-1, keepdims=True)
        for p in range(n_pairs):
            sl = slice(PAIR * p, PAIR * (p + 1))
            o2 = (jnp.dot(p_scr[p, :, :n_loc], v_ref[pl.ds(start, n_loc), sl], preferred_element_type=F32)
                  + jnp.dot(p_scr[p, :, n_loc:], vc_ref[:, sl], preferred_element_type=F32))
            o_ref[i * GRID_W:(i + 1) * GRID_W, sl] = _unpair(o2 * inv[p]).astype(o_ref.dtype)


def _na_attention(qkv, bias, idx, n_lat, ctx_len):
    _, rows, _ = qkv.shape
    d = D_MODEL
    n_rows = rows // GRID_W
    half = WIN_ROWS // 2
    nr = NA_ROWS_PER_STEP
    assert n_rows % nr == 0

    def variant(r):
        return r - jnp.clip(r - half, 0, n_rows - WIN_ROWS)

    n_keys = WIN_ROWS * GRID_W + ctx_len
    scratch = []
    for _ in range(nr):
        scratch += [pltpu.VMEM((NA_HEADS // 2, PAIR, n_keys), F32), pltpu.VMEM((NA_HEADS // 2, PAIR, n_keys), BF16)]
    return pl.pallas_call(
        functools.partial(_na_kernel, n_rows=n_rows),
        grid=(n_lat, n_rows // nr),
        in_specs=[
            pl.BlockSpec((None, nr * GRID_W, d), lambda b, t: (b, t, 0)),
            pl.BlockSpec((None, rows, d), lambda b, t: (b, 0, 1)),
            pl.BlockSpec((None, rows, d), lambda b, t: (b, 0, 2)),
            pl.BlockSpec((None, ctx_len, d), lambda b, t: (n_lat, b, 1)),
            pl.BlockSpec((None, ctx_len, d), lambda b, t: (n_lat, b, 2)),
        ] + [pl.BlockSpec((None, None) + bias.shape[2:], lambda b, t, i=i: (idx, variant(t * nr + i), 0, 0, 0))
             for i in range(nr)],
        out_specs=pl.BlockSpec((None, nr * GRID_W, d), lambda b, t: (b, t, 0)),
        out_shape=jax.ShapeDtypeStruct((n_lat, rows, d), BF16),
        scratch_shapes=scratch,
        compiler_params=_params("arbitrary", "arbitrary"),
        name="na_attention",
    )(qkv, qkv, qkv, qkv, qkv, *([bias] * nr))


def _ctx_attn_kernel(q_ref, k_ref, v_ref, o_ref):
    for p in range(NA_HEADS // 2):
        sl = slice(PAIR * p, PAIR * (p + 1))
        s = _scores(_pair_queries(q_ref[:, sl]), k_ref[:, sl])
        e = jnp.exp(s - jnp.max(s, axis=-1, keepdims=True))
        pr = (e * (1.0 / jnp.sum(e, axis=-1, keepdims=True))).astype(BF16)
        o_ref[:, sl] = _unpair(jnp.dot(pr, v_ref[:, sl], preferred_element_type=F32)).astype(o_ref.dtype)


def _ctx_attention(qkv, n_lat, ctx_len):
    _, rows, _ = qkv.shape
    d = D_MODEL
    return pl.pallas_call(
        _ctx_attn_kernel,
        grid=(n_lat,),
        in_specs=[pl.BlockSpec((None, ctx_len, d), lambda b, c=c: (n_lat, b, c)) for c in range(3)],
        out_specs=pl.BlockSpec((None, ctx_len, d), lambda b: (0, b, 0)),
        out_shape=jax.ShapeDtypeStruct((1, rows, d), BF16),
        compiler_params=_params("arbitrary"),
        name="ctx_attention",
    )(qkv, qkv, qkv)


def _na_bias_table(rpb):
    col = jnp.arange(GRID_W)
    col_start = jnp.clip(col - WIN_COLS // 2, 0, GRID_W - WIN_COLS)
    kcol = col[None, :]
    col_in = (kcol >= col_start[:, None]) & (kcol < col_start[:, None] + WIN_COLS)
    dcol = jnp.clip(kcol - col[:, None], 1 - WIN_COLS, WIN_COLS - 1) + WIN_COLS - 1
    drow = jnp.arange(WIN_ROWS)[None, :] - jnp.arange(WIN_ROWS)[:, None] + WIN_ROWS - 1
    hot_c = (dcol[None] == jnp.arange(2 * WIN_COLS - 1)[:, None, None]).astype(F32)
    hot_d = (drow[:, :, None] == jnp.arange(2 * WIN_ROWS - 1)).astype(F32)
    tbl = jnp.einsum('nhdc,vjd,cqk->nvhqjk', rpb.astype(F32), hot_d, hot_c, precision=lax.Precision.HIGHEST)
    tbl = jnp.where(col_in[:, None, :], tbl, NEG_INF)
    return tbl.reshape(rpb.shape[0], WIN_ROWS, NA_HEADS // 2, 2 * GRID_W, WIN_ROWS * GRID_W)


def _lru_tile_starts():
    starts = []
    for j in range(D_RNN // LRU_TILE):
        lo = (LRU_TILE * j // LRU_BLOCK_W) * LRU_BLOCK_W
        hi = ((LRU_TILE * (j + 1) - 1) // LRU_BLOCK_W + 1) * LRU_BLOCK_W
        ks = min(lo // LANES * LANES, D_RNN - LRU_KW)
        assert ks <= lo and hi <= ks + LRU_KW
        starts.append(ks)
    return starts


def _shift_rows(x, s, row):
    n = x.shape[0]
    if s > 0:
        return jnp.where(row >= s, pltpu.roll(x, s, 0), 0.0)
    return jnp.where(row < n + s, pltpu.roll(x, n + s, 0), 0.0)


def _dwconv(u, w_ref, row, left):
    acc = None
    for i in range(w_ref.shape[0]):
        s = left - i
        term = w_ref[i:i + 1, :] * (u if s == 0 else _shift_rows(u, s, row))
        acc = term if acc is None else acc + term
    return acc


def _lru_kernel(kb_ref, u0_ref, u1_ref, u2_ref, ut_ref, gate_ref, w0_ref, w1_ref, w2_ref, bg_ref,
                cw0_ref, cw1_ref, cw2_ref, cwt_ref, cb0_ref, cb1_ref, cb2_ref, cbt_ref, lam_ref, h0_ref,
                y_ref, hfin_ref, a_scr, b_scr, pa_scr, pb_scr, cin_scr, y_scr, l1a_scr, l1b_scr):
    del kb_ref
    t_len = ut_ref.shape[0]
    n_tiles = t_len // SUBLANES
    left = LRU_CONV_W // 2
    row = lax.broadcasted_iota(jnp.int32, (t_len, 1), 0)

    uc = jnp.concatenate(
        [_dwconv(u_ref[...], cw_ref, row, left) + cb_ref[...]
         for u_ref, cw_ref, cb_ref in ((u0_ref, cw0_ref, cb0_ref), (u1_ref, cw1_ref, cb1_ref),
                                       (u2_ref, cw2_ref, cb2_ref))], axis=1)
    wt = jnp.concatenate([jnp.concatenate([w_ref[q] for q in range(4)], axis=1)
                          for w_ref in (w0_ref, w1_ref, w2_ref)], axis=0)
    bg = jnp.concatenate([bg_ref[q:q + 1, :] for q in range(4)], axis=1)
    gates = jnp.dot(uc.astype(BF16), wt, preferred_element_type=F32) + bg
    ut = _dwconv(ut_ref[...], cwt_ref, row, left) + cbt_ref[...]

    for d in range(2):
        r = jax.nn.sigmoid(gates[:, (2 * d) * LRU_TILE:(2 * d + 1) * LRU_TILE])
        i = jax.nn.sigmoid(gates[:, (2 * d + 1) * LRU_TILE:(2 * d + 2) * LRU_TILE])
        neg_lam = -lam_ref[d:d + 1, :]
        softplus = jnp.maximum(neg_lam, 0.0) + jnp.log1p(jnp.exp(-jnp.abs(neg_lam)))
        a = jnp.exp(-LRU_C * r * softplus)
        a_scr[d] = a
        v = 1.0 - a * a
        b_scr[d] = (v * lax.rsqrt(jnp.maximum(v, SQRT_FLOOR))) * (i * ut)

    rowk = lax.broadcasted_iota(jnp.int32, (n_tiles, 1), 0)
    for d, rev in ((0, False), (1, True)):
        acc_a = acc_b = None
        for n in range(SUBLANES):
            i = SUBLANES - 1 - n if rev else n
            ai = a_scr[d, pl.ds(i, n_tiles, stride=SUBLANES), :]
            bi = b_scr[d, pl.ds(i, n_tiles, stride=SUBLANES), :]
            if n == 0:
                acc_a, acc_b = ai, bi
            else:
                acc_b = ai * acc_b + bi
                acc_a = ai * acc_a
            pa_scr[d, i] = acc_a
            pb_scr[d, i] = acc_b
        h0 = h0_ref[d:d + 1, :]
        if n_tiles >= SUBLANES * SUBLANES:
            n2 = n_tiles // SUBLANES
            l1a_scr[...] = acc_a
            l1b_scr[...] = acc_b
            row2 = lax.broadcasted_iota(jnp.int32, (n2, 1), 0)
            p2a, p2b = [None] * SUBLANES, [None] * SUBLANES
            ca = cb = None
            for n in range(SUBLANES):
                i = SUBLANES - 1 - n if rev else n
                ai = l1a_scr[pl.ds(i, n2, stride=SUBLANES), :]
                bi = l1b_scr[pl.ds(i, n2, stride=SUBLANES), :]
                if n == 0:
                    ca, cb = ai, bi
                else:
                    cb = ai * cb + bi
                    ca = ai * ca
                p2a[i], p2b[i] = ca, cb
            s = 1
            while s < n2:
                if rev:
                    valid = row2 < n2 - s
                    a_sh, b_sh = pltpu.roll(ca, n2 - s, 0), pltpu.roll(cb, n2 - s, 0)
                else:
                    valid = row2 >= s
                    a_sh, b_sh = pltpu.roll(ca, s, 0), pltpu.roll(cb, s, 0)
                cb = jnp.where(valid, ca * b_sh + cb, cb)
                ca = jnp.where(valid, ca * a_sh, ca)
                s *= 2
            st2 = ca * h0 + cb
            if rev:
                c2 = jnp.where(row2 < n2 - 1, pltpu.roll(st2, n2 - 1, 0), h0)
                hfin_ref[d:d + 1, :] = st2[0:1, :]
            else:
                c2 = jnp.where(row2 >= 1, pltpu.roll(st2, 1, 0), h0)
                hfin_ref[d:d + 1, :] = st2[n2 - 1:n2, :]
            prev = c2
            for n in range(SUBLANES):
                i = SUBLANES - 1 - n if rev else n
                cin_scr[d, pl.ds(i, n2, stride=SUBLANES), :] = prev
                prev = p2a[i] * c2 + p2b[i]
        else:
            s = 1
            while s < n_tiles:
                if rev:
                    valid = rowk < n_tiles - s
                    a_sh, b_sh = pltpu.roll(acc_a, n_tiles - s, 0), pltpu.roll(acc_b, n_tiles - s, 0)
                else:
                    valid = rowk >= s
                    a_sh, b_sh = pltpu.roll(acc_a, s, 0), pltpu.roll(acc_b, s, 0)
                acc_b = jnp.where(valid, acc_a * b_sh + acc_b, acc_b)
                acc_a = jnp.where(valid, acc_a * a_sh, acc_a)
                s *= 2
            state = acc_a * h0 + acc_b
            if rev:
                cin_scr[d] = jnp.where(rowk < n_tiles - 1, pltpu.roll(state, n_tiles - 1, 0), h0)
                hfin_ref[d:d + 1, :] = state[0:1, :]
            else:
                cin_scr[d] = jnp.where(rowk >= 1, pltpu.roll(state, 1, 0), h0)
                hfin_ref[d:d + 1, :] = state[n_tiles - 1:n_tiles, :]

    for i in range(SUBLANES):
        h_sum = (pa_scr[0, i] * cin_scr[0] + pb_scr[0, i]) + (pa_scr[1, i] * cin_scr[1] + pb_scr[1, i])
        gi = gate_ref[pl.ds(i, n_tiles, stride=SUBLANES), :]
        y_scr[pl.ds(i, n_tiles, stride=SUBLANES), :] = jax.nn.gelu(gi) * h_sum
    y_ref[...] = y_scr[...].astype(y_ref.dtype)


def _lru_scan(proj, params, h0, n_seq, seq_len, group0, per_group):
    kb, dense, bg, conv_w, conv_b, lam = params
    n_tiles = D_RNN // LRU_TILE
    col0 = D_RNN // LRU_TILE
    rows = proj.shape[1]

    def seq_block(s):
        return group0 + s // per_group, s % per_group

    def u_spec(off):
        return pl.BlockSpec((None, seq_len, LRU_TILE),
                            lambda s, j, kb_ref: (*seq_block(s), col0 + kb_ref[j] + off))

    def win_spec(n_rows, off):
        return pl.BlockSpec((n_rows, LRU_TILE), lambda s, j, kb_ref: (0, kb_ref[j] + off))

    def tile_spec(n_rows):
        return pl.BlockSpec((n_rows, LRU_TILE), lambda s, j, kb_ref: (0, j))

    def w_spec(off):
        return pl.BlockSpec((4, LRU_TILE, LRU_TILE), lambda s, j, kb_ref: (0, kb_ref[j] + off, j))

    grid_spec = pltpu.PrefetchScalarGridSpec(
        num_scalar_prefetch=1,
        grid=(n_seq, n_tiles),
        in_specs=[
            u_spec(0), u_spec(1), u_spec(2),
            pl.BlockSpec((None, seq_len, LRU_TILE), lambda s, j, kb_ref: (*seq_block(s), col0 + j)),
            pl.BlockSpec((None, seq_len, LRU_TILE), lambda s, j, kb_ref: (*seq_block(s), j)),
            w_spec(0), w_spec(1), w_spec(2),
            tile_spec(4),
            win_spec(LRU_CONV_W, 0), win_spec(LRU_CONV_W, 1), win_spec(LRU_CONV_W, 2), tile_spec(LRU_CONV_W),
            win_spec(1, 0), win_spec(1, 1), win_spec(1, 2), tile_spec(1),
            tile_spec(2),
            pl.BlockSpec((None, 2, LRU_TILE), lambda s, j, kb_ref: (s, 0, j)),
        ],
        out_specs=[
            pl.BlockSpec((None, seq_len, LRU_TILE), lambda s, j, kb_ref: (s // per_group, s % per_group, j)),
            pl.BlockSpec((None, 2, LRU_TILE), lambda s, j, kb_ref: (s, 0, j)),
        ],
        scratch_shapes=[
            pltpu.VMEM((2, seq_len, LRU_TILE), F32),
            pltpu.VMEM((2, seq_len, LRU_TILE), F32),
            pltpu.VMEM((2, SUBLANES, seq_len // SUBLANES, LRU_TILE), F32),
            pltpu.VMEM((2, SUBLANES, seq_len // SUBLANES, LRU_TILE), F32),
            pltpu.VMEM((2, seq_len // SUBLANES, LRU_TILE), F32),
            pltpu.VMEM((seq_len, LRU_TILE), F32),
            pltpu.VMEM((seq_len // SUBLANES, LRU_TILE), F32),
            pltpu.VMEM((seq_len // SUBLANES, LRU_TILE), F32),
        ],
    )
    return pl.pallas_call(
        _lru_kernel,
        grid_spec=grid_spec,
        out_shape=[jax.ShapeDtypeStruct((n_seq // per_group, rows, D_RNN), BF16),
                   jax.ShapeDtypeStruct((n_seq, 2, D_RNN), F32)],
        compiler_params=_params("arbitrary", "arbitrary"),
        name="rglru_scan",
    )(kb, proj, proj, proj, proj, proj, dense, dense, dense, bg, conv_w, conv_w, conv_w, conv_w,
      conv_b, conv_b, conv_b, conv_b, lam, h0)


def _lru_params(conv_w, conv_b, w_gates, b_gates, lam):
    bw, nb = LRU_BLOCK_W, LRU_BLOCKS
    w = jnp.transpose(w_gates.reshape(4, nb, bw, bw), (0, 2, 1, 3))
    w = jnp.pad(w, ((0, 0), (0, 0), (0, 0), (0, D_RNN))).reshape(4, bw, nb * (bw + D_RNN))
    w = w[:, :, :nb * D_RNN].reshape(4, bw, nb, D_RNN)
    dense = jnp.transpose(w, (0, 2, 1, 3)).reshape(4, D_RNN, D_RNN).astype(BF16)
    kb = jnp.asarray([ks // LANES for ks in _lru_tile_starts()], jnp.int32)
    return kb, dense, b_gates.reshape(4, D_RNN), conv_w, conv_b.reshape(1, D_RNN), lam


def _sc_kernel(xm_ref, xp_ref, xn_ref, m_ref, w1_ref, w2_ref, cw_ref, g_ref, b_ref, o_ref, h_ref,
               *, n_lat, ctx_len, rows):
    tm = xm_ref.shape[0]
    ext = tm + 2 * HALO
    shift, scale, gate = _mod_rows(m_ref, 1)

    def mod(x):
        return (x * (1 + scale) + shift).astype(BF16)

    h_ref[0:HALO, :] = mod(xp_ref[...])
    h_ref[HALO:HALO + tm, :] = mod(xm_ref[...])
    h_ref[HALO + tm:ext, :] = mod(xn_ref[...])

    pos = pl.program_id(1) * tm + lax.broadcasted_iota(jnp.int32, (tm, 1), 0)
    is_ctx = pl.program_id(0) == n_lat
    seq_pos = jnp.where(is_ctx, pos & (ctx_len - 1), pos)
    seq_last = jnp.where(is_ctx, ctx_len - 1, rows - 1)
    has_prev = seq_pos != 0
    has_next = seq_pos != seq_last

    acc = None
    for c in range(w1_ref.shape[0]):
        t = jnp.dot(h_ref[...], w1_ref[c], preferred_element_type=F32)
        bgate = t[HALO:HALO + tm, :SC_CHUNK]
        w = t[:, SC_CHUNK:2 * SC_CHUNK] * t[:, 2 * SC_CHUNK:]
        w_prev = pltpu.roll(w, 1, 0)[HALO:HALO + tm]
        w_next = pltpu.roll(w, ext - 1, 0)[HALO:HALO + tm]
        cw = cw_ref[c]
        conv = (cw[0:1, :] * jnp.where(has_prev, w_prev, 0.0) + cw[1:2, :] * w[HALO:HALO + tm]
                + cw[2:3, :] * jnp.where(has_next, w_next, 0.0))
        y = jnp.dot((bgate * conv).astype(BF16), w2_ref[c], preferred_element_type=F32)
        acc = y if acc is None else acc + y
    t = ALPHA * xm_ref[...] + gate * acc
    o_ref[...] = _layer_norm(t, g_ref[...], b_ref[...])


def _sc_mixer(x, mods, l, w1, w2, cw, ln_g, ln_b, n_groups, n_lat, ctx_len):
    _, rows, d = x.shape
    tm = ROW_TILE
    halo_per_tile = tm // HALO
    n_halo = rows // HALO
    assert ctx_len & (ctx_len - 1) == 0 and SC_CONV_W == 3
    return pl.pallas_call(
        functools.partial(_sc_kernel, n_lat=n_lat, ctx_len=ctx_len, rows=rows),
        grid=(n_groups, rows // tm),
        in_specs=[
            pl.BlockSpec((None, tm, d), lambda g, i: (g, i, 0)),
            pl.BlockSpec((None, HALO, d), lambda g, i: (g, jnp.maximum(i * halo_per_tile - 1, 0), 0)),
            pl.BlockSpec((None, HALO, d), lambda g, i: (g, jnp.minimum((i + 1) * halo_per_tile, n_halo - 1), 0)),
            _mod_spec(mods, l),
            _resident(w1.shape),
            _resident(w2.shape),
            _resident(cw.shape),
            _ln_spec(ln_g, l, 1),
            _ln_spec(ln_b, l, 1),
        ],
        out_specs=pl.BlockSpec((None, tm, d), lambda g, i: (g, i, 0)),
        out_shape=jax.ShapeDtypeStruct((n_groups, rows, d), F32),
        scratch_shapes=[pltpu.VMEM((tm + 2 * HALO, d), BF16)],
        compiler_params=_params("arbitrary", "arbitrary"),
        name="sc_mixer",
    )(x, x, x, mods, w1, w2, cw, ln_g, ln_b)


def _chunk_cols(w, parts, chunk):
    *lead, k, pn = w.shape
    n = pn // parts
    w = w.reshape(*lead, k, parts, n // chunk, chunk)
    nl = len(lead)
    w = jnp.transpose(w, tuple(range(nl)) + (nl + 2, nl, nl + 1, nl + 3))
    return w.reshape(*lead, n // chunk, k, parts * chunk)


def kernel(x, c, ctx, c_ctx, mod_w, mod_b, ln_g, ln_b, ffn_w_in, ffn_w_out, na_w_qkv, na_w_o, na_rpb,
           lru_w_in, lru_conv_w, lru_conv_b, lru_w_gates, lru_b_gates, lru_lambda, lru_w_out,
           sc_w_in, sc_conv_w, sc_w_out):
    n_lat, rows, d = x.shape
    ctx_len = ctx.shape[1]
    assert d == D_MODEL and n_lat * ctx_len == rows and n_lat + 1 <= MOD_ROWS
    assert rows % ROW_TILE == 0 and rows % GRID_W == 0 and rows // GRID_W >= WIN_ROWS
    n_all = n_lat + 1

    x_ctx = ctx.reshape(1, rows, d)
    cond = jnp.concatenate([c, c_ctx[None], jnp.zeros((MOD_ROWS - n_all, d), F32)], axis=0)
    mods = _modulation(cond, mod_w, mod_b).reshape(DEPTH, MOD_ROWS, N_MOD, d)
    ln_g = ln_g.reshape(DEPTH * 3, 1, d)
    ln_b = ln_b.reshape(DEPTH * 3, 1, d)

    ffn_w1 = _ffn_w1_layout(ffn_w_in.reshape(DEPTH * 2, d, 2 * D_FF))
    ffn_w2 = ffn_w_out.reshape(DEPTH * 2, D_FF // FF_CHUNK, FF_CHUNK, d).astype(BF16)

    na_bias = _na_bias_table(na_rpb)

    xs = x
    for l in range(DEPTH):
        kind = l % N_MIXERS
        idx = l // N_MIXERS
        ctx_out = l < DEPTH - 1
        ctx_in = ctx_out or kind != 2
        n_in = n_all if ctx_in else n_lat
        n_out = n_all if ctx_out else n_lat

        first_ctx = x_ctx if (l == 0 and ctx_in) else None
        xs = _ffn_half(xs, first_ctx, mods, l, 0, ffn_w1, ffn_w2, ln_g, ln_b, n_in)

        if kind == 0:
            qkv = _mixer_proj(xs, mods, l, na_w_qkv[idx].astype(BF16), BF16)
            a_lat = _na_attention(qkv, na_bias, idx, n_lat, ctx_len)
            a_ctx = _ctx_attention(qkv, n_lat, ctx_len) if ctx_out else None
            xs = _mixer_outproj(a_lat, a_ctx, xs, mods, l, na_w_o[idx].astype(BF16), ln_g, ln_b)
        elif kind == 1:
            proj = _mixer_proj(xs, mods, l, lru_w_in[idx].astype(BF16), F32)
            params = _lru_params(lru_conv_w[idx], lru_conv_b[idx], lru_w_gates[idx], lru_b_gates[idx],
                                 lru_lambda[idx])
            zeros = jnp.zeros((n_lat, 2, D_RNN), F32)
            a_ctx, h_ctx = _lru_scan(proj, params, zeros, n_lat, ctx_len, n_lat, n_lat)
            a_lat, _ = _lru_scan(proj, params, h_ctx, n_lat, rows, 0, 1)
            xs = _mixer_outproj(a_lat, a_ctx if ctx_out else None, xs, mods, l, lru_w_out[idx].astype(BF16),
                                ln_g, ln_b)
        else:
            w1 = _chunk_cols(sc_w_in[idx], 3, SC_CHUNK).astype(BF16)
            w2 = sc_w_out[idx].reshape(d // SC_CHUNK, SC_CHUNK, d).astype(BF16)
            cw = jnp.transpose(sc_conv_w[idx].reshape(SC_CONV_W, d // SC_CHUNK, SC_CHUNK), (1, 0, 2))
            xs = _sc_mixer(xs, mods, l, w1, w2, cw, ln_g, ln_b, n_out, n_lat, ctx_len)

        xs = _ffn_half(xs, None, mods, l, 2, ffn_w1, ffn_w2, ln_g, ln_b, n_out)
    return xs
```

```python
import functools

import jax
import jax.numpy as jnp
from jax import lax
from jax.experimental import pallas as pl
from jax.experimental.pallas import tpu as pltpu

F32 = jnp.float32
BF16 = jnp.bfloat16

D_MODEL = 1024
DEPTH = 4
GRID_W = 64
N_MIXERS = 3
NA_HEADS = 16
NA_HEAD_DIM = D_MODEL // NA_HEADS
WIN_ROWS = 8
WIN_COLS = 16
D_RNN = 1408
LRU_BLOCKS = 16
LRU_BLOCK_W = D_RNN // LRU_BLOCKS
LRU_CONV_W = 4
LRU_C = 8.0
SQRT_FLOOR = 1e-30
SC_CONV_W = 3
D_FF = 2816
N_MOD = 9
ALPHA = (2 * DEPTH) ** 0.25
LN_EPS = 1e-5
NEG_INF = -1e30

LANES = 128
SUBLANES = 8
VMEM_LIMIT_BYTES = 56 * 1024 * 1024

ROW_TILE = 512
FF_CHUNK = 256
SC_CHUNK = 512
W1_LAYOUT_ROWS = 256
MOD_COLS = 2304
MOD_ROWS = 16
LRU_TILE = LANES
LRU_KW = 3 * LANES
HALO = SUBLANES
PAIR = 2 * NA_HEAD_DIM
NA_ROWS_PER_STEP = 4


def _params(*sem):
    return pltpu.CompilerParams(dimension_semantics=sem, vmem_limit_bytes=VMEM_LIMIT_BYTES)


def _resident(shape):
    zeros = (0,) * len(shape)
    return pl.BlockSpec(shape, lambda *_: zeros, pipeline_mode=pl.Buffered(1))


def _layer_norm(t, g, b):
    mu = jnp.mean(t, axis=-1, keepdims=True)
    d = t - mu
    var = jnp.mean(d * d, axis=-1, keepdims=True)
    return d * lax.rsqrt(var + LN_EPS) * g + b


def _mod_spec(mods, l):
    return pl.BlockSpec((None, None) + mods.shape[2:], lambda g, i: (l, g, 0, 0))


def _ln_spec(ln, l, j):
    return pl.BlockSpec((None,) + ln.shape[1:], lambda g, i: (3 * l + j, 0, 0))


def _layer_weight_spec(w, idx):
    zeros = (0,) * (w.ndim - 1)
    return pl.BlockSpec((None,) + w.shape[1:], lambda *_: (idx,) + zeros, pipeline_mode=pl.Buffered(1))


def _mod_rows(m_ref, j):
    return m_ref[3 * j:3 * j + 1, :], m_ref[3 * j + 1:3 * j + 2, :], m_ref[3 * j + 2:3 * j + 3, :]


def _mod_kernel(c_ref, w_ref, b_ref, o_ref):
    c = c_ref[...]
    s = (c * jax.nn.sigmoid(c)).astype(BF16)
    o_ref[...] = jnp.dot(s, w_ref[...].astype(BF16), preferred_element_type=F32) + b_ref[...]


def _modulation(cond, mod_w, mod_b):
    depth, d, n = mod_w.shape
    return pl.pallas_call(
        _mod_kernel,
        grid=(depth, n // MOD_COLS),
        in_specs=[
            pl.BlockSpec((MOD_ROWS, d), lambda l, j: (0, 0)),
            pl.BlockSpec((None, d, MOD_COLS), lambda l, j: (l, 0, j)),
            pl.BlockSpec((None, 1, MOD_COLS), lambda l, j: (l, 0, j)),
        ],
        out_specs=pl.BlockSpec((None, MOD_ROWS, MOD_COLS), lambda l, j: (l, 0, j)),
        out_shape=jax.ShapeDtypeStruct((depth, MOD_ROWS, n), F32),
        compiler_params=_params("arbitrary", "arbitrary"),
        name="modulation",
    )(cond, mod_w, mod_b.reshape(depth, 1, n))


def _ffn_kernel(*refs, j, n_lat, has_ctx):
    if has_ctx:
        xl_ref, xc_ref, m_ref, w1_ref, w2_ref, g_ref, b_ref, o_ref, h_ref = refs
        x = jnp.where(pl.program_id(0) == n_lat, xc_ref[...], xl_ref[...])
    else:
        xl_ref, m_ref, w1_ref, w2_ref, g_ref, b_ref, o_ref, h_ref = refs
        x = xl_ref[...]
    shift, scale, gate = _mod_rows(m_ref, j)
    h_ref[...] = (x * (1 + scale) + shift).astype(BF16)
    acc = None
    for c in range(w1_ref.shape[0]):
        gu = jnp.dot(h_ref[...], w1_ref[c], preferred_element_type=F32)
        g = gu[:, :FF_CHUNK]
        u = gu[:, FF_CHUNK:]
        a = (g * jax.nn.sigmoid(g) * u).astype(BF16)
        y = jnp.dot(a, w2_ref[c], preferred_element_type=F32)
        acc = y if acc is None else acc + y
    t = ALPHA * x + (0.5 * gate) * acc
    o_ref[...] = _layer_norm(t, g_ref[...], b_ref[...])


def _ffn_half(x, x_ctx, mods, l, j, w1, w2, ln_g, ln_b, n_groups):
    n_lat, rows, d = x.shape
    has_ctx = x_ctx is not None
    if has_ctx:
        assert n_groups == n_lat + 1
        in_specs = [pl.BlockSpec((None, ROW_TILE, d), lambda g, i: (jnp.minimum(g, n_lat - 1), i, 0)),
                    pl.BlockSpec((None, ROW_TILE, d), lambda g, i: (0, i, 0))]
        args = [x, x_ctx]
    else:
        in_specs = [pl.BlockSpec((None, ROW_TILE, d), lambda g, i: (g, i, 0))]
        args = [x]
    slab = 2 * l + j // 2
    in_specs += [_mod_spec(mods, l), _layer_weight_spec(w1, slab), _layer_weight_spec(w2, slab),
                 _ln_spec(ln_g, l, j), _ln_spec(ln_b, l, j)]
    return pl.pallas_call(
        functools.partial(_ffn_kernel, j=j, n_lat=n_lat, has_ctx=has_ctx),
        grid=(n_groups, rows // ROW_TILE),
        in_specs=in_specs,
        out_specs=pl.BlockSpec((None, ROW_TILE, d), lambda g, i: (g, i, 0)),
        out_shape=jax.ShapeDtypeStruct((n_groups, rows, d), F32),
        scratch_shapes=[pltpu.VMEM((ROW_TILE, d), BF16)],
        compiler_params=_params("arbitrary", "arbitrary"),
        name="ffn_half",
    )(*args, mods, w1, w2, ln_g, ln_b)


def _w1_layout_kernel(g_ref, u_ref, o_ref):
    for c in range(o_ref.shape[0]):
        o_ref[c, :, :FF_CHUNK] = g_ref[:, FF_CHUNK * c:FF_CHUNK * (c + 1)].astype(BF16)
        o_ref[c, :, FF_CHUNK:] = u_ref[:, FF_CHUNK * c:FF_CHUNK * (c + 1)].astype(BF16)


def _ffn_w1_layout(w_in):
    n, d, _ = w_in.shape
    nc = D_FF // FF_CHUNK
    rt = W1_LAYOUT_ROWS
    return pl.pallas_call(
        _w1_layout_kernel,
        grid=(n, d // rt),
        in_specs=[pl.BlockSpec((None, rt, D_FF), lambda l, r: (l, r, 0)),
                  pl.BlockSpec((None, rt, D_FF), lambda l, r: (l, r, 1))],
        out_specs=pl.BlockSpec((None, nc, rt, 2 * FF_CHUNK), lambda l, r: (l, 0, r, 0)),
        out_shape=jax.ShapeDtypeStruct((n, nc, d, 2 * FF_CHUNK), BF16),
        compiler_params=_params("arbitrary", "arbitrary"),
        name="ffn_w1_layout",
    )(w_in, w_in)


def _proj_kernel(x_ref, m_ref, w_ref, o_ref):
    shift, scale, _ = _mod_rows(m_ref, 1)
    h = (x_ref[...] * (1 + scale) + shift).astype(BF16)
    o_ref[...] = jnp.dot(h, w_ref[...], preferred_element_type=F32).astype(o_ref.dtype)


def _mixer_proj(x, mods, l, w, out_dtype):
    n_groups, rows, d = x.shape
    n = w.shape[1]
    return pl.pallas_call(
        _proj_kernel,
        grid=(n_groups, rows // ROW_TILE),
        in_specs=[
            pl.BlockSpec((None, ROW_TILE, d), lambda g, i: (g, i, 0)),
            _mod_spec(mods, l),
            _resident(w.shape),
        ],
        out_specs=pl.BlockSpec((None, ROW_TILE, n), lambda g, i: (g, i, 0)),
        out_shape=jax.ShapeDtypeStruct((n_groups, rows, n), out_dtype),
        compiler_params=_params("arbitrary", "arbitrary"),
        name="mixer_proj",
    )(x, mods, w)


def _outproj_kernel(*refs, n_lat, has_ctx):
    if has_ctx:
        al_ref, ac_ref, x_ref, m_ref, w_ref, g_ref, b_ref, o_ref = refs
        a = jnp.where(pl.program_id(0) == n_lat, ac_ref[...], al_ref[...])
    else:
        al_ref, x_ref, m_ref, w_ref, g_ref, b_ref, o_ref = refs
        a = al_ref[...]
    _, _, gate = _mod_rows(m_ref, 1)
    y = jnp.dot(a, w_ref[...], preferred_element_type=F32)
    t = ALPHA * x_ref[...] + gate * y
    o_ref[...] = _layer_norm(t, g_ref[...], b_ref[...])


def _mixer_outproj(a_lat, a_ctx, x, mods, l, w, ln_g, ln_b):
    n_lat, rows, k = a_lat.shape
    d = x.shape[-1]
    has_ctx = a_ctx is not None
    n_groups = n_lat + 1 if has_ctx else n_lat
    in_specs = [pl.BlockSpec((None, ROW_TILE, k), lambda g, i: (jnp.minimum(g, n_lat - 1), i, 0))]
    args = [a_lat]
    if has_ctx:
        in_specs.append(pl.BlockSpec((None, ROW_TILE, k), lambda g, i: (0, i, 0)))
        args.append(a_ctx)
    in_specs += [
        pl.BlockSpec((None, ROW_TILE, d), lambda g, i: (g, i, 0)),
        _mod_spec(mods, l),
        _resident(w.shape),
        _ln_spec(ln_g, l, 1),
        _ln_spec(ln_b, l, 1),
    ]
    args += [x, mods, w, ln_g, ln_b]
    return pl.pallas_call(
        functools.partial(_outproj_kernel, n_lat=n_lat, has_ctx=has_ctx),
        grid=(n_groups, rows // ROW_TILE),
        in_specs=in_specs,
        out_specs=pl.BlockSpec((None, ROW_TILE, d), lambda g, i: (g, i, 0)),
        out_shape=jax.ShapeDtypeStruct((n_groups, rows, d), F32),
        compiler_params=_params("arbitrary", "arbitrary"),
        name="mixer_outproj",
    )(*args)


def _pair_queries(q2):
    lane = lax.broadcasted_iota(jnp.int32, q2.shape, 1)
    zero = jnp.zeros_like(q2)
    qs = q2 * jnp.asarray(NA_HEAD_DIM ** -0.5, q2.dtype)
    return jnp.concatenate([jnp.where(lane < NA_HEAD_DIM, qs, zero),
                            jnp.where(lane < NA_HEAD_DIM, zero, qs)], axis=0)


def _unpair(o2):
    n = o2.shape[0] // 2
    lane = lax.broadcasted_iota(jnp.int32, (n, PAIR), 1)
    return jnp.where(lane < NA_HEAD_DIM, o2[:n], o2[n:])


def _scores(q, k):
    return lax.dot_general(q, k, (((1,), (1,)), ((), ())), preferred_element_type=F32)


def _na_kernel(q_ref, k_ref, v_ref, kc_ref, vc_ref, *rest, n_rows):
    bias_refs = rest[:NA_ROWS_PER_STEP]
    o_ref = rest[NA_ROWS_PER_STEP]
    scr = rest[NA_ROWS_PER_STEP + 1:]
    n_loc = WIN_ROWS * GRID_W
    n_pairs = NA_HEADS // 2
    starts = []
    for i in range(NA_ROWS_PER_STEP):
        r = pl.program_id(1) * NA_ROWS_PER_STEP + i
        r0 = jnp.clip(r - WIN_ROWS // 2, 0, n_rows - WIN_ROWS)
        start = pl.multiple_of(r0 * GRID_W, GRID_W)
        starts.append(start)
        s_scr = scr[2 * i]
        for p in range(n_pairs):
            sl = slice(PAIR * p, PAIR * (p + 1))
            q = _pair_queries(q_ref[i * GRID_W:(i + 1) * GRID_W, sl])
            s_scr[p, :, :n_loc] = _scores(q, k_ref[pl.ds(start, n_loc), sl]) + bias_refs[i][p]
            s_scr[p, :, n_loc:] = _scores(q, kc_ref[:, sl])
    for i in range(NA_ROWS_PER_STEP):
        s_scr, p_scr = scr[2 * i], scr[2 * i + 1]
        start = starts[i]
        s = s_scr[...]
        e = jnp.exp(s - jnp.max(s, axis=-1, keepdims=True))
        p_scr[...] = e.astype(BF16)
        inv = 1.0 / jnp.sum(e, axis=-1, keepdims=True)
        for p in range(n_pairs):
            sl = slice(PAIR * p, PAIR * (p + 1))
            o2 = (jnp.dot(p_scr[p, :, :n_loc], v_ref[pl.ds(start, n_loc), sl], preferred_element_type=F32)
                  + jnp.dot(p_scr[p, :, n_loc:], vc_ref[:, sl], preferred_element_type=F32))
            o_ref[i * GRID_W:(i + 1) * GRID_W, sl] = _unpair(o2 * inv[p]).astype(o_ref.dtype)


def _na_attention(qkv, bias, idx, n_lat, ctx_len):
    _, rows, _ = qkv.shape
    d = D_MODEL
    n_rows = rows // GRID_W
    half = WIN_ROWS // 2
    nr = NA_ROWS_PER_STEP
    assert n_rows % nr == 0

    def variant(r):
        return r - jnp.clip(r - half, 0, n_rows - WIN_ROWS)

    n_keys = WIN_ROWS * GRID_W + ctx_len
    scratch = []
    for _ in range(nr):
        scratch += [pltpu.VMEM((NA_HEADS // 2, PAIR, n_keys), F32), pltpu.VMEM((NA_HEADS // 2, PAIR, n_keys), BF16)]
    return pl.pallas_call(
        functools.partial(_na_kernel, n_rows=n_rows),
        grid=(n_lat, n_rows // nr),
        in_specs=[
            pl.BlockSpec((None, nr * GRID_W, d), lambda b, t: (b, t, 0)),
            pl.BlockSpec((None, rows, d), lambda b, t: (b, 0, 1)),
            pl.BlockSpec((None, rows, d), lambda b, t: (b, 0, 2)),
            pl.BlockSpec((None, ctx_len, d), lambda b, t: (n_lat, b, 1)),
            pl.BlockSpec((None, ctx_len, d), lambda b, t: (n_lat, b, 2)),
        ] + [pl.BlockSpec((None, None) + bias.shape[2:], lambda b, t, i=i: (idx, variant(t * nr + i), 0, 0, 0))
             for i in range(nr)],
        out_specs=pl.BlockSpec((None, nr * GRID_W, d), lambda b, t: (b, t, 0)),
        out_shape=jax.ShapeDtypeStruct((n_lat, rows, d), BF16),
        scratch_shapes=scratch,
        compiler_params=_params("arbitrary", "arbitrary"),
        name="na_attention",
    )(qkv, qkv, qkv, qkv, qkv, *([bias] * nr))


def _ctx_attn_kernel(q_ref, k_ref, v_ref, o_ref):
    for p in range(NA_HEADS // 2):
        sl = slice(PAIR * p, PAIR * (p + 1))
        s = _scores(_pair_queries(q_ref[:, sl]), k_ref[:, sl])
        e = jnp.exp(s - jnp.max(s, axis=-1, keepdims=True))
        pr = (e * (1.0 / jnp.sum(e, axis=-1, keepdims=True))).astype(BF16)
        o_ref[:, sl] = _unpair(jnp.dot(pr, v_ref[:, sl], preferred_element_type=F32)).astype(o_ref.dtype)


def _ctx_attention(qkv, n_lat, ctx_len):
    _, rows, _ = qkv.shape
    d = D_MODEL
    return pl.pallas_call(
        _ctx_attn_kernel,
        grid=(n_lat,),
        in_specs=[pl.BlockSpec((None, ctx_len, d), lambda b, c=c: (n_lat, b, c)) for c in range(3)],
        out_specs=pl.BlockSpec((None, ctx_len, d), lambda b: (0, b, 0)),
        out_shape=jax.ShapeDtypeStruct((1, rows, d), BF16),
        compiler_params=_params("arbitrary"),
        name="ctx_attention",
    )(qkv, qkv, qkv)


def _na_bias_table(rpb):
    col = jnp.arange(GRID_W)
    col_start = jnp.clip(col - WIN_COLS // 2, 0, GRID_W - WIN_COLS)
    kcol = col[None, :]
    col_in = (kcol >= col_start[:, None]) & (kcol < col_start[:, None] + WIN_COLS)
    dcol = jnp.clip(kcol - col[:, None], 1 - WIN_COLS, WIN_COLS - 1) + WIN_COLS - 1
    drow = jnp.arange(WIN_ROWS)[None, :] - jnp.arange(WIN_ROWS)[:, None] + WIN_ROWS - 1
    hot_c = (dcol[None] == jnp.arange(2 * WIN_COLS - 1)[:, None, None]).astype(F32)
    hot_d = (drow[:, :, None] == jnp.arange(2 * WIN_ROWS - 1)).astype(F32)
    tbl = jnp.einsum('nhdc,vjd,cqk->nvhqjk', rpb.astype(F32), hot_d, hot_c, precision=lax.Precision.HIGHEST)
    tbl = jnp.where(col_in[:, None, :], tbl, NEG_INF)
    return tbl.reshape(rpb.shape[0], WIN_ROWS, NA_HEADS // 2, 2 * GRID_W, WIN_ROWS * GRID_W)


def _lru_tile_starts():
    starts = []
    for j in range(D_RNN // LRU_TILE):
        lo = (LRU_TILE * j // LRU_BLOCK_W) * LRU_BLOCK_W
        hi = ((LRU_TILE * (j + 1) - 1) // LRU_BLOCK_W + 1) * LRU_BLOCK_W
        ks = min(lo // LANES * LANES, D_RNN - LRU_KW)
        assert ks <= lo and hi <= ks + LRU_KW
        starts.append(ks)
    return starts


def _shift_rows(x, s, row):
    n = x.shape[0]
    if s > 0:
        return jnp.where(row >= s, pltpu.roll(x, s, 0), 0.0)
    return jnp.where(row < n + s, pltpu.roll(x, n + s, 0), 0.0)


def _dwconv(u, w_ref, row, left):
    acc = None
    for i in range(w_ref.shape[0]):
        s = left - i
        term = w_ref[i:i + 1, :] * (u if s == 0 else _shift_rows(u, s, row))
        acc = term if acc is None else acc + term
    return acc


def _lru_kernel(kb_ref, u0_ref, u1_ref, u2_ref, gate_ref, w0_ref, w1_ref, w2_ref, bg_ref,
                cw0_ref, cw1_ref, cw2_ref, cb0_ref, cb1_ref, cb2_ref, lam_ref, h0_ref,
                y_ref, hfin_ref, uc_scr, a_scr, b_scr, pa_scr, pb_scr, cin_scr, y_scr, l1a_scr, l1b_scr):
    t_len = gate_ref.shape[0]
    n_tiles = t_len // SUBLANES
    left = LRU_CONV_W // 2
    row = lax.broadcasted_iota(jnp.int32, (t_len, 1), 0)

    j = pl.program_id(1)
    kb = kb_ref[j]
    kb_prev = kb_ref[jnp.maximum(j - 1, 0)]

    def conv_store(u_ref, cw_ref, cb_ref, blk):
        uc_scr[blk] = _dwconv(u_ref[...], cw_ref, row, left) + cb_ref[...]

    @pl.when(j == 0)
    def _():
        conv_store(u0_ref, cw0_ref, cb0_ref, kb)
        conv_store(u1_ref, cw1_ref, cb1_ref, kb + 1)
        conv_store(u2_ref, cw2_ref, cb2_ref, kb + 2)

    @pl.when(jnp.logical_and(j > 0, kb > kb_prev))
    def _():
        conv_store(u2_ref, cw2_ref, cb2_ref, kb + 2)

    uc = jnp.concatenate([uc_scr[kb + w] for w in range(3)], axis=1)
    wt = jnp.concatenate([jnp.concatenate([w_ref[q] for q in range(4)], axis=1)
                          for w_ref in (w0_ref, w1_ref, w2_ref)], axis=0)
    bg = jnp.concatenate([bg_ref[q:q + 1, :] for q in range(4)], axis=1)
    gates = jnp.dot(uc.astype(BF16), wt, preferred_element_type=F32) + bg
    ut = uc_scr[j]

    for d in range(2):
        r = jax.nn.sigmoid(gates[:, (2 * d) * LRU_TILE:(2 * d + 1) * LRU_TILE])
        i = jax.nn.sigmoid(gates[:, (2 * d + 1) * LRU_TILE:(2 * d + 2) * LRU_TILE])
        neg_lam = -lam_ref[d:d + 1, :]
        softplus = jnp.maximum(neg_lam, 0.0) + jnp.log1p(jnp.exp(-jnp.abs(neg_lam)))
        a = jnp.exp(-LRU_C * r * softplus)
        a_scr[d] = a
        v = 1.0 - a * a
        b_scr[d] = (v * lax.rsqrt(jnp.maximum(v, SQRT_FLOOR))) * (i * ut)

    rowk = lax.broadcasted_iota(jnp.int32, (n_tiles, 1), 0)
    for d, rev in ((0, False), (1, True)):
        acc_a = acc_b = None
        for n in range(SUBLANES):
            i = SUBLANES - 1 - n if rev else n
            ai = a_scr[d, pl.ds(i, n_tiles, stride=SUBLANES), :]
            bi = b_scr[d, pl.ds(i, n_tiles, stride=SUBLANES), :]
            if n == 0:
                acc_a, acc_b = ai, bi
            else:
                acc_b = ai * acc_b + bi
                acc_a = ai * acc_a
            pa_scr[d, i] = acc_a
            pb_scr[d, i] = acc_b
        h0 = h0_ref[d:d + 1, :]
        if n_tiles >= SUBLANES * SUBLANES:
            n2 = n_tiles // SUBLANES
            l1a_scr[...] = acc_a
            l1b_scr[...] = acc_b
            row2 = lax.broadcasted_iota(jnp.int32, (n2, 1), 0)
            p2a, p2b = [None] * SUBLANES, [None] * SUBLANES
            ca = cb = None
            for n in range(SUBLANES):
                i = SUBLANES - 1 - n if rev else n
                ai = l1a_scr[pl.ds(i, n2, stride=SUBLANES), :]
                bi = l1b_scr[pl.ds(i, n2, stride=SUBLANES), :]
                if n == 0:
                    ca, cb = ai, bi
                else:
                    cb = ai * cb + bi
                    ca = ai * ca
                p2a[i], p2b[i] = ca, cb
            s = 1
            while s < n2:
                if rev:
                    valid = row2 < n2 - s
                    a_sh, b_sh = pltpu.roll(ca, n2 - s, 0), pltpu.roll(cb, n2 - s, 0)
                else:
                    valid = row2 >= s
                    a_sh, b_sh = pltpu.roll(ca, s, 0), pltpu.roll(cb, s, 0)
                cb = jnp.where(valid, ca * b_sh + cb, cb)
                ca = jnp.where(valid, ca * a_sh, ca)
                s *= 2
            st2 = ca * h0 + cb
            if rev:
                c2 = jnp.where(row2 < n2 - 1, pltpu.roll(st2, n2 - 1, 0), h0)
                hfin_ref[d:d + 1, :] = st2[0:1, :]
            else:
                c2 = jnp.where(row2 >= 1, pltpu.roll(st2, 1, 0), h0)
                hfin_ref[d:d + 1, :] = st2[n2 - 1:n2, :]
            prev = c2
            for n in range(SUBLANES):
                i = SUBLANES - 1 - n if rev else n
                cin_scr[d, pl.ds(i, n2, stride=SUBLANES), :] = prev
                prev = p2a[i] * c2 + p2b[i]
        else:
            s = 1
            while s < n_tiles:
                if rev:
                    valid = rowk < n_tiles - s
                    a_sh, b_sh = pltpu.roll(acc_a, n_tiles - s, 0), pltpu.roll(acc_b, n_tiles - s, 0)
                else:
                    valid = rowk >= s
                    a_sh, b_sh = pltpu.roll(acc_a, s, 0), pltpu.roll(acc_b, s, 0)
                acc_b = jnp.where(valid, acc_a * b_sh + acc_b, acc_b)
                acc_a = jnp.where(valid, acc_a * a_sh, acc_a)
                s *= 2
            state = acc_a * h0 + acc_b
            if rev:
                cin_scr[d] = jnp.where(rowk < n_tiles - 1, pltpu.roll(state, n_tiles - 1, 0), h0)
                hfin_ref[d:d + 1, :] = state[0:1, :]
            else:
                cin_scr[d] = jnp.where(rowk >= 1, pltpu.roll(state, 1, 0), h0)
                hfin_ref[d:d + 1, :] = state[n_tiles - 1:n_tiles, :]

    for i in range(SUBLANES):
        h_sum = (pa_scr[0, i] * cin_scr[0] + pb_scr[0, i]) + (pa_scr[1, i] * cin_scr[1] + pb_scr[1, i])
        gi = gate_ref[pl.ds(i, n_tiles, stride=SUBLANES), :]
        y_scr[pl.ds(i, n_tiles, stride=SUBLANES), :] = jax.nn.gelu(gi) * h_sum
    y_ref[...] = y_scr[...].astype(y_ref.dtype)


def _lru_scan(proj, params, h0, n_seq, seq_len, group0, per_group):
    kb, dense, bg, conv_w, conv_b, lam = params
    n_tiles = D_RNN // LRU_TILE
    col0 = D_RNN // LRU_TILE
    rows = proj.shape[1]

    def seq_block(s):
        return group0 + s // per_group, s % per_group

    def u_spec(off):
        return pl.BlockSpec((None, seq_len, LRU_TILE),
                            lambda s, j, kb_ref: (*seq_block(s), col0 + kb_ref[j] + off))

    def win_spec(n_rows, off):
        return pl.BlockSpec((n_rows, LRU_TILE), lambda s, j, kb_ref: (0, kb_ref[j] + off))

    def tile_spec(n_rows):
        return pl.BlockSpec((n_rows, LRU_TILE), lambda s, j, kb_ref: (0, j))

    def w_spec(off):
        return pl.BlockSpec((4, LRU_TILE, LRU_TILE), lambda s, j, kb_ref: (0, kb_ref[j] + off, j))

    grid_spec = pltpu.PrefetchScalarGridSpec(
        num_scalar_prefetch=1,
        grid=(n_seq, n_tiles),
        in_specs=[
            u_spec(0), u_spec(1), u_spec(2),
            pl.BlockSpec((None, seq_len, LRU_TILE), lambda s, j, kb_ref: (*seq_block(s), j)),
            w_spec(0), w_spec(1), w_spec(2),
            tile_spec(4),
            win_spec(LRU_CONV_W, 0), win_spec(LRU_CONV_W, 1), win_spec(LRU_CONV_W, 2),
            win_spec(1, 0), win_spec(1, 1), win_spec(1, 2),
            tile_spec(2),
            pl.BlockSpec((None, 2, LRU_TILE), lambda s, j, kb_ref: (s, 0, j)),
        ],
        out_specs=[
            pl.BlockSpec((None, seq_len, LRU_TILE), lambda s, j, kb_ref: (s // per_group, s % per_group, j)),
            pl.BlockSpec((None, 2, LRU_TILE), lambda s, j, kb_ref: (s, 0, j)),
        ],
        scratch_shapes=[
            pltpu.VMEM((n_tiles, seq_len, LRU_TILE), F32),
            pltpu.VMEM((2, seq_len, LRU_TILE), F32),
            pltpu.VMEM((2, seq_len, LRU_TILE), F32),
            pltpu.VMEM((2, SUBLANES, seq_len // SUBLANES, LRU_TILE), F32),
            pltpu.VMEM((2, SUBLANES, seq_len // SUBLANES, LRU_TILE), F32),
            pltpu.VMEM((2, seq_len // SUBLANES, LRU_TILE), F32),
            pltpu.VMEM((seq_len, LRU_TILE), F32),
            pltpu.VMEM((seq_len // SUBLANES, LRU_TILE), F32),
            pltpu.VMEM((seq_len // SUBLANES, LRU_TILE), F32),
        ],
    )
    return pl.pallas_call(
        _lru_kernel,
        grid_spec=grid_spec,
        out_shape=[jax.ShapeDtypeStruct((n_seq // per_group, rows, D_RNN), BF16),
                   jax.ShapeDtypeStruct((n_seq, 2, D_RNN), F32)],
        compiler_params=_params("arbitrary", "arbitrary"),
        name="rglru_scan",
    )(kb, proj, proj, proj, proj, dense, dense, dense, bg, conv_w, conv_w, conv_w,
      conv_b, conv_b, conv_b, lam, h0)


def _lru_params(conv_w, conv_b, w_gates, b_gates, lam):
    bw, nb = LRU_BLOCK_W, LRU_BLOCKS
    w = jnp.transpose(w_gates.reshape(4, nb, bw, bw), (0, 2, 1, 3))
    w = jnp.pad(w, ((0, 0), (0, 0), (0, 0), (0, D_RNN))).reshape(4, bw, nb * (bw + D_RNN))
    w = w[:, :, :nb * D_RNN].reshape(4, bw, nb, D_RNN)
    dense = jnp.transpose(w, (0, 2, 1, 3)).reshape(4, D_RNN, D_RNN).astype(BF16)
    kb = jnp.asarray([ks // LANES for ks in _lru_tile_starts()], jnp.int32)
    return kb, dense, b_gates.reshape(4, D_RNN), conv_w, conv_b.reshape(1, D_RNN), lam


def _sc_kernel(xm_ref, xp_ref, xn_ref, m_ref, w1_ref, w2_ref, cw_ref, g_ref, b_ref, o_ref, h_ref,
               *, n_lat, ctx_len, rows):
    tm = xm_ref.shape[0]
    ext = tm + 2 * HALO
    shift, scale, gate = _mod_rows(m_ref, 1)

    def mod(x):
        return (x * (1 + scale) + shift).astype(BF16)

    h_ref[0:HALO, :] = mod(xp_ref[...])
    h_ref[HALO:HALO + tm, :] = mod(xm_ref[...])
    h_ref[HALO + tm:ext, :] = mod(xn_ref[...])

    pos = pl.program_id(1) * tm + lax.broadcasted_iota(jnp.int32, (tm, 1), 0)
    is_ctx = pl.program_id(0) == n_lat
    seq_pos = jnp.where(is_ctx, pos & (ctx_len - 1), pos)
    seq_last = jnp.where(is_ctx, ctx_len - 1, rows - 1)
    has_prev = seq_pos != 0
    has_next = seq_pos != seq_last

    acc = None
    for c in range(w1_ref.shape[0]):
        t = jnp.dot(h_ref[...], w1_ref[c], preferred_element_type=F32)
        bgate = t[HALO:HALO + tm, :SC_CHUNK]
        w = t[:, SC_CHUNK:2 * SC_CHUNK] * t[:, 2 * SC_CHUNK:]
        w_prev = pltpu.roll(w, 1, 0)[HALO:HALO + tm]
        w_next = pltpu.roll(w, ext - 1, 0)[HALO:HALO + tm]
        cw = cw_ref[c]
        conv = (cw[0:1, :] * jnp.where(has_prev, w_prev, 0.0) + cw[1:2, :] * w[HALO:HALO + tm]
                + cw[2:3, :] * jnp.where(has_next, w_next, 0.0))
        y = jnp.dot((bgate * conv).astype(BF16), w2_ref[c], preferred_element_type=F32)
        acc = y if acc is None else acc + y
    t = ALPHA * xm_ref[...] + gate * acc
    o_ref[...] = _layer_norm(t, g_ref[...], b_ref[...])


def _sc_mixer(x, mods, l, w1, w2, cw, ln_g, ln_b, n_groups, n_lat, ctx_len):
    _, rows, d = x.shape
    tm = ROW_TILE
    halo_per_tile = tm // HALO
    n_halo = rows // HALO
    assert ctx_len & (ctx_len - 1) == 0 and SC_CONV_W == 3
    return pl.pallas_call(
        functools.partial(_sc_kernel, n_lat=n_lat, ctx_len=ctx_len, rows=rows),
        grid=(n_groups, rows // tm),
        in_specs=[
            pl.BlockSpec((None, tm, d), lambda g, i: (g, i, 0)),
            pl.BlockSpec((None, HALO, d), lambda g, i: (g, jnp.maximum(i * halo_per_tile - 1, 0), 0)),
            pl.BlockSpec((None, HALO, d), lambda g, i: (g, jnp.minimum((i + 1) * halo_per_tile, n_halo - 1), 0)),
            _mod_spec(mods, l),
            _resident(w1.shape),
            _resident(w2.shape),
            _resident(cw.shape),
            _ln_spec(ln_g, l, 1),
            _ln_spec(ln_b, l, 1),
        ],
        out_specs=pl.BlockSpec((None, tm, d), lambda g, i: (g, i, 0)),
        out_shape=jax.ShapeDtypeStruct((n_groups, rows, d), F32),
        scratch_shapes=[pltpu.VMEM((tm + 2 * HALO, d), BF16)],
        compiler_params=_params("arbitrary", "arbitrary"),
        name="sc_mixer",
    )(x, x, x, mods, w1, w2, cw, ln_g, ln_b)


def _chunk_cols(w, parts, chunk):
    *lead, k, pn = w.shape
    n = pn // parts
    w = w.reshape(*lead, k, parts, n // chunk, chunk)
    nl = len(lead)
    w = jnp.transpose(w, tuple(range(nl)) + (nl + 2, nl, nl + 1, nl + 3))
    return w.reshape(*lead, n // chunk, k, parts * chunk)


def kernel(x, c, ctx, c_ctx, mod_w, mod_b, ln_g, ln_b, ffn_w_in, ffn_w_out, na_w_qkv, na_w_o, na_rpb,
           lru_w_in, lru_conv_w, lru_conv_b, lru_w_gates, lru_b_gates, lru_lambda, lru_w_out,
           sc_w_in, sc_conv_w, sc_w_out):
    n_lat, rows, d = x.shape
    ctx_len = ctx.shape[1]
    assert d == D_MODEL and n_lat * ctx_len == rows and n_lat + 1 <= MOD_ROWS
    assert rows % ROW_TILE == 0 and rows % GRID_W == 0 and rows // GRID_W >= WIN_ROWS
    n_all = n_lat + 1

    x_ctx = ctx.reshape(1, rows, d)
    cond = jnp.concatenate([c, c_ctx[None], jnp.zeros((MOD_ROWS - n_all, d), F32)], axis=0)
    mods = _modulation(cond, mod_w, mod_b).reshape(DEPTH, MOD_ROWS, N_MOD, d)
    ln_g = ln_g.reshape(DEPTH * 3, 1, d)
    ln_b = ln_b.reshape(DEPTH * 3, 1, d)

    ffn_w1 = _ffn_w1_layout(ffn_w_in.reshape(DEPTH * 2, d, 2 * D_FF))
    ffn_w2 = ffn_w_out.reshape(DEPTH * 2, D_FF // FF_CHUNK, FF_CHUNK, d).astype(BF16)

    na_bias = _na_bias_table(na_rpb)

    xs = x
    for l in range(DEPTH):
        kind = l % N_MIXERS
        idx = l // N_MIXERS
        ctx_out = l < DEPTH - 1
        ctx_in = ctx_out or kind != 2
        n_in = n_all if ctx_in else n_lat
        n_out = n_all if ctx_out else n_lat

        first_ctx = x_ctx if (l == 0 and ctx_in) else None
        xs = _ffn_half(xs, first_ctx, mods, l, 0, ffn_w1, ffn_w2, ln_g, ln_b, n_in)

        if kind == 0:
            qkv = _mixer_proj(xs, mods, l, na_w_qkv[idx].astype(BF16), BF16)
            a_lat = _na_attention(qkv, na_bias, idx, n_lat, ctx_len)
            a_ctx = _ctx_attention(qkv, n_lat, ctx_len) if ctx_out else None
            xs = _mixer_outproj(a_lat, a_ctx, xs, mods, l, na_w_o[idx].astype(BF16), ln_g, ln_b)
        elif kind == 1:
            proj = _mixer_proj(xs, mods, l, lru_w_in[idx].astype(BF16), F32)
            params = _lru_params(lru_conv_w[idx], lru_conv_b[idx], lru_w_gates[idx], lru_b_gates[idx],
                                 lru_lambda[idx])
            zeros = jnp.zeros((n_lat, 2, D_RNN), F32)
            a_ctx, h_ctx = _lru_scan(proj, params, zeros, n_lat, ctx_len, n_lat, n_lat)
            a_lat, _ = _lru_scan(proj, params, h_ctx, n_lat, rows, 0, 1)
            xs = _mixer_outproj(a_lat, a_ctx if ctx_out else None, xs, mods, l, lru_w_out[idx].astype(BF16),
                                ln_g, ln_b)
        else:
            w1 = _chunk_cols(sc_w_in[idx], 3, SC_CHUNK).astype(BF16)
            w2 = sc_w_out[idx].reshape(d // SC_CHUNK, SC_CHUNK, d).astype(BF16)
            cw = jnp.transpose(sc_conv_w[idx].reshape(SC_CONV_W, d // SC_CHUNK, SC_CHUNK), (1, 0, 2))
            xs = _sc_mixer(xs, mods, l, w1, w2, cw, ln_g, ln_b, n_out, n_lat, ctx_len)

        xs = _ffn_half(xs, None, mods, l, 2, ffn_w1, ffn_w2, ln_g, ln_b, n_out)
    return xs
```

```python
import functools

import jax
import jax.numpy as jnp
from jax import lax
from jax.experimental import pallas as pl
from jax.experimental.pallas import tpu as pltpu

F32 = jnp.float32
BF16 = jnp.bfloat16

D_MODEL = 1024
DEPTH = 4
GRID_W = 64
N_MIXERS = 3
NA_HEADS = 16
NA_HEAD_DIM = D_MODEL // NA_HEADS
WIN_ROWS = 8
WIN_COLS = 16
D_RNN = 1408
LRU_BLOCKS = 16
LRU_BLOCK_W = D_RNN // LRU_BLOCKS
LRU_CONV_W = 4
LRU_C = 8.0
SQRT_FLOOR = 1e-30
SC_CONV_W = 3
D_FF = 2816
N_MOD = 9
ALPHA = (2 * DEPTH) ** 0.25
LN_EPS = 1e-5
NEG_INF = -1e30

LANES = 128
SUBLANES = 8
VMEM_LIMIT_BYTES = 56 * 1024 * 1024

ROW_TILE = 512
FF_CHUNK = 256
SC_CHUNK = 512
W1_LAYOUT_ROWS = 256
MOD_COLS = 2304
MOD_ROWS = 16
LRU_TILE = LANES
LRU_KW = 3 * LANES
HALO = SUBLANES
PAIR = 2 * NA_HEAD_DIM
NA_ROWS_PER_STEP = 4


def _params(*sem):
    return pltpu.CompilerParams(dimension_semantics=sem, vmem_limit_bytes=VMEM_LIMIT_BYTES)


def _resident(shape):
    zeros = (0,) * len(shape)
    return pl.BlockSpec(shape, lambda *_: zeros, pipeline_mode=pl.Buffered(1))


def _layer_norm(t, g, b):
    mu = jnp.mean(t, axis=-1, keepdims=True)
    d = t - mu
    var = jnp.mean(d * d, axis=-1, keepdims=True)
    return d * lax.rsqrt(var + LN_EPS) * g + b


def _mod_spec(mods, l):
    return pl.BlockSpec((None, None) + mods.shape[2:], lambda g, i: (l, g, 0, 0))


def _ln_spec(ln, l, j):
    return pl.BlockSpec((None,) + ln.shape[1:], lambda g, i: (3 * l + j, 0, 0))


def _layer_weight_spec(w, idx):
    zeros = (0,) * (w.ndim - 1)
    return pl.BlockSpec((None,) + w.shape[1:], lambda *_: (idx,) + zeros, pipeline_mode=pl.Buffered(1))


def _mod_rows(m_ref, j):
    return m_ref[3 * j:3 * j + 1, :], m_ref[3 * j + 1:3 * j + 2, :], m_ref[3 * j + 2:3 * j + 3, :]


def _mod_kernel(c_ref, w_ref, b_ref, o_ref):
    c = c_ref[...]
    s = (c * jax.nn.sigmoid(c)).astype(BF16)
    o_ref[...] = jnp.dot(s, w_ref[...].astype(BF16), preferred_element_type=F32) + b_ref[...]


def _modulation(cond, mod_w, mod_b):
    depth, d, n = mod_w.shape
    return pl.pallas_call(
        _mod_kernel,
        grid=(depth, n // MOD_COLS),
        in_specs=[
            pl.BlockSpec((MOD_ROWS, d), lambda l, j: (0, 0)),
            pl.BlockSpec((None, d, MOD_COLS), lambda l, j: (l, 0, j)),
            pl.BlockSpec((None, 1, MOD_COLS), lambda l, j: (l, 0, j)),
        ],
        out_specs=pl.BlockSpec((None, MOD_ROWS, MOD_COLS), lambda l, j: (l, 0, j)),
        out_shape=jax.ShapeDtypeStruct((depth, MOD_ROWS, n), F32),
        compiler_params=_params("arbitrary", "arbitrary"),
        name="modulation",
    )(cond, mod_w, mod_b.reshape(depth, 1, n))


def _ffn_kernel(*refs, j, n_lat, has_ctx, n_mixer):
    refs = list(refs)
    xl_ref = refs.pop(0)
    xc_ref = refs.pop(0) if has_ctx else None
    a_refs = [refs.pop(0) for _ in range(n_mixer)]
    m_ref = refs.pop(0)
    if n_mixer:
        wo_ref, g1_ref, b1_ref = refs.pop(0), refs.pop(0), refs.pop(0)
    w1_ref, w2_ref, g_ref, b_ref, o_ref, h_ref = refs
    if has_ctx:
        x = jnp.where(pl.program_id(0) == n_lat, xc_ref[...], xl_ref[...])
    else:
        x = xl_ref[...]
    if n_mixer:
        a = a_refs[0][...]
        if n_mixer == 2:
            a = jnp.where(pl.program_id(0) == n_lat, a_refs[1][...], a)
        _, _, gate1 = _mod_rows(m_ref, 1)
        y = jnp.dot(a, wo_ref[...], preferred_element_type=F32)
        x = _layer_norm(ALPHA * x + gate1 * y, g1_ref[...], b1_ref[...])
    shift, scale, gate = _mod_rows(m_ref, j)
    h_ref[...] = (x * (1 + scale) + shift).astype(BF16)
    acc = None
    for c in range(w1_ref.shape[0]):
        gu = jnp.dot(h_ref[...], w1_ref[c], preferred_element_type=F32)
        g = gu[:, :FF_CHUNK]
        u = gu[:, FF_CHUNK:]
        act = (g * jax.nn.sigmoid(g) * u).astype(BF16)
        y = jnp.dot(act, w2_ref[c], preferred_element_type=F32)
        acc = y if acc is None else acc + y
    t = ALPHA * x + (0.5 * gate) * acc
    o_ref[...] = _layer_norm(t, g_ref[...], b_ref[...])


def _ffn_half(x, x_ctx, mods, l, j, w1, w2, ln_g, ln_b, n_groups, mixer=None):
    n_lat, rows, d = x.shape
    has_ctx = x_ctx is not None
    if has_ctx:
        assert n_groups == n_lat + 1
        in_specs = [pl.BlockSpec((None, ROW_TILE, d), lambda g, i: (jnp.minimum(g, n_lat - 1), i, 0)),
                    pl.BlockSpec((None, ROW_TILE, d), lambda g, i: (0, i, 0))]
        args = [x, x_ctx]
    else:
        in_specs = [pl.BlockSpec((None, ROW_TILE, d), lambda g, i: (g, i, 0))]
        args = [x]
    n_mixer = 0
    if mixer is not None:
        a_lat, a_ctx, w_out = mixer
        n_lat, _, k = a_lat.shape
        in_specs.append(pl.BlockSpec((None, ROW_TILE, k), lambda g, i: (jnp.minimum(g, n_lat - 1), i, 0)))
        args.append(a_lat)
        n_mixer = 1
        if a_ctx is not None:
            in_specs.append(pl.BlockSpec((None, ROW_TILE, k), lambda g, i: (0, i, 0)))
            args.append(a_ctx)
            n_mixer = 2
    slab = 2 * l + j // 2
    in_specs.append(_mod_spec(mods, l))
    args.append(mods)
    if mixer is not None:
        in_specs += [_resident(w_out.shape), _ln_spec(ln_g, l, 1), _ln_spec(ln_b, l, 1)]
        args += [w_out, ln_g, ln_b]
    in_specs += [_layer_weight_spec(w1, slab), _layer_weight_spec(w2, slab),
                 _ln_spec(ln_g, l, j), _ln_spec(ln_b, l, j)]
    args += [w1, w2, ln_g, ln_b]
    return pl.pallas_call(
        functools.partial(_ffn_kernel, j=j, n_lat=n_lat, has_ctx=has_ctx, n_mixer=n_mixer),
        grid=(n_groups, rows // ROW_TILE),
        in_specs=in_specs,
        out_specs=pl.BlockSpec((None, ROW_TILE, d), lambda g, i: (g, i, 0)),
        out_shape=jax.ShapeDtypeStruct((n_groups, rows, d), F32),
        scratch_shapes=[pltpu.VMEM((ROW_TILE, d), BF16)],
        compiler_params=_params("arbitrary", "arbitrary"),
        name="ffn_half",
    )(*args)


def _w1_layout_kernel(g_ref, u_ref, o_ref):
    for c in range(o_ref.shape[0]):
        o_ref[c, :, :FF_CHUNK] = g_ref[:, FF_CHUNK * c:FF_CHUNK * (c + 1)].astype(BF16)
        o_ref[c, :, FF_CHUNK:] = u_ref[:, FF_CHUNK * c:FF_CHUNK * (c + 1)].astype(BF16)


def _ffn_w1_layout(w_in):
    n, d, _ = w_in.shape
    nc = D_FF // FF_CHUNK
    rt = W1_LAYOUT_ROWS
    return pl.pallas_call(
        _w1_layout_kernel,
        grid=(n, d // rt),
        in_specs=[pl.BlockSpec((None, rt, D_FF), lambda l, r: (l, r, 0)),
                  pl.BlockSpec((None, rt, D_FF), lambda l, r: (l, r, 1))],
        out_specs=pl.BlockSpec((None, nc, rt, 2 * FF_CHUNK), lambda l, r: (l, 0, r, 0)),
        out_shape=jax.ShapeDtypeStruct((n, nc, d, 2 * FF_CHUNK), BF16),
        compiler_params=_params("arbitrary", "arbitrary"),
        name="ffn_w1_layout",
    )(w_in, w_in)


def _proj_kernel(x_ref, m_ref, w_ref, o_ref):
    shift, scale, _ = _mod_rows(m_ref, 1)
    h = (x_ref[...] * (1 + scale) + shift).astype(BF16)
    o_ref[...] = jnp.dot(h, w_ref[...], preferred_element_type=F32).astype(o_ref.dtype)


def _mixer_proj(x, mods, l, w, out_dtype):
    n_groups, rows, d = x.shape
    n = w.shape[1]
    return pl.pallas_call(
        _proj_kernel,
        grid=(n_groups, rows // ROW_TILE),
        in_specs=[
            pl.BlockSpec((None, ROW_TILE, d), lambda g, i: (g, i, 0)),
            _mod_spec(mods, l),
            _resident(w.shape),
        ],
        out_specs=pl.BlockSpec((None, ROW_TILE, n), lambda g, i: (g, i, 0)),
        out_shape=jax.ShapeDtypeStruct((n_groups, rows, n), out_dtype),
        compiler_params=_params("arbitrary", "arbitrary"),
        name="mixer_proj",
    )(x, mods, w)


def _pair_queries(q2):
    lane = lax.broadcasted_iota(jnp.int32, q2.shape, 1)
    zero = jnp.zeros_like(q2)
    qs = q2 * jnp.asarray(NA_HEAD_DIM ** -0.5, q2.dtype)
    return jnp.concatenate([jnp.where(lane < NA_HEAD_DIM, qs, zero),
                            jnp.where(lane < NA_HEAD_DIM, zero, qs)], axis=0)


def _unpair(o2):
    n = o2.shape[0] // 2
    lane = lax.broadcasted_iota(jnp.int32, (n, PAIR), 1)
    return jnp.where(lane < NA_HEAD_DIM, o2[:n], o2[n:])


def _scores(q, k):
    return lax.dot_general(q, k, (((1,), (1,)), ((), ())), preferred_element_type=F32)


def _na_kernel(q_ref, k_ref, v_ref, kc_ref, vc_ref, *rest, n_rows):
    bias_refs = rest[:NA_ROWS_PER_STEP]
    o_ref = rest[NA_ROWS_PER_STEP]
    scr = rest[NA_ROWS_PER_STEP + 1:]
    n_loc = WIN_ROWS * GRID_W
    n_pairs = NA_HEADS // 2
    starts = []
    for i in range(NA_ROWS_PER_STEP):
        r = pl.program_id(1) * NA_ROWS_PER_STEP + i
        r0 = jnp.clip(r - WIN_ROWS // 2, 0, n_rows - WIN_ROWS)
        start = pl.multiple_of(r0 * GRID_W, GRID_W)
        starts.append(start)
        s_scr = scr[2 * i]
        for p in range(n_pairs):
            sl = slice(PAIR * p, PAIR * (p + 1))
            q = _pair_queries(q_ref[i * GRID_W:(i + 1) * GRID_W, sl])
            s_scr[p, :, :n_loc] = _scores(q, k_ref[pl.ds(start, n_loc), sl]) + bias_refs[i][p]
            s_scr[p, :, n_loc:] = _scores(q, kc_ref[:, sl])
    for i in range(NA_ROWS_PER_STEP):
        s_scr, p_scr = scr[2 * i], scr[2 * i + 1]
        start = starts[i]
        s = s_scr[...]
        e = jnp.exp(s - jnp.max(s, axis=-1, keepdims=True))
        p_scr[...] = e.astype(BF16)
        inv = 1.0 / jnp.sum(e, axis=-1, keepdims=True)
        for p in range(n_pairs):
            sl = slice(PAIR * p, PAIR * (p + 1))
            o2 = (jnp.dot(p_scr[p, :, :n_loc], v_ref[pl.ds(start, n_loc), sl], preferred_element_type=F32)
                  + jnp.dot(p_scr[p, :, n_loc:], vc_ref[:, sl], preferred_element_type=F32))
            o_ref[i * GRID_W:(i + 1) * GRID_W, sl] = _unpair(o2 * inv[p]).astype(o_ref.dtype)


def _na_attention(qkv, bias, idx, n_lat, ctx_len):
    _, rows, _ = qkv.shape
    d = D_MODEL
    n_rows = rows // GRID_W
    half = WIN_ROWS // 2
    nr = NA_ROWS_PER_STEP
    assert n_rows % nr == 0

    def variant(r):
        return r - jnp.clip(r - half, 0, n_rows - WIN_ROWS)

    n_keys = WIN_ROWS * GRID_W + ctx_len
    scratch = []
    for _ in range(nr):
        scratch += [pltpu.VMEM((NA_HEADS // 2, PAIR, n_keys), F32), pltpu.VMEM((NA_HEADS // 2, PAIR, n_keys), BF16)]
    return pl.pallas_call(
        functools.partial(_na_kernel, n_rows=n_rows),
        grid=(n_lat, n_rows // nr),
        in_specs=[
            pl.BlockSpec((None, nr * GRID_W, d), lambda b, t: (b, t, 0)),
            pl.BlockSpec((None, rows, d), lambda b, t: (b, 0, 1)),
            pl.BlockSpec((None, rows, d), lambda b, t: (b, 0, 2)),
            pl.BlockSpec((None, ctx_len, d), lambda b, t: (n_lat, b, 1)),
            pl.BlockSpec((None, ctx_len, d), lambda b, t: (n_lat, b, 2)),
        ] + [pl.BlockSpec((None, None) + bias.shape[2:], lambda b, t, i=i: (idx, variant(t * nr + i), 0, 0, 0))
             for i in range(nr)],
        out_specs=pl.BlockSpec((None, nr * GRID_W, d), lambda b, t: (b, t, 0)),
        out_shape=jax.ShapeDtypeStruct((n_lat, rows, d), BF16),
        scratch_shapes=scratch,
        compiler_params=_params("arbitrary", "arbitrary"),
        name="na_attention",
    )(qkv, qkv, qkv, qkv, qkv, *([bias] * nr))


def _ctx_attn_kernel(q_ref, k_ref, v_ref, o_ref):
    for p in range(NA_HEADS // 2):
        sl = slice(PAIR * p, PAIR * (p + 1))
        s = _scores(_pair_queries(q_ref[:, sl]), k_ref[:, sl])
        e = jnp.exp(s - jnp.max(s, axis=-1, keepdims=True))
        pr = (e * (1.0 / jnp.sum(e, axis=-1, keepdims=True))).astype(BF16)
        o_ref[:, sl] = _unpair(jnp.dot(pr, v_ref[:, sl], preferred_element_type=F32)).astype(o_ref.dtype)


def _ctx_attention(qkv, n_lat, ctx_len):
    _, rows, _ = qkv.shape
    d = D_MODEL
    return pl.pallas_call(
        _ctx_attn_kernel,
        grid=(n_lat,),
        in_specs=[pl.BlockSpec((None, ctx_len, d), lambda b, c=c: (n_lat, b, c)) for c in range(3)],
        out_specs=pl.BlockSpec((None, ctx_len, d), lambda b: (0, b, 0)),
        out_shape=jax.ShapeDtypeStruct((1, rows, d), BF16),
        compiler_params=_params("arbitrary"),
        name="ctx_attention",
    )(qkv, qkv, qkv)


def _na_bias_table(rpb):
    col = jnp.arange(GRID_W)
    col_start = jnp.clip(col - WIN_COLS // 2, 0, GRID_W - WIN_COLS)
    kcol = col[None, :]
    col_in = (kcol >= col_start[:, None]) & (kcol < col_start[:, None] + WIN_COLS)
    dcol = jnp.clip(kcol - col[:, None], 1 - WIN_COLS, WIN_COLS - 1) + WIN_COLS - 1
    drow = jnp.arange(WIN_ROWS)[None, :] - jnp.arange(WIN_ROWS)[:, None] + WIN_ROWS - 1
    hot_c = (dcol[None] == jnp.arange(2 * WIN_COLS - 1)[:, None, None]).astype(F32)
    hot_d = (drow[:, :, None] == jnp.arange(2 * WIN_ROWS - 1)).astype(F32)
    tbl = jnp.einsum('nhdc,vjd,cqk->nvhqjk', rpb.astype(F32), hot_d, hot_c, precision=lax.Precision.HIGHEST)
    tbl = jnp.where(col_in[:, None, :], tbl, NEG_INF)
    return tbl.reshape(rpb.shape[0], WIN_ROWS, NA_HEADS // 2, 2 * GRID_W, WIN_ROWS * GRID_W)


def _lru_tile_starts():
    starts = []
    for j in range(D_RNN // LRU_TILE):
        lo = (LRU_TILE * j // LRU_BLOCK_W) * LRU_BLOCK_W
        hi = ((LRU_TILE * (j + 1) - 1) // LRU_BLOCK_W + 1) * LRU_BLOCK_W
        ks = min(lo // LANES * LANES, D_RNN - LRU_KW)
        assert ks <= lo and hi <= ks + LRU_KW
        starts.append(ks)
    return starts


def _shift_rows(x, s, row):
    n = x.shape[0]
    if s > 0:
        return jnp.where(row >= s, pltpu.roll(x, s, 0), 0.0)
    return jnp.where(row < n + s, pltpu.roll(x, n + s, 0), 0.0)


def _dwconv(u, w_ref, row, left):
    acc = None
    for i in range(w_ref.shape[0]):
        s = left - i
        term = w_ref[i:i + 1, :] * (u if s == 0 else _shift_rows(u, s, row))
        acc = term if acc is None else acc + term
    return acc


def _lru_kernel(kb_ref, u0_ref, u1_ref, u2_ref, gate_ref, w0_ref, w1_ref, w2_ref, bg_ref,
                cw0_ref, cw1_ref, cw2_ref, cb0_ref, cb1_ref, cb2_ref, lam_ref, h0_ref,
                y_ref, hfin_ref, uc_scr, a_scr, b_scr, pa_scr, pb_scr, cin_scr, y_scr, l1a_scr, l1b_scr):
    t_len = gate_ref.shape[0]
    n_tiles = t_len // SUBLANES
    left = LRU_CONV_W // 2
    row = lax.broadcasted_iota(jnp.int32, (t_len, 1), 0)

    j = pl.program_id(1)
    kb = kb_ref[j]
    kb_prev = kb_ref[jnp.maximum(j - 1, 0)]

    def conv_store(u_ref, cw_ref, cb_ref, blk):
        uc_scr[blk] = _dwconv(u_ref[...], cw_ref, row, left) + cb_ref[...]

    @pl.when(j == 0)
    def _():
        conv_store(u0_ref, cw0_ref, cb0_ref, kb)
        conv_store(u1_ref, cw1_ref, cb1_ref, kb + 1)
        conv_store(u2_ref, cw2_ref, cb2_ref, kb + 2)

    @pl.when(jnp.logical_and(j > 0, kb > kb_prev))
    def _():
        conv_store(u2_ref, cw2_ref, cb2_ref, kb + 2)

    uc = jnp.concatenate([uc_scr[kb + w] for w in range(3)], axis=1)
    wt = jnp.concatenate([jnp.concatenate([w_ref[q] for q in range(4)], axis=1)
                          for w_ref in (w0_ref, w1_ref, w2_ref)], axis=0)
    bg = jnp.concatenate([bg_ref[q:q + 1, :] for q in range(4)], axis=1)
    gates = jnp.dot(uc.astype(BF16), wt, preferred_element_type=F32) + bg
    ut = uc_scr[j]

    for d in range(2):
        r = jax.nn.sigmoid(gates[:, (2 * d) * LRU_TILE:(2 * d + 1) * LRU_TILE])
        i = jax.nn.sigmoid(gates[:, (2 * d + 1) * LRU_TILE:(2 * d + 2) * LRU_TILE])
        neg_lam = -lam_ref[d:d + 1, :]
        softplus = jnp.maximum(neg_lam, 0.0) + jnp.log1p(jnp.exp(-jnp.abs(neg_lam)))
        a = jnp.exp(-LRU_C * r * softplus)
        a_scr[d] = a
        v = 1.0 - a * a
        b_scr[d] = (v * lax.rsqrt(jnp.maximum(v, SQRT_FLOOR))) * (i * ut)

    rowk = lax.broadcasted_iota(jnp.int32, (n_tiles, 1), 0)
    for d, rev in ((0, False), (1, True)):
        acc_a = acc_b = None
        for n in range(SUBLANES):
            i = SUBLANES - 1 - n if rev else n
            ai = a_scr[d, pl.ds(i, n_tiles, stride=SUBLANES), :]
            bi = b_scr[d, pl.ds(i, n_tiles, stride=SUBLANES), :]
            if n == 0:
                acc_a, acc_b = ai, bi
            else:
                acc_b = ai * acc_b + bi
                acc_a = ai * acc_a
            pa_scr[d, i] = acc_a
            pb_scr[d, i] = acc_b
        h0 = h0_ref[d:d + 1, :]
        if n_tiles >= SUBLANES * SUBLANES:
            n2 = n_tiles // SUBLANES
            l1a_scr[...] = acc_a
            l1b_scr[...] = acc_b
            row2 = lax.broadcasted_iota(jnp.int32, (n2, 1), 0)
            p2a, p2b = [None] * SUBLANES, [None] * SUBLANES
            ca = cb = None
            for n in range(SUBLANES):
                i = SUBLANES - 1 - n if rev else n
                ai = l1a_scr[pl.ds(i, n2, stride=SUBLANES), :]
                bi = l1b_scr[pl.ds(i, n2, stride=SUBLANES), :]
                if n == 0:
                    ca, cb = ai, bi
                else:
                    cb = ai * cb + bi
                    ca = ai * ca
                p2a[i], p2b[i] = ca, cb
            s = 1
            while s < n2:
                if rev:
                    valid = row2 < n2 - s
                    a_sh, b_sh = pltpu.roll(ca, n2 - s, 0), pltpu.roll(cb, n2 - s, 0)
                else:
                    valid = row2 >= s
                    a_sh, b_sh = pltpu.roll(ca, s, 0), pltpu.roll(cb, s, 0)
                cb = jnp.where(valid, ca * b_sh + cb, cb)
                ca = jnp.where(valid, ca * a_sh, ca)
                s *= 2
            st2 = ca * h0 + cb
            if rev:
                c2 = jnp.where(row2 < n2 - 1, pltpu.roll(st2, n2 - 1, 0), h0)
                hfin_ref[d:d + 1, :] = st2[0:1, :]
            else:
                c2 = jnp.where(row2 >= 1, pltpu.roll(st2, 1, 0), h0)
                hfin_ref[d:d + 1, :] = st2[n2 - 1:n2, :]
            prev = c2
            for n in range(SUBLANES):
                i = SUBLANES - 1 - n if rev else n
                cin_scr[d, pl.ds(i, n2, stride=SUBLANES), :] = prev
                prev = p2a[i] * c2 + p2b[i]
        else:
            s = 1
            while s < n_tiles:
                if rev:
                    valid = rowk < n_tiles - s
                    a_sh, b_sh = pltpu.roll(acc_a, n_tiles - s, 0), pltpu.roll(acc_b, n_tiles - s, 0)
                else:
                    valid = rowk >= s
                    a_sh, b_sh = pltpu.roll(acc_a, s, 0), pltpu.roll(acc_b, s, 0)
                acc_b = jnp.where(valid, acc_a * b_sh + acc_b, acc_b)
                acc_a = jnp.where(valid, acc_a * a_sh, acc_a)
                s *= 2
            state = acc_a * h0 + acc_b
            if rev:
                cin_scr[d] = jnp.where(rowk < n_tiles - 1, pltpu.roll(state, n_tiles - 1, 0), h0)
                hfin_ref[d:d + 1, :] = state[0:1, :]
            else:
                cin_scr[d] = jnp.where(rowk >= 1, pltpu.roll(state, 1, 0), h0)
                hfin_ref[d:d + 1, :] = state[n_tiles - 1:n_tiles, :]

    for i in range(SUBLANES):
        h_sum = (pa_scr[0, i] * cin_scr[0] + pb_scr[0, i]) + (pa_scr[1, i] * cin_scr[1] + pb_scr[1, i])
        gi = gate_ref[pl.ds(i, n_tiles, stride=SUBLANES), :]
        y_scr[pl.ds(i, n_tiles, stride=SUBLANES), :] = jax.nn.gelu(gi) * h_sum
    y_ref[...] = y_scr[...].astype(y_ref.dtype)


def _lru_scan(proj, params, h0, n_seq, seq_len, group0, per_group):
    kb, dense, bg, conv_w, conv_b, lam = params
    n_tiles = D_RNN // LRU_TILE
    col0 = D_RNN // LRU_TILE
    rows = proj.shape[1]

    def seq_block(s):
        return group0 + s // per_group, s % per_group

    def u_spec(off):
        return pl.BlockSpec((None, seq_len, LRU_TILE),
                            lambda s, j, kb_ref: (*seq_block(s), col0 + kb_ref[j] + off))

    def win_spec(n_rows, off):
        return pl.BlockSpec((n_rows, LRU_TILE), lambda s, j, kb_ref: (0, kb_ref[j] + off))

    def tile_spec(n_rows):
        return pl.BlockSpec((n_rows, LRU_TILE), lambda s, j, kb_ref: (0, j))

    def w_spec(off):
        return pl.BlockSpec((4, LRU_TILE, LRU_TILE), lambda s, j, kb_ref: (0, kb_ref[j] + off, j))

    grid_spec = pltpu.PrefetchScalarGridSpec(
        num_scalar_prefetch=1,
        grid=(n_seq, n_tiles),
        in_specs=[
            u_spec(0), u_spec(1), u_spec(2),
            pl.BlockSpec((None, seq_len, LRU_TILE), lambda s, j, kb_ref: (*seq_block(s), j)),
            w_spec(0), w_spec(1), w_spec(2),
            tile_spec(4),
            win_spec(LRU_CONV_W, 0), win_spec(LRU_CONV_W, 1), win_spec(LRU_CONV_W, 2),
            win_spec(1, 0), win_spec(1, 1), win_spec(1, 2),
            tile_spec(2),
            pl.BlockSpec((None, 2, LRU_TILE), lambda s, j, kb_ref: (s, 0, j)),
        ],
        out_specs=[
            pl.BlockSpec((None, seq_len, LRU_TILE), lambda s, j, kb_ref: (s // per_group, s % per_group, j)),
            pl.BlockSpec((None, 2, LRU_TILE), lambda s, j, kb_ref: (s, 0, j)),
        ],
        scratch_shapes=[
            pltpu.VMEM((n_tiles, seq_len, LRU_TILE), F32),
            pltpu.VMEM((2, seq_len, LRU_TILE), F32),
            pltpu.VMEM((2, seq_len, LRU_TILE), F32),
            pltpu.VMEM((2, SUBLANES, seq_len // SUBLANES, LRU_TILE), F32),
            pltpu.VMEM((2, SUBLANES, seq_len // SUBLANES, LRU_TILE), F32),
            pltpu.VMEM((2, seq_len // SUBLANES, LRU_TILE), F32),
            pltpu.VMEM((seq_len, LRU_TILE), F32),
            pltpu.VMEM((seq_len // SUBLANES, LRU_TILE), F32),
            pltpu.VMEM((seq_len // SUBLANES, LRU_TILE), F32),
        ],
    )
    return pl.pallas_call(
        _lru_kernel,
        grid_spec=grid_spec,
        out_shape=[jax.ShapeDtypeStruct((n_seq // per_group, rows, D_RNN), BF16),
                   jax.ShapeDtypeStruct((n_seq, 2, D_RNN), F32)],
        compiler_params=_params("arbitrary", "arbitrary"),
        name="rglru_scan",
    )(kb, proj, proj, proj, proj, dense, dense, dense, bg, conv_w, conv_w, conv_w,
      conv_b, conv_b, conv_b, lam, h0)


def _lru_params(conv_w, conv_b, w_gates, b_gates, lam):
    bw, nb = LRU_BLOCK_W, LRU_BLOCKS
    w = jnp.transpose(w_gates.reshape(4, nb, bw, bw), (0, 2, 1, 3))
    w = jnp.pad(w, ((0, 0), (0, 0), (0, 0), (0, D_RNN))).reshape(4, bw, nb * (bw + D_RNN))
    w = w[:, :, :nb * D_RNN].reshape(4, bw, nb, D_RNN)
    dense = jnp.transpose(w, (0, 2, 1, 3)).reshape(4, D_RNN, D_RNN).astype(BF16)
    kb = jnp.asarray([ks // LANES for ks in _lru_tile_starts()], jnp.int32)
    return kb, dense, b_gates.reshape(4, D_RNN), conv_w, conv_b.reshape(1, D_RNN), lam


def _sc_kernel(xm_ref, xp_ref, xn_ref, m_ref, w1_ref, w2_ref, cw_ref, g_ref, b_ref, o_ref, h_ref,
               *, n_lat, ctx_len, rows):
    tm = xm_ref.shape[0]
    ext = tm + 2 * HALO
    shift, scale, gate = _mod_rows(m_ref, 1)

    def mod(x):
        return (x * (1 + scale) + shift).astype(BF16)

    h_ref[0:HALO, :] = mod(xp_ref[...])
    h_ref[HALO:HALO + tm, :] = mod(xm_ref[...])
    h_ref[HALO + tm:ext, :] = mod(xn_ref[...])

    pos = pl.program_id(1) * tm + lax.broadcasted_iota(jnp.int32, (tm, 1), 0)
    is_ctx = pl.program_id(0) == n_lat
    seq_pos = jnp.where(is_ctx, pos & (ctx_len - 1), pos)
    seq_last = jnp.where(is_ctx, ctx_len - 1, rows - 1)
    has_prev = seq_pos != 0
    has_next = seq_pos != seq_last

    acc = None
    for c in range(w1_ref.shape[0]):
        t = jnp.dot(h_ref[...], w1_ref[c], preferred_element_type=F32)
        bgate = t[HALO:HALO + tm, :SC_CHUNK]
        w = t[:, SC_CHUNK:2 * SC_CHUNK] * t[:, 2 * SC_CHUNK:]
        w_prev = pltpu.roll(w, 1, 0)[HALO:HALO + tm]
        w_next = pltpu.roll(w, ext - 1, 0)[HALO:HALO + tm]
        cw = cw_ref[c]
        conv = (cw[0:1, :] * jnp.where(has_prev, w_prev, 0.0) + cw[1:2, :] * w[HALO:HALO + tm]
                + cw[2:3, :] * jnp.where(has_next, w_next, 0.0))
        y = jnp.dot((bgate * conv).astype(BF16), w2_ref[c], preferred_element_type=F32)
        acc = y if acc is None else acc + y
    t = ALPHA * xm_ref[...] + gate * acc
    o_ref[...] = _layer_norm(t, g_ref[...], b_ref[...])


def _sc_mixer(x, mods, l, w1, w2, cw, ln_g, ln_b, n_groups, n_lat, ctx_len):
    _, rows, d = x.shape
    tm = ROW_TILE
    halo_per_tile = tm // HALO
    n_halo = rows // HALO
    assert ctx_len & (ctx_len - 1) == 0 and SC_CONV_W == 3
    return pl.pallas_call(
        functools.partial(_sc_kernel, n_lat=n_lat, ctx_len=ctx_len, rows=rows),
        grid=(n_groups, rows // tm),
        in_specs=[
            pl.BlockSpec((None, tm, d), lambda g, i: (g, i, 0)),
            pl.BlockSpec((None, HALO, d), lambda g, i: (g, jnp.maximum(i * halo_per_tile - 1, 0), 0)),
            pl.BlockSpec((None, HALO, d), lambda g, i: (g, jnp.minimum((i + 1) * halo_per_tile, n_halo - 1), 0)),
            _mod_spec(mods, l),
            _resident(w1.shape),
            _resident(w2.shape),
            _resident(cw.shape),
            _ln_spec(ln_g, l, 1),
            _ln_spec(ln_b, l, 1),
        ],
        out_specs=pl.BlockSpec((None, tm, d), lambda g, i: (g, i, 0)),
        out_shape=jax.ShapeDtypeStruct((n_groups, rows, d), F32),
        scratch_shapes=[pltpu.VMEM((tm + 2 * HALO, d), BF16)],
        compiler_params=_params("arbitrary", "arbitrary"),
        name="sc_mixer",
    )(x, x, x, mods, w1, w2, cw, ln_g, ln_b)


def _chunk_cols(w, parts, chunk):
    *lead, k, pn = w.shape
    n = pn // parts
    w = w.reshape(*lead, k, parts, n // chunk, chunk)
    nl = len(lead)
    w = jnp.transpose(w, tuple(range(nl)) + (nl + 2, nl, nl + 1, nl + 3))
    return w.reshape(*lead, n // chunk, k, parts * chunk)


def kernel(x, c, ctx, c_ctx, mod_w, mod_b, ln_g, ln_b, ffn_w_in, ffn_w_out, na_w_qkv, na_w_o, na_rpb,
           lru_w_in, lru_conv_w, lru_conv_b, lru_w_gates, lru_b_gates, lru_lambda, lru_w_out,
           sc_w_in, sc_conv_w, sc_w_out):
    n_lat, rows, d = x.shape
    ctx_len = ctx.shape[1]
    assert d == D_MODEL and n_lat * ctx_len == rows and n_lat + 1 <= MOD_ROWS
    assert rows % ROW_TILE == 0 and rows % GRID_W == 0 and rows // GRID_W >= WIN_ROWS
    n_all = n_lat + 1

    x_ctx = ctx.reshape(1, rows, d)
    cond = jnp.concatenate([c, c_ctx[None], jnp.zeros((MOD_ROWS - n_all, d), F32)], axis=0)
    mods = _modulation(cond, mod_w, mod_b).reshape(DEPTH, MOD_ROWS, N_MOD, d)
    ln_g = ln_g.reshape(DEPTH * 3, 1, d)
    ln_b = ln_b.reshape(DEPTH * 3, 1, d)

    ffn_w1 = _ffn_w1_layout(ffn_w_in.reshape(DEPTH * 2, d, 2 * D_FF))
    ffn_w2 = ffn_w_out.reshape(DEPTH * 2, D_FF // FF_CHUNK, FF_CHUNK, d).astype(BF16)

    na_bias = _na_bias_table(na_rpb)

    xs = x
    for l in range(DEPTH):
        kind = l % N_MIXERS
        idx = l // N_MIXERS
        ctx_out = l < DEPTH - 1
        ctx_in = ctx_out or kind != 2
        n_in = n_all if ctx_in else n_lat
        n_out = n_all if ctx_out else n_lat

        first_ctx = x_ctx if (l == 0 and ctx_in) else None
        xs = _ffn_half(xs, first_ctx, mods, l, 0, ffn_w1, ffn_w2, ln_g, ln_b, n_in)

        if kind == 0:
            qkv = _mixer_proj(xs, mods, l, na_w_qkv[idx].astype(BF16), BF16)
            a_lat = _na_attention(qkv, na_bias, idx, n_lat, ctx_len)
            a_ctx = _ctx_attention(qkv, n_lat, ctx_len) if ctx_out else None
            mixer = (a_lat, a_ctx, na_w_o[idx].astype(BF16))
        elif kind == 1:
            proj = _mixer_proj(xs, mods, l, lru_w_in[idx].astype(BF16), F32)
            params = _lru_params(lru_conv_w[idx], lru_conv_b[idx], lru_w_gates[idx], lru_b_gates[idx],
                                 lru_lambda[idx])
            zeros = jnp.zeros((n_lat, 2, D_RNN), F32)
            a_ctx, h_ctx = _lru_scan(proj, params, zeros, n_lat, ctx_len, n_lat, n_lat)
            a_lat, _ = _lru_scan(proj, params, h_ctx, n_lat, rows, 0, 1)
            mixer = (a_lat, a_ctx if ctx_out else None, lru_w_out[idx].astype(BF16))
        else:
            w1 = _chunk_cols(sc_w_in[idx], 3, SC_CHUNK).astype(BF16)
            w2 = sc_w_out[idx].reshape(d // SC_CHUNK, SC_CHUNK, d).astype(BF16)
            cw = jnp.transpose(sc_conv_w[idx].reshape(SC_CONV_W, d // SC_CHUNK, SC_CHUNK), (1, 0, 2))
            xs = _sc_mixer(xs, mods, l, w1, w2, cw, ln_g, ln_b, n_out, n_lat, ctx_len)
            mixer = None

        xs = _ffn_half(xs, None, mods, l, 2, ffn_w1, ffn_w2, ln_g, ln_b, n_out, mixer)
    return xs
```

```python
import functools

import jax
import jax.numpy as jnp
from jax import lax
from jax.experimental import pallas as pl
from jax.experimental.pallas import tpu as pltpu

F32 = jnp.float32
BF16 = jnp.bfloat16

D_MODEL = 1024
DEPTH = 4
GRID_W = 64
N_MIXERS = 3
NA_HEADS = 16
NA_HEAD_DIM = D_MODEL // NA_HEADS
WIN_ROWS = 8
WIN_COLS = 16
D_RNN = 1408
LRU_BLOCKS = 16
LRU_BLOCK_W = D_RNN // LRU_BLOCKS
LRU_CONV_W = 4
LRU_C = 8.0
SQRT_FLOOR = 1e-30
SC_CONV_W = 3
D_FF = 2816
N_MOD = 9
ALPHA = (2 * DEPTH) ** 0.25
LN_EPS = 1e-5
NEG_INF = -1e30

LANES = 128
SUBLANES = 8
VMEM_LIMIT_BYTES = 56 * 1024 * 1024

ROW_TILE = 512
FF_CHUNK = 256
SC_CHUNK = 512
W1_LAYOUT_ROWS = 256
MOD_COLS = 2304
MOD_ROWS = 16
LRU_TILE = LANES
LRU_KW = 3 * LANES
HALO = SUBLANES
PAIR = 2 * NA_HEAD_DIM
NA_ROWS_PER_STEP = 4


def _params(*sem):
    return pltpu.CompilerParams(dimension_semantics=sem, vmem_limit_bytes=VMEM_LIMIT_BYTES)


def _resident(shape):
    zeros = (0,) * len(shape)
    return pl.BlockSpec(shape, lambda *_: zeros, pipeline_mode=pl.Buffered(1))


def _layer_norm(t, g, b):
    mu = jnp.mean(t, axis=-1, keepdims=True)
    d = t - mu
    var = jnp.mean(d * d, axis=-1, keepdims=True)
    return d * lax.rsqrt(var + LN_EPS) * g + b


def _mod_spec(mods, l):
    return pl.BlockSpec((None, None) + mods.shape[2:], lambda g, i: (l, g, 0, 0))


def _ln_spec(ln, l, j):
    return pl.BlockSpec((None,) + ln.shape[1:], lambda g, i: (3 * l + j, 0, 0))


def _layer_weight_spec(w, idx):
    zeros = (0,) * (w.ndim - 1)
    return pl.BlockSpec((None,) + w.shape[1:], lambda *_: (idx,) + zeros, pipeline_mode=pl.Buffered(1))


def _mod_rows(m_ref, j):
    return m_ref[3 * j:3 * j + 1, :], m_ref[3 * j + 1:3 * j + 2, :], m_ref[3 * j + 2:3 * j + 3, :]


def _mod_kernel(c_ref, w_ref, b_ref, o_ref):
    c = c_ref[...]
    s = (c * jax.nn.sigmoid(c)).astype(BF16)
    o_ref[...] = jnp.dot(s, w_ref[...].astype(BF16), preferred_element_type=F32) + b_ref[...]


def _modulation(cond, mod_w, mod_b):
    depth, d, n = mod_w.shape
    return pl.pallas_call(
        _mod_kernel,
        grid=(depth, n // MOD_COLS),
        in_specs=[
            pl.BlockSpec((MOD_ROWS, d), lambda l, j: (0, 0)),
            pl.BlockSpec((None, d, MOD_COLS), lambda l, j: (l, 0, j)),
            pl.BlockSpec((None, 1, MOD_COLS), lambda l, j: (l, 0, j)),
        ],
        out_specs=pl.BlockSpec((None, MOD_ROWS, MOD_COLS), lambda l, j: (l, 0, j)),
        out_shape=jax.ShapeDtypeStruct((depth, MOD_ROWS, n), F32),
        compiler_params=_params("arbitrary", "arbitrary"),
        name="modulation",
    )(cond, mod_w, mod_b.reshape(depth, 1, n))


def _ffn_kernel(*refs, j, n_lat, has_ctx, n_mixer):
    refs = list(refs)
    xl_ref = refs.pop(0)
    xc_ref = refs.pop(0) if has_ctx else None
    a_refs = [refs.pop(0) for _ in range(n_mixer)]
    m_ref = refs.pop(0)
    if n_mixer:
        wo_ref, g1_ref, b1_ref = refs.pop(0), refs.pop(0), refs.pop(0)
    w1_ref, w2_ref, g_ref, b_ref, o_ref, h_ref = refs
    if has_ctx:
        x = jnp.where(pl.program_id(0) == n_lat, xc_ref[...], xl_ref[...])
    else:
        x = xl_ref[...]
    if n_mixer:
        a = a_refs[0][...]
        if n_mixer == 2:
            a = jnp.where(pl.program_id(0) == n_lat, a_refs[1][...], a)
        _, _, gate1 = _mod_rows(m_ref, 1)
        y = jnp.dot(a, wo_ref[...], preferred_element_type=F32)
        x = _layer_norm(ALPHA * x + gate1 * y, g1_ref[...], b1_ref[...])
    shift, scale, gate = _mod_rows(m_ref, j)
    h_ref[...] = (x * (1 + scale) + shift).astype(BF16)
    acc = None
    for c in range(w1_ref.shape[0]):
        gu = jnp.dot(h_ref[...], w1_ref[c], preferred_element_type=F32)
        g = gu[:, :FF_CHUNK]
        u = gu[:, FF_CHUNK:]
        act = (g * jax.nn.sigmoid(g) * u).astype(BF16)
        y = jnp.dot(act, w2_ref[c], preferred_element_type=F32)
        acc = y if acc is None else acc + y
    t = ALPHA * x + (0.5 * gate) * acc
    o_ref[...] = _layer_norm(t, g_ref[...], b_ref[...])


def _ffn_half(x, x_ctx, mods, l, j, w1, w2, ln_g, ln_b, n_groups, mixer=None):
    n_lat, rows, d = x.shape
    has_ctx = x_ctx is not None
    if has_ctx:
        assert n_groups == n_lat + 1
        in_specs = [pl.BlockSpec((None, ROW_TILE, d), lambda g, i: (jnp.minimum(g, n_lat - 1), i, 0)),
                    pl.BlockSpec((None, ROW_TILE, d), lambda g, i: (0, i, 0))]
        args = [x, x_ctx]
    else:
        in_specs = [pl.BlockSpec((None, ROW_TILE, d), lambda g, i: (g, i, 0))]
        args = [x]
    n_mixer = 0
    if mixer is not None:
        a_lat, a_ctx, w_out = mixer
        n_lat, _, k = a_lat.shape
        in_specs.append(pl.BlockSpec((None, ROW_TILE, k), lambda g, i: (jnp.minimum(g, n_lat - 1), i, 0)))
        args.append(a_lat)
        n_mixer = 1
        if a_ctx is not None:
            in_specs.append(pl.BlockSpec((None, ROW_TILE, k), lambda g, i: (0, i, 0)))
            args.append(a_ctx)
            n_mixer = 2
    slab = 2 * l + j // 2
    in_specs.append(_mod_spec(mods, l))
    args.append(mods)
    if mixer is not None:
        in_specs += [_resident(w_out.shape), _ln_spec(ln_g, l, 1), _ln_spec(ln_b, l, 1)]
        args += [w_out, ln_g, ln_b]
    in_specs += [_layer_weight_spec(w1, slab), _layer_weight_spec(w2, slab),
                 _ln_spec(ln_g, l, j), _ln_spec(ln_b, l, j)]
    args += [w1, w2, ln_g, ln_b]
    return pl.pallas_call(
        functools.partial(_ffn_kernel, j=j, n_lat=n_lat, has_ctx=has_ctx, n_mixer=n_mixer),
        grid=(n_groups, rows // ROW_TILE),
        in_specs=in_specs,
        out_specs=pl.BlockSpec((None, ROW_TILE, d), lambda g, i: (g, i, 0)),
        out_shape=jax.ShapeDtypeStruct((n_groups, rows, d), F32),
        scratch_shapes=[pltpu.VMEM((ROW_TILE, d), BF16)],
        compiler_params=_params("arbitrary", "arbitrary"),
        name="ffn_half",
    )(*args)


def _w1_layout_kernel(g_ref, u_ref, o_ref):
    for c in range(o_ref.shape[0]):
        o_ref[c, :, :FF_CHUNK] = g_ref[:, FF_CHUNK * c:FF_CHUNK * (c + 1)].astype(BF16)
        o_ref[c, :, FF_CHUNK:] = u_ref[:, FF_CHUNK * c:FF_CHUNK * (c + 1)].astype(BF16)


def _ffn_w1_layout(w_in):
    n, d, _ = w_in.shape
    nc = D_FF // FF_CHUNK
    rt = W1_LAYOUT_ROWS
    return pl.pallas_call(
        _w1_layout_kernel,
        grid=(n, d // rt),
        in_specs=[pl.BlockSpec((None, rt, D_FF), lambda l, r: (l, r, 0)),
                  pl.BlockSpec((None, rt, D_FF), lambda l, r: (l, r, 1))],
        out_specs=pl.BlockSpec((None, nc, rt, 2 * FF_CHUNK), lambda l, r: (l, 0, r, 0)),
        out_shape=jax.ShapeDtypeStruct((n, nc, d, 2 * FF_CHUNK), BF16),
        compiler_params=_params("arbitrary", "arbitrary"),
        name="ffn_w1_layout",
    )(w_in, w_in)


def _proj_kernel(x_ref, m_ref, w_ref, o_ref):
    shift, scale, _ = _mod_rows(m_ref, 1)
    h = (x_ref[...] * (1 + scale) + shift).astype(BF16)
    o_ref[...] = jnp.dot(h, w_ref[...], preferred_element_type=F32).astype(o_ref.dtype)


def _mixer_proj(x, mods, l, w, out_dtype):
    n_groups, rows, d = x.shape
    n = w.shape[1]
    return pl.pallas_call(
        _proj_kernel,
        grid=(n_groups, rows // ROW_TILE),
        in_specs=[
            pl.BlockSpec((None, ROW_TILE, d), lambda g, i: (g, i, 0)),
            _mod_spec(mods, l),
            _resident(w.shape),
        ],
        out_specs=pl.BlockSpec((None, ROW_TILE, n), lambda g, i: (g, i, 0)),
        out_shape=jax.ShapeDtypeStruct((n_groups, rows, n), out_dtype),
        compiler_params=_params("arbitrary", "arbitrary"),
        name="mixer_proj",
    )(x, mods, w)


def _pair_queries(q2):
    lane = lax.broadcasted_iota(jnp.int32, q2.shape, 1)
    zero = jnp.zeros_like(q2)
    qs = q2 * jnp.asarray(NA_HEAD_DIM ** -0.5, q2.dtype)
    return jnp.concatenate([jnp.where(lane < NA_HEAD_DIM, qs, zero),
                            jnp.where(lane < NA_HEAD_DIM, zero, qs)], axis=0)


def _unpair(o2):
    n = o2.shape[0] // 2
    lane = lax.broadcasted_iota(jnp.int32, (n, PAIR), 1)
    return jnp.where(lane < NA_HEAD_DIM, o2[:n], o2[n:])


def _scores(q, k):
    return lax.dot_general(q, k, (((1,), (1,)), ((), ())), preferred_element_type=F32)


def _na_kernel(q_ref, k_ref, v_ref, kc_ref, vc_ref, *rest, n_rows):
    bias_refs = rest[:NA_ROWS_PER_STEP]
    o_ref = rest[NA_ROWS_PER_STEP]
    scr = rest[NA_ROWS_PER_STEP + 1:]
    n_loc = WIN_ROWS * GRID_W
    n_pairs = NA_HEADS // 2
    starts = []
    for i in range(NA_ROWS_PER_STEP):
        r = pl.program_id(1) * NA_ROWS_PER_STEP + i
        r0 = jnp.clip(r - WIN_ROWS // 2, 0, n_rows - WIN_ROWS)
        start = pl.multiple_of(r0 * GRID_W, GRID_W)
        starts.append(start)
        s_scr = scr[2 * i]
        for p in range(n_pairs):
            sl = slice(PAIR * p, PAIR * (p + 1))
            q = _pair_queries(q_ref[i * GRID_W:(i + 1) * GRID_W, sl])
            s_scr[p, :, :n_loc] = _scores(q, k_ref[pl.ds(start, n_loc), sl]) + bias_refs[i][p]
            s_scr[p, :, n_loc:] = _scores(q, kc_ref[:, sl])
    for i in range(NA_ROWS_PER_STEP):
        s_scr, p_scr = scr[2 * i], scr[2 * i + 1]
        start = starts[i]
        s = s_scr[...]
        e = jnp.exp(s - jnp.max(s, axis=-1, keepdims=True))
        p_scr[...] = e.astype(BF16)
        inv = 1.0 / jnp.sum(e, axis=-1, keepdims=True)
        for p in range(n_pairs):
            sl = slice(PAIR * p, PAIR * (p + 1))
            o2 = (jnp.dot(p_scr[p, :, :n_loc], v_ref[pl.ds(start, n_loc), sl], preferred_element_type=F32)
                  + jnp.dot(p_scr[p, :, n_loc:], vc_ref[:, sl], preferred_element_type=F32))
            o_ref[i * GRID_W:(i + 1) * GRID_W, sl] = _unpair(o2 * inv[p]).astype(o_ref.dtype)


def _na_attention(qkv, bias, idx, n_lat, ctx_len):
    _, rows, _ = qkv.shape
    d = D_MODEL
    n_rows = rows // GRID_W
    half = WIN_ROWS // 2
    nr = NA_ROWS_PER_STEP
    assert n_rows % nr == 0

    def variant(r):
        return r - jnp.clip(r - half, 0, n_rows - WIN_ROWS)

    n_keys = WIN_ROWS * GRID_W + ctx_len
    scratch = []
    for _ in range(nr):
        scratch += [pltpu.VMEM((NA_HEADS // 2, PAIR, n_keys), F32), pltpu.VMEM((NA_HEADS // 2, PAIR, n_keys), BF16)]
    return pl.pallas_call(
        functools.partial(_na_kernel, n_rows=n_rows),
        grid=(n_lat, n_rows // nr),
        in_specs=[
            pl.BlockSpec((None, nr * GRID_W, d), lambda b, t: (b, t, 0)),
            pl.BlockSpec((None, rows, d), lambda b, t: (b, 0, 1)),
            pl.BlockSpec((None, rows, d), lambda b, t: (b, 0, 2)),
            pl.BlockSpec((None, ctx_len, d), lambda b, t: (n_lat, b, 1)),
            pl.BlockSpec((None, ctx_len, d), lambda b, t: (n_lat, b, 2)),
        ] + [pl.BlockSpec((None, None) + bias.shape[2:], lambda b, t, i=i: (idx, variant(t * nr + i), 0, 0, 0))
             for i in range(nr)],
        out_specs=pl.BlockSpec((None, nr * GRID_W, d), lambda b, t: (b, t, 0)),
        out_shape=jax.ShapeDtypeStruct((n_lat, rows, d), BF16),
        scratch_shapes=scratch,
        compiler_params=_params("arbitrary", "arbitrary"),
        name="na_attention",
    )(qkv, qkv, qkv, qkv, qkv, *([bias] * nr))


def _ctx_attn_kernel(q_ref, k_ref, v_ref, o_ref):
    for p in range(NA_HEADS // 2):
        sl = slice(PAIR * p, PAIR * (p + 1))
        s = _scores(_pair_queries(q_ref[:, sl]), k_ref[:, sl])
        e = jnp.exp(s - jnp.max(s, axis=-1, keepdims=True))
        pr = (e * (1.0 / jnp.sum(e, axis=-1, keepdims=True))).astype(BF16)
        o_ref[:, sl] = _unpair(jnp.dot(pr, v_ref[:, sl], preferred_element_type=F32)).astype(o_ref.dtype)


def _ctx_attention(qkv, n_lat, ctx_len):
    _, rows, _ = qkv.shape
    d = D_MODEL
    return pl.pallas_call(
        _ctx_attn_kernel,
        grid=(n_lat,),
        in_specs=[pl.BlockSpec((None, ctx_len, d), lambda b, c=c: (n_lat, b, c)) for c in range(3)],
        out_specs=pl.BlockSpec((None, ctx_len, d), lambda b: (0, b, 0)),
        out_shape=jax.ShapeDtypeStruct((1, rows, d), BF16),
        compiler_params=_params("arbitrary"),
        name="ctx_attention",
    )(qkv, qkv, qkv)


def _na_bias_table(rpb):
    col = jnp.arange(GRID_W)
    col_start = jnp.clip(col - WIN_COLS // 2, 0, GRID_W - WIN_COLS)
    kcol = col[None, :]
    col_in = (kcol >= col_start[:, None]) & (kcol < col_start[:, None] + WIN_COLS)
    dcol = jnp.clip(kcol - col[:, None], 1 - WIN_COLS, WIN_COLS - 1) + WIN_COLS - 1
    drow = jnp.arange(WIN_ROWS)[None, :] - jnp.arange(WIN_ROWS)[:, None] + WIN_ROWS - 1
    hot_c = (dcol[None] == jnp.arange(2 * WIN_COLS - 1)[:, None, None]).astype(F32)
    hot_d = (drow[:, :, None] == jnp.arange(2 * WIN_ROWS - 1)).astype(F32)
    tbl = jnp.einsum('nhdc,vjd,cqk->nvhqjk', rpb.astype(F32), hot_d, hot_c, precision=lax.Precision.HIGHEST)
    tbl = jnp.where(col_in[:, None, :], tbl, NEG_INF)
    return tbl.reshape(rpb.shape[0], WIN_ROWS, NA_HEADS // 2, 2 * GRID_W, WIN_ROWS * GRID_W)


def _lru_tile_starts():
    starts = []
    for j in range(D_RNN // LRU_TILE):
        lo = (LRU_TILE * j // LRU_BLOCK_W) * LRU_BLOCK_W
        hi = ((LRU_TILE * (j + 1) - 1) // LRU_BLOCK_W + 1) * LRU_BLOCK_W
        ks = min(lo // LANES * LANES, D_RNN - LRU_KW)
        assert ks <= lo and hi <= ks + LRU_KW
        starts.append(ks)
    return starts


def _shift_rows(x, s, row):
    n = x.shape[0]
    if s > 0:
        return jnp.where(row >= s, pltpu.roll(x, s, 0), 0.0)
    return jnp.where(row < n + s, pltpu.roll(x, n + s, 0), 0.0)


def _dwconv(u, w_ref, row, left):
    acc = None
    for i in range(w_ref.shape[0]):
        s = left - i
        term = w_ref[i:i + 1, :] * (u if s == 0 else _shift_rows(u, s, row))
        acc = term if acc is None else acc + term
    return acc


def _lru_kernel(kb_ref, u0_ref, u1_ref, u2_ref, gate_ref, w0_ref, w1_ref, w2_ref, bg_ref,
                cw0_ref, cw1_ref, cw2_ref, cb0_ref, cb1_ref, cb2_ref, lam_ref, h0_ref,
                y_ref, hfin_ref, uc_scr, a_scr, b_scr, pa_scr, pb_scr, cin_scr, y_scr, l1a_scr, l1b_scr):
    t_len = gate_ref.shape[0]
    n_tiles = t_len // SUBLANES
    left = LRU_CONV_W // 2
    row = lax.broadcasted_iota(jnp.int32, (t_len, 1), 0)

    j = pl.program_id(1)
    kb = kb_ref[j]
    kb_prev = kb_ref[jnp.maximum(j - 1, 0)]

    def conv_store(u_ref, cw_ref, cb_ref, blk):
        uc_scr[blk] = _dwconv(u_ref[...], cw_ref, row, left) + cb_ref[...]

    @pl.when(j == 0)
    def _():
        conv_store(u0_ref, cw0_ref, cb0_ref, kb)
        conv_store(u1_ref, cw1_ref, cb1_ref, kb + 1)
        conv_store(u2_ref, cw2_ref, cb2_ref, kb + 2)

    @pl.when(jnp.logical_and(j > 0, kb > kb_prev))
    def _():
        conv_store(u2_ref, cw2_ref, cb2_ref, kb + 2)

    uc = jnp.concatenate([uc_scr[kb + w] for w in range(3)], axis=1)
    wt = jnp.concatenate([jnp.concatenate([w_ref[q] for q in range(4)], axis=1)
                          for w_ref in (w0_ref, w1_ref, w2_ref)], axis=0)
    bg = jnp.concatenate([bg_ref[q:q + 1, :] for q in range(4)], axis=1)
    gates = jnp.dot(uc.astype(BF16), wt, preferred_element_type=F32) + bg
    ut = uc_scr[j]

    for d in range(2):
        r = jax.nn.sigmoid(gates[:, (2 * d) * LRU_TILE:(2 * d + 1) * LRU_TILE])
        i = jax.nn.sigmoid(gates[:, (2 * d + 1) * LRU_TILE:(2 * d + 2) * LRU_TILE])
        neg_lam = -lam_ref[d:d + 1, :]
        softplus = jnp.maximum(neg_lam, 0.0) + jnp.log1p(jnp.exp(-jnp.abs(neg_lam)))
        a = jnp.exp(-LRU_C * r * softplus)
        a_scr[d] = a
        v = 1.0 - a * a
        b_scr[d] = (v * lax.rsqrt(jnp.maximum(v, SQRT_FLOOR))) * (i * ut)

    rowk = lax.broadcasted_iota(jnp.int32, (n_tiles, 1), 0)
    for d, rev in ((0, False), (1, True)):
        acc_a = acc_b = None
        for n in range(SUBLANES):
            i = SUBLANES - 1 - n if rev else n
            ai = a_scr[d, pl.ds(i, n_tiles, stride=SUBLANES), :]
            bi = b_scr[d, pl.ds(i, n_tiles, stride=SUBLANES), :]
            if n == 0:
                acc_a, acc_b = ai, bi
            else:
                acc_b = ai * acc_b + bi
                acc_a = ai * acc_a
            pa_scr[d, i] = acc_a
            pb_scr[d, i] = acc_b
        h0 = h0_ref[d:d + 1, :]
        if n_tiles >= SUBLANES * SUBLANES:
            n2 = n_tiles // SUBLANES
            l1a_scr[...] = acc_a
            l1b_scr[...] = acc_b
            row2 = lax.broadcasted_iota(jnp.int32, (n2, 1), 0)
            p2a, p2b = [None] * SUBLANES, [None] * SUBLANES
            ca = cb = None
            for n in range(SUBLANES):
                i = SUBLANES - 1 - n if rev else n
                ai = l1a_scr[pl.ds(i, n2, stride=SUBLANES), :]
                bi = l1b_scr[pl.ds(i, n2, stride=SUBLANES), :]
                if n == 0:
                    ca, cb = ai, bi
                else:
                    cb = ai * cb + bi
                    ca = ai * ca
                p2a[i], p2b[i] = ca, cb
            s = 1
            while s < n2:
                if rev:
                    valid = row2 < n2 - s
                    a_sh, b_sh = pltpu.roll(ca, n2 - s, 0), pltpu.roll(cb, n2 - s, 0)
                else:
                    valid = row2 >= s
                    a_sh, b_sh = pltpu.roll(ca, s, 0), pltpu.roll(cb, s, 0)
                cb = jnp.where(valid, ca * b_sh + cb, cb)
                ca = jnp.where(valid, ca * a_sh, ca)
                s *= 2
            st2 = ca * h0 + cb
            if rev:
                c2 = jnp.where(row2 < n2 - 1, pltpu.roll(st2, n2 - 1, 0), h0)
                hfin_ref[d:d + 1, :] = st2[0:1, :]
            else:
                c2 = jnp.where(row2 >= 1, pltpu.roll(st2, 1, 0), h0)
                hfin_ref[d:d + 1, :] = st2[n2 - 1:n2, :]
            prev = c2
            for n in range(SUBLANES):
                i = SUBLANES - 1 - n if rev else n
                cin_scr[d, pl.ds(i, n2, stride=SUBLANES), :] = prev
                prev = p2a[i] * c2 + p2b[i]
        else:
            s = 1
            while s < n_tiles:
                if rev:
                    valid = rowk < n_tiles - s
                    a_sh, b_sh = pltpu.roll(acc_a, n_tiles - s, 0), pltpu.roll(acc_b, n_tiles - s, 0)
                else:
                    valid = rowk >= s
                    a_sh, b_sh = pltpu.roll(acc_a, s, 0), pltpu.roll(acc_b, s, 0)
                acc_b = jnp.where(valid, acc_a * b_sh + acc_b, acc_b)
                acc_a = jnp.where(valid, acc_a * a_sh, acc_a)
                s *= 2
            state = acc_a * h0 + acc_b
            if rev:
                cin_scr[d] = jnp.where(rowk < n_tiles - 1, pltpu.roll(state, n_tiles - 1, 0), h0)
                hfin_ref[d:d + 1, :] = state[0:1, :]
            else:
                cin_scr[d] = jnp.where(rowk >= 1, pltpu.roll(state, 1, 0), h0)
                hfin_ref[d:d + 1, :] = state[n_tiles - 1:n_tiles, :]

    for i in range(SUBLANES):
        h_sum = (pa_scr[0, i] * cin_scr[0] + pb_scr[0, i]) + (pa_scr[1, i] * cin_scr[1] + pb_scr[1, i])
        gi = gate_ref[pl.ds(i, n_tiles, stride=SUBLANES), :]
        y_scr[pl.ds(i, n_tiles, stride=SUBLANES), :] = jax.nn.gelu(gi) * h_sum
    y_ref[...] = y_scr[...].astype(y_ref.dtype)


def _lru_scan(proj, params, h0, n_seq, seq_len, group0, per_group):
    kb, dense, bg, conv_w, conv_b, lam = params
    n_tiles = D_RNN // LRU_TILE
    col0 = D_RNN // LRU_TILE
    rows = proj.shape[1]

    def seq_block(s):
        return group0 + s // per_group, s % per_group

    def u_spec(off):
        return pl.BlockSpec((None, seq_len, LRU_TILE),
                            lambda s, j, kb_ref: (*seq_block(s), col0 + kb_ref[j] + off))

    def win_spec(n_rows, off):
        return pl.BlockSpec((n_rows, LRU_TILE), lambda s, j, kb_ref: (0, kb_ref[j] + off))

    def tile_spec(n_rows):
        return pl.BlockSpec((n_rows, LRU_TILE), lambda s, j, kb_ref: (0, j))

    def w_spec(off):
        return pl.BlockSpec((4, LRU_TILE, LRU_TILE), lambda s, j, kb_ref: (0, kb_ref[j] + off, j))

    grid_spec = pltpu.PrefetchScalarGridSpec(
        num_scalar_prefetch=1,
        grid=(n_seq, n_tiles),
        in_specs=[
            u_spec(0), u_spec(1), u_spec(2),
            pl.BlockSpec((None, seq_len, LRU_TILE), lambda s, j, kb_ref: (*seq_block(s), j)),
            w_spec(0), w_spec(1), w_spec(2),
            tile_spec(4),
            win_spec(LRU_CONV_W, 0), win_spec(LRU_CONV_W, 1), win_spec(LRU_CONV_W, 2),
            win_spec(1, 0), win_spec(1, 1), win_spec(1, 2),
            tile_spec(2),
            pl.BlockSpec((None, 2, LRU_TILE), lambda s, j, kb_ref: (s, 0, j)),
        ],
        out_specs=[
            pl.BlockSpec((None, seq_len, LRU_TILE), lambda s, j, kb_ref: (s // per_group, s % per_group, j)),
            pl.BlockSpec((None, 2, LRU_TILE), lambda s, j, kb_ref: (s, 0, j)),
        ],
        scratch_shapes=[
            pltpu.VMEM((n_tiles, seq_len, LRU_TILE), F32),
            pltpu.VMEM((2, seq_len, LRU_TILE), F32),
            pltpu.VMEM((2, seq_len, LRU_TILE), F32),
            pltpu.VMEM((2, SUBLANES, seq_len // SUBLANES, LRU_TILE), F32),
            pltpu.VMEM((2, SUBLANES, seq_len // SUBLANES, LRU_TILE), F32),
            pltpu.VMEM((2, seq_len // SUBLANES, LRU_TILE), F32),
            pltpu.VMEM((seq_len, LRU_TILE), F32),
            pltpu.VMEM((seq_len // SUBLANES, LRU_TILE), F32),
            pltpu.VMEM((seq_len // SUBLANES, LRU_TILE), F32),
        ],
    )
    return pl.pallas_call(
        _lru_kernel,
        grid_spec=grid_spec,
        out_shape=[jax.ShapeDtypeStruct((n_seq // per_group, rows, D_RNN), BF16),
                   jax.ShapeDtypeStruct((n_seq, 2, D_RNN), F32)],
        compiler_params=_params("arbitrary", "arbitrary"),
        name="rglru_scan",
    )(kb, proj, proj, proj, proj, dense, dense, dense, bg, conv_w, conv_w, conv_w,
      conv_b, conv_b, conv_b, lam, h0)


def _lru_params(conv_w, conv_b, w_gates, b_gates, lam):
    bw, nb = LRU_BLOCK_W, LRU_BLOCKS
    w = jnp.transpose(w_gates.astype(BF16).reshape(4, nb, bw, bw), (0, 2, 1, 3))
    w = jnp.pad(w, ((0, 0), (0, 0), (0, 0), (0, D_RNN))).reshape(4, bw, nb * (bw + D_RNN))
    w = w[:, :, :nb * D_RNN].reshape(4, bw, nb, D_RNN)
    dense = jnp.transpose(w, (0, 2, 1, 3)).reshape(4, D_RNN, D_RNN)
    kb = jnp.asarray([ks // LANES for ks in _lru_tile_starts()], jnp.int32)
    return kb, dense, b_gates.reshape(4, D_RNN), conv_w, conv_b.reshape(1, D_RNN), lam


def _sc_kernel(xm_ref, xp_ref, xn_ref, m_ref, w1_ref, w2_ref, cw_ref, g_ref, b_ref, o_ref, h_ref,
               *, n_lat, ctx_len, rows):
    tm = xm_ref.shape[0]
    ext = tm + 2 * HALO
    shift, scale, gate = _mod_rows(m_ref, 1)

    def mod(x):
        return (x * (1 + scale) + shift).astype(BF16)

    h_ref[0:HALO, :] = mod(xp_ref[...])
    h_ref[HALO:HALO + tm, :] = mod(xm_ref[...])
    h_ref[HALO + tm:ext, :] = mod(xn_ref[...])

    pos = pl.program_id(1) * tm + lax.broadcasted_iota(jnp.int32, (tm, 1), 0)
    is_ctx = pl.program_id(0) == n_lat
    seq_pos = jnp.where(is_ctx, pos & (ctx_len - 1), pos)
    seq_last = jnp.where(is_ctx, ctx_len - 1, rows - 1)
    has_prev = seq_pos != 0
    has_next = seq_pos != seq_last

    acc = None
    for c in range(w1_ref.shape[0]):
        t = jnp.dot(h_ref[...], w1_ref[c], preferred_element_type=F32)
        bgate = t[HALO:HALO + tm, :SC_CHUNK]
        w = t[:, SC_CHUNK:2 * SC_CHUNK] * t[:, 2 * SC_CHUNK:]
        w_prev = pltpu.roll(w, 1, 0)[HALO:HALO + tm]
        w_next = pltpu.roll(w, ext - 1, 0)[HALO:HALO + tm]
        cw = cw_ref[c]
        conv = (cw[0:1, :] * jnp.where(has_prev, w_prev, 0.0) + cw[1:2, :] * w[HALO:HALO + tm]
                + cw[2:3, :] * jnp.where(has_next, w_next, 0.0))
        y = jnp.dot((bgate * conv).astype(BF16), w2_ref[c], preferred_element_type=F32)
        acc = y if acc is None else acc + y
    t = ALPHA * xm_ref[...] + gate * acc
    o_ref[...] = _layer_norm(t, g_ref[...], b_ref[...])


def _sc_mixer(x, mods, l, w1, w2, cw, ln_g, ln_b, n_groups, n_lat, ctx_len):
    _, rows, d = x.shape
    tm = ROW_TILE
    halo_per_tile = tm // HALO
    n_halo = rows // HALO
    assert ctx_len & (ctx_len - 1) == 0 and SC_CONV_W == 3
    return pl.pallas_call(
        functools.partial(_sc_kernel, n_lat=n_lat, ctx_len=ctx_len, rows=rows),
        grid=(n_groups, rows // tm),
        in_specs=[
            pl.BlockSpec((None, tm, d), lambda g, i: (g, i, 0)),
            pl.BlockSpec((None, HALO, d), lambda g, i: (g, jnp.maximum(i * halo_per_tile - 1, 0), 0)),
            pl.BlockSpec((None, HALO, d), lambda g, i: (g, jnp.minimum((i + 1) * halo_per_tile, n_halo - 1), 0)),
            _mod_spec(mods, l),
            _resident(w1.shape),
            _resident(w2.shape),
            _resident(cw.shape),
            _ln_spec(ln_g, l, 1),
            _ln_spec(ln_b, l, 1),
        ],
        out_specs=pl.BlockSpec((None, tm, d), lambda g, i: (g, i, 0)),
        out_shape=jax.ShapeDtypeStruct((n_groups, rows, d), F32),
        scratch_shapes=[pltpu.VMEM((tm + 2 * HALO, d), BF16)],
        compiler_params=_params("arbitrary", "arbitrary"),
        name="sc_mixer",
    )(x, x, x, mods, w1, w2, cw, ln_g, ln_b)


def _chunk_cols(w, parts, chunk):
    *lead, k, pn = w.shape
    n = pn // parts
    w = w.reshape(*lead, k, parts, n // chunk, chunk)
    nl = len(lead)
    w = jnp.transpose(w, tuple(range(nl)) + (nl + 2, nl, nl + 1, nl + 3))
    return w.reshape(*lead, n // chunk, k, parts * chunk)


def kernel(x, c, ctx, c_ctx, mod_w, mod_b, ln_g, ln_b, ffn_w_in, ffn_w_out, na_w_qkv, na_w_o, na_rpb,
           lru_w_in, lru_conv_w, lru_conv_b, lru_w_gates, lru_b_gates, lru_lambda, lru_w_out,
           sc_w_in, sc_conv_w, sc_w_out):
    n_lat, rows, d = x.shape
    ctx_len = ctx.shape[1]
    assert d == D_MODEL and n_lat * ctx_len == rows and n_lat + 1 <= MOD_ROWS
    assert rows % ROW_TILE == 0 and rows % GRID_W == 0 and rows // GRID_W >= WIN_ROWS
    n_all = n_lat + 1

    x_ctx = ctx.reshape(1, rows, d)
    cond = jnp.concatenate([c, c_ctx[None], jnp.zeros((MOD_ROWS - n_all, d), F32)], axis=0)
    mods = _modulation(cond, mod_w, mod_b).reshape(DEPTH, MOD_ROWS, N_MOD, d)
    ln_g = ln_g.reshape(DEPTH * 3, 1, d)
    ln_b = ln_b.reshape(DEPTH * 3, 1, d)

    ffn_w1 = _ffn_w1_layout(ffn_w_in.reshape(DEPTH * 2, d, 2 * D_FF))
    ffn_w2 = ffn_w_out.reshape(DEPTH * 2, D_FF // FF_CHUNK, FF_CHUNK, d).astype(BF16)

    na_bias = _na_bias_table(na_rpb)

    xs = x
    for l in range(DEPTH):
        kind = l % N_MIXERS
        idx = l // N_MIXERS
        ctx_out = l < DEPTH - 1
        ctx_in = ctx_out or kind != 2
        n_in = n_all if ctx_in else n_lat
        n_out = n_all if ctx_out else n_lat

        first_ctx = x_ctx if (l == 0 and ctx_in) else None
        xs = _ffn_half(xs, first_ctx, mods, l, 0, ffn_w1, ffn_w2, ln_g, ln_b, n_in)

        if kind == 0:
            qkv = _mixer_proj(xs, mods, l, na_w_qkv[idx].astype(BF16), BF16)
            a_lat = _na_attention(qkv, na_bias, idx, n_lat, ctx_len)
            a_ctx = _ctx_attention(qkv, n_lat, ctx_len) if ctx_out else None
            mixer = (a_lat, a_ctx, na_w_o[idx].astype(BF16))
        elif kind == 1:
            proj = _mixer_proj(xs, mods, l, lru_w_in[idx].astype(BF16), F32)
            params = _lru_params(lru_conv_w[idx], lru_conv_b[idx], lru_w_gates[idx], lru_b_gates[idx],
                                 lru_lambda[idx])
            zeros = jnp.zeros((n_lat, 2, D_RNN), F32)
            a_ctx, h_ctx = _lru_scan(proj, params, zeros, n_lat, ctx_len, n_lat, n_lat)
            a_lat, _ = _lru_scan(proj, params, h_ctx, n_lat, rows, 0, 1)
            mixer = (a_lat, a_ctx if ctx_out else None, lru_w_out[idx].astype(BF16))
        else:
            w1 = _chunk_cols(sc_w_in[idx], 3, SC_CHUNK).astype(BF16)
            w2 = sc_w_out[idx].reshape(d // SC_CHUNK, SC_CHUNK, d).astype(BF16)
            cw = jnp.transpose(sc_conv_w[idx].reshape(SC_CONV_W, d // SC_CHUNK, SC_CHUNK), (1, 0, 2))
            xs = _sc_mixer(xs, mods, l, w1, w2, cw, ln_g, ln_b, n_out, n_lat, ctx_len)
            mixer = None

        xs = _ffn_half(xs, None, mods, l, 2, ffn_w1, ffn_w2, ln_g, ln_b, n_out, mixer)
    return xs
```
